```python
import jax, jax.numpy as jnp
from jax import lax
import numpy as np

D_MODEL = 2048
BATCH = 8
SEQ = 4096
DEPTH = 4

CHUNK = 64
N_A_LAYERS = DEPTH // 2
N_B_LAYERS = DEPTH - N_A_LAYERS
GMLP_BLOCK = 128
GMLP_HALF = 3 * D_MODEL
GMLP_GROUPS = 8
GMLP_GROUP_DIM = GMLP_HALF // GMLP_GROUPS
N_HEADS = 16
HEAD_DIM = D_MODEL // N_HEADS
Q_BLOCK = 128
D_FF = 4 * D_MODEL
N_MOD = 6
EPS = 1e-6

kernel_name = "yoco_gmlp_forgetting_attn_adaln_trunk"


def rms_norm(x, g):
    xf = x.astype(jnp.float32)
    y = xf * lax.rsqrt(jnp.mean(xf * xf, axis=-1, keepdims=True) + EPS)
    return y.astype(x.dtype) * g


def layer_norm(x, g, b):
    xf = x.astype(jnp.float32)
    mu = jnp.mean(xf, axis=-1, keepdims=True)
    var = jnp.mean(jnp.square(xf - mu), axis=-1, keepdims=True)
    y = (xf - mu) * lax.rsqrt(var + EPS)
    return y.astype(x.dtype) * g + b


def modulate(h, shift, scale):
    return h * (1.0 + scale[:, None, :]) + shift[:, None, :]


def split_heads(t):
    b, s, _ = t.shape
    return t.reshape(b, s, N_HEADS, HEAD_DIM)


def gmlp_block_mask():
    idx = np.arange(GMLP_BLOCK) // CHUNK
    return jnp.asarray(idx[None, :] <= idx[:, None])


def gmlp_mixer(h, w_in, ln_g, ln_b, ws, bs, w_out):
    b, s, _ = h.shape
    z = jax.nn.gelu(h @ w_in, approximate=False)
    u, v = jnp.split(z, 2, axis=-1)
    v = layer_norm(v, ln_g, ln_b)
    n_blk = s // GMLP_BLOCK
    v = v.reshape(b, n_blk, GMLP_BLOCK, GMLP_GROUPS, GMLP_GROUP_DIM)
    w = jnp.where(gmlp_block_mask()[None], ws, jnp.zeros((), ws.dtype))
    sv = jnp.einsum('gij,bnjgc->bnigc', w, v) + bs.T[:, :, None]
    sv = sv.reshape(b, s, GMLP_HALF)
    return (u * sv) @ w_out


def squared_relu_mlp(h, w1, w2):
    return jnp.square(jax.nn.relu(h @ w1)) @ w2


def shared_kv(x, sc, kv_norm_g, kv_ada_w, kv_ada_b, w_kv, k_norm_g, w_f, b_f):
    shift, scale = jnp.split(sc @ kv_ada_w + kv_ada_b, 2, axis=-1)
    h = modulate(rms_norm(x, kv_norm_g), shift, scale)
    k, v = jnp.split(h @ w_kv, 2, axis=-1)
    k = rms_norm(split_heads(k), k_norm_g).transpose(0, 2, 1, 3)
    v = split_heads(v).transpose(0, 2, 1, 3)
    logf = jax.nn.log_sigmoid((h @ w_f).astype(jnp.float32) + b_f.astype(jnp.float32))
    fcum = jnp.cumsum(logf, axis=1).transpose(0, 2, 1)
    return k, v, fcum


def forgetting_attention(q, k, v, fcum):
    b, nh, s, dh = q.shape
    n_blk = s // Q_BLOCK
    qb = q.reshape(b, nh, n_blk, Q_BLOCK, dh).transpose(2, 0, 1, 3, 4)
    fq = fcum.reshape(b, nh, n_blk, Q_BLOCK).transpose(2, 0, 1, 3)
    key_pos = jnp.arange(s)
    inv_sqrt = 1.0 / float(np.sqrt(dh))

    def one_block(args):
        i, q_i, f_i = args
        logits = jnp.einsum('bhqd,bhkd->bhqk', q_i, k,
                            preferred_element_type=jnp.float32) * inv_sqrt
        logits = logits + (f_i[..., :, None] - fcum[..., None, :])
        q_pos = i * Q_BLOCK + jnp.arange(Q_BLOCK)
        logits = jnp.where(key_pos[None, :] <= q_pos[:, None], logits, -jnp.inf)
        p = jax.nn.softmax(logits, axis=-1)
        return jnp.einsum('bhqk,bhkd->bhqd', p.astype(v.dtype), v)

    out = lax.map(one_block, (jnp.arange(n_blk), qb, fq))
    return out.transpose(1, 2, 0, 3, 4).reshape(b, nh, s, dh)


def attention_mixer(h, k, v, fcum, wq, q_norm_g, wo):
    b, s, _ = h.shape
    q = rms_norm(split_heads(h @ wq), q_norm_g).transpose(0, 2, 1, 3)
    o = forgetting_attention(q, k, v, fcum)
    return o.transpose(0, 2, 1, 3).reshape(b, s, D_MODEL) @ wo


def _fwd_setup_inputs(seed: int = 0) -> dict:
    key = jax.random.key(seed)
    ks = jax.random.split(key, 24)
    f32 = jnp.float32
    nrm = lambda k, shape, s: jax.random.normal(k, shape, f32) * s
    d = D_MODEL
    return {
        "x": nrm(ks[0], (BATCH, SEQ, d), 1.0),
        "c": nrm(ks[1], (BATCH, d), 1.0),
        "ada_w": nrm(ks[2], (DEPTH, d, N_MOD * d), d ** -0.5),
        "ada_b": nrm(ks[3], (DEPTH, N_MOD * d), 0.02),
        "norm_g": 1.0 + nrm(ks[4], (DEPTH, 2, d), 0.02),
        "mlp_w1": nrm(ks[5], (DEPTH, d, D_FF), d ** -0.5),
        "mlp_w2": nrm(ks[6], (DEPTH, D_FF, d), D_FF ** -0.5),
        "gmlp_w_in": nrm(ks[7], (N_A_LAYERS, d, 2 * GMLP_HALF), d ** -0.5),
        "gmlp_ln_g": 1.0 + nrm(ks[8], (N_A_LAYERS, GMLP_HALF), 0.02),
        "gmlp_ln_b": nrm(ks[9], (N_A_LAYERS, GMLP_HALF), 0.02),
        "gmlp_ws": nrm(ks[10], (N_A_LAYERS, GMLP_GROUPS, GMLP_BLOCK, GMLP_BLOCK), GMLP_BLOCK ** -0.5),
        "gmlp_bs": 1.0 + nrm(ks[11], (N_A_LAYERS, GMLP_GROUPS, GMLP_BLOCK), 0.1),
        "gmlp_w_out": nrm(ks[12], (N_A_LAYERS, GMLP_HALF, d), GMLP_HALF ** -0.5),
        "kv_norm_g": 1.0 + nrm(ks[13], (d,), 0.02),
        "kv_ada_w": nrm(ks[14], (d, 2 * d), d ** -0.5),
        "kv_ada_b": nrm(ks[15], (2 * d,), 0.02),
        "w_kv": nrm(ks[16], (d, 2 * d), d ** -0.5),
        "k_norm_g": 1.0 + nrm(ks[17], (HEAD_DIM,), 0.02),
        "w_f": nrm(ks[18], (d, N_HEADS), 0.1 * d ** -0.5),
        "b_f": jax.random.uniform(ks[19], (N_HEADS,), f32, 0.5, 5.0),
        "attn_wq": nrm(ks[20], (N_B_LAYERS, d, d), d ** -0.5),
        "q_norm_g": 1.0 + nrm(ks[21], (N_B_LAYERS, HEAD_DIM), 0.02),
        "attn_wo": nrm(ks[22], (N_B_LAYERS, d, d), d ** -0.5),
    }


def _fwd_reference(x, c, ada_w, ada_b, norm_g, mlp_w1, mlp_w2, gmlp_w_in, gmlp_ln_g,
              gmlp_ln_b, gmlp_ws, gmlp_bs, gmlp_w_out, kv_norm_g, kv_ada_w, kv_ada_b,
              w_kv, k_norm_g, w_f, b_f, attn_wq, q_norm_g, attn_wo):
    sc = jax.nn.silu(c)
    k = v = fcum = None
    for layer in range(DEPTH):
        mod = sc @ ada_w[layer] + ada_b[layer]
        sh1, sc1, g1, sh2, sc2, g2 = jnp.split(mod, N_MOD, axis=-1)
        h = modulate(rms_norm(x, norm_g[layer, 0]), sh1, sc1)
        if layer < N_A_LAYERS:
            a = layer
            y = gmlp_mixer(h, gmlp_w_in[a], gmlp_ln_g[a], gmlp_ln_b[a],
                           gmlp_ws[a], gmlp_bs[a], gmlp_w_out[a])
        else:
            if layer == N_A_LAYERS:
                k, v, fcum = shared_kv(x, sc, kv_norm_g, kv_ada_w, kv_ada_b,
                                       w_kv, k_norm_g, w_f, b_f)
            bl = layer - N_A_LAYERS
            y = attention_mixer(h, k, v, fcum, attn_wq[bl], q_norm_g[bl], attn_wo[bl])
        x = x + g1[:, None, :] * y
        h = modulate(rms_norm(x, norm_g[layer, 1]), sh2, sc2)
        x = x + g2[:, None, :] * squared_relu_mlp(h, mlp_w1[layer], mlp_w2[layer])
    return x


import jax as _jax
import jax.numpy as _jnp

TWIN_FORMAT = 'train_step'
FWD_PARAMS = ['x', 'c', 'ada_w', 'ada_b', 'norm_g', 'mlp_w1', 'mlp_w2', 'gmlp_w_in', 'gmlp_ln_g', 'gmlp_ln_b', 'gmlp_ws', 'gmlp_bs', 'gmlp_w_out', 'kv_norm_g', 'kv_ada_w', 'kv_ada_b', 'w_kv', 'k_norm_g', 'w_f', 'b_f', 'attn_wq', 'q_norm_g', 'attn_wo']
TWIN_WEIGHTS = ['ada_w', 'ada_b', 'norm_g', 'mlp_w1', 'mlp_w2', 'gmlp_w_in', 'gmlp_ln_g', 'gmlp_ln_b', 'gmlp_ws', 'gmlp_bs', 'gmlp_w_out', 'kv_norm_g', 'kv_ada_w', 'kv_ada_b', 'w_kv', 'k_norm_g', 'w_f', 'b_f', 'attn_wq', 'q_norm_g', 'attn_wo']
TWIN_DIFF_INPUT = 'x'
TWIN_INPUTS = ['x', 'c', 'ada_w', 'ada_b', 'norm_g', 'mlp_w1', 'mlp_w2', 'gmlp_w_in', 'gmlp_ln_g', 'gmlp_ln_b', 'gmlp_ws', 'gmlp_bs', 'gmlp_w_out', 'kv_norm_g', 'kv_ada_w', 'kv_ada_b', 'w_kv', 'k_norm_g', 'w_f', 'b_f', 'attn_wq', 'q_norm_g', 'attn_wo', 'loss_target', 'm_ada_w', 'm_ada_b', 'm_norm_g', 'm_mlp_w1', 'm_mlp_w2', 'm_gmlp_w_in', 'm_gmlp_ln_g', 'm_gmlp_ln_b', 'm_gmlp_ws', 'm_gmlp_bs', 'm_gmlp_w_out', 'm_kv_norm_g', 'm_kv_ada_w', 'm_kv_ada_b', 'm_w_kv', 'm_k_norm_g', 'm_w_f', 'm_b_f', 'm_attn_wq', 'm_q_norm_g', 'm_attn_wo', 'v_ada_w', 'v_ada_b', 'v_norm_g', 'v_mlp_w1', 'v_mlp_w2', 'v_gmlp_w_in', 'v_gmlp_ln_g', 'v_gmlp_ln_b', 'v_gmlp_ws', 'v_gmlp_bs', 'v_gmlp_w_out', 'v_kv_norm_g', 'v_kv_ada_w', 'v_kv_ada_b', 'v_w_kv', 'v_k_norm_g', 'v_w_f', 'v_b_f', 'v_attn_wq', 'v_q_norm_g', 'v_attn_wo']
TWIN_OUTPUTS = ['loss', 'grad_x', 'grad_ada_w', 'grad_ada_b', 'grad_norm_g', 'grad_mlp_w1', 'grad_mlp_w2', 'grad_gmlp_w_in', 'grad_gmlp_ln_g', 'grad_gmlp_ln_b', 'grad_gmlp_ws', 'grad_gmlp_bs', 'grad_gmlp_w_out', 'grad_kv_norm_g', 'grad_kv_ada_w', 'grad_kv_ada_b', 'grad_w_kv', 'grad_k_norm_g', 'grad_w_f', 'grad_b_f', 'grad_attn_wq', 'grad_q_norm_g', 'grad_attn_wo', 'delta_ada_w', 'delta_ada_b', 'delta_norm_g', 'delta_mlp_w1', 'delta_mlp_w2', 'delta_gmlp_w_in', 'delta_gmlp_ln_g', 'delta_gmlp_ln_b', 'delta_gmlp_ws', 'delta_gmlp_bs', 'delta_gmlp_w_out', 'delta_kv_norm_g', 'delta_kv_ada_w', 'delta_kv_ada_b', 'delta_w_kv', 'delta_k_norm_g', 'delta_w_f', 'delta_b_f', 'delta_attn_wq', 'delta_q_norm_g', 'delta_attn_wo', 'new_m_ada_w', 'new_m_ada_b', 'new_m_norm_g', 'new_m_mlp_w1', 'new_m_mlp_w2', 'new_m_gmlp_w_in', 'new_m_gmlp_ln_g', 'new_m_gmlp_ln_b', 'new_m_gmlp_ws', 'new_m_gmlp_bs', 'new_m_gmlp_w_out', 'new_m_kv_norm_g', 'new_m_kv_ada_w', 'new_m_kv_ada_b', 'new_m_w_kv', 'new_m_k_norm_g', 'new_m_w_f', 'new_m_b_f', 'new_m_attn_wq', 'new_m_q_norm_g', 'new_m_attn_wo', 'new_v_ada_w', 'new_v_ada_b', 'new_v_norm_g', 'new_v_mlp_w1', 'new_v_mlp_w2', 'new_v_gmlp_w_in', 'new_v_gmlp_ln_g', 'new_v_gmlp_ln_b', 'new_v_gmlp_ws', 'new_v_gmlp_bs', 'new_v_gmlp_w_out', 'new_v_kv_norm_g', 'new_v_kv_ada_w', 'new_v_kv_ada_b', 'new_v_w_kv', 'new_v_k_norm_g', 'new_v_w_f', 'new_v_b_f', 'new_v_attn_wq', 'new_v_q_norm_g', 'new_v_attn_wo']
TWIN_LEAF_KINDS = {'loss': 'loss', 'grad_x': 'grad_x', 'grad_ada_w': 'grad_w', 'grad_ada_b': 'grad_w', 'grad_norm_g': 'grad_w', 'grad_mlp_w1': 'grad_w', 'grad_mlp_w2': 'grad_w', 'grad_gmlp_w_in': 'grad_w', 'grad_gmlp_ln_g': 'grad_w', 'grad_gmlp_ln_b': 'grad_w', 'grad_gmlp_ws': 'grad_w', 'grad_gmlp_bs': 'grad_w', 'grad_gmlp_w_out': 'grad_w', 'grad_kv_norm_g': 'grad_w', 'grad_kv_ada_w': 'grad_w', 'grad_kv_ada_b': 'grad_w', 'grad_w_kv': 'grad_w', 'grad_k_norm_g': 'grad_w', 'grad_w_f': 'grad_w', 'grad_b_f': 'grad_w', 'grad_attn_wq': 'grad_w', 'grad_q_norm_g': 'grad_w', 'grad_attn_wo': 'grad_w', 'delta_ada_w': 'delta_w', 'delta_ada_b': 'delta_w', 'delta_norm_g': 'delta_w', 'delta_mlp_w1': 'delta_w', 'delta_mlp_w2': 'delta_w', 'delta_gmlp_w_in': 'delta_w', 'delta_gmlp_ln_g': 'delta_w', 'delta_gmlp_ln_b': 'delta_w', 'delta_gmlp_ws': 'delta_w', 'delta_gmlp_bs': 'delta_w', 'delta_gmlp_w_out': 'delta_w', 'delta_kv_norm_g': 'delta_w', 'delta_kv_ada_w': 'delta_w', 'delta_kv_ada_b': 'delta_w', 'delta_w_kv': 'delta_w', 'delta_k_norm_g': 'delta_w', 'delta_w_f': 'delta_w', 'delta_b_f': 'delta_w', 'delta_attn_wq': 'delta_w', 'delta_q_norm_g': 'delta_w', 'delta_attn_wo': 'delta_w', 'new_m_ada_w': 'new_m', 'new_m_ada_b': 'new_m', 'new_m_norm_g': 'new_m', 'new_m_mlp_w1': 'new_m', 'new_m_mlp_w2': 'new_m', 'new_m_gmlp_w_in': 'new_m', 'new_m_gmlp_ln_g': 'new_m', 'new_m_gmlp_ln_b': 'new_m', 'new_m_gmlp_ws': 'new_m', 'new_m_gmlp_bs': 'new_m', 'new_m_gmlp_w_out': 'new_m', 'new_m_kv_norm_g': 'new_m', 'new_m_kv_ada_w': 'new_m', 'new_m_kv_ada_b': 'new_m', 'new_m_w_kv': 'new_m', 'new_m_k_norm_g': 'new_m', 'new_m_w_f': 'new_m', 'new_m_b_f': 'new_m', 'new_m_attn_wq': 'new_m', 'new_m_q_norm_g': 'new_m', 'new_m_attn_wo': 'new_m', 'new_v_ada_w': 'new_v', 'new_v_ada_b': 'new_v', 'new_v_norm_g': 'new_v', 'new_v_mlp_w1': 'new_v', 'new_v_mlp_w2': 'new_v', 'new_v_gmlp_w_in': 'new_v', 'new_v_gmlp_ln_g': 'new_v', 'new_v_gmlp_ln_b': 'new_v', 'new_v_gmlp_ws': 'new_v', 'new_v_gmlp_bs': 'new_v', 'new_v_gmlp_w_out': 'new_v', 'new_v_kv_norm_g': 'new_v', 'new_v_kv_ada_w': 'new_v', 'new_v_kv_ada_b': 'new_v', 'new_v_w_kv': 'new_v', 'new_v_k_norm_g': 'new_v', 'new_v_w_f': 'new_v', 'new_v_b_f': 'new_v', 'new_v_attn_wq': 'new_v', 'new_v_q_norm_g': 'new_v', 'new_v_attn_wo': 'new_v'}


def _forward(args):
    return _fwd_reference(*[args[k] for k in FWD_PARAMS])


def _output_shape():
    out = _jax.eval_shape(lambda: _forward(_fwd_setup_inputs(0)))
    return out.shape, out.dtype

N_MICROBATCH = 1
ADAM_LR = 0.001
ADAM_B1 = 0.9
ADAM_B2 = 0.999
ADAM_EPS = 1e-08
ADAM_WD = 0.01
ADAM_STEP = 10
PER_EXAMPLE_BATCH_AXIS = {'x': 0, 'c': 0, 'loss_target': 0}
SHARED_INPUTS = []
_WEIGHT_DTYPES = {'ada_w': _jnp.float32, 'ada_b': _jnp.float32, 'norm_g': _jnp.float32, 'mlp_w1': _jnp.float32, 'mlp_w2': _jnp.float32, 'gmlp_w_in': _jnp.float32, 'gmlp_ln_g': _jnp.float32, 'gmlp_ln_b': _jnp.float32, 'gmlp_ws': _jnp.float32, 'gmlp_bs': _jnp.float32, 'gmlp_w_out': _jnp.float32, 'kv_norm_g': _jnp.float32, 'kv_ada_w': _jnp.float32, 'kv_ada_b': _jnp.float32, 'w_kv': _jnp.float32, 'k_norm_g': _jnp.float32, 'w_f': _jnp.float32, 'b_f': _jnp.float32, 'attn_wq': _jnp.float32, 'q_norm_g': _jnp.float32, 'attn_wo': _jnp.float32}
MOMENT_SCALE = {'ada_w': 1.533115e+01, 'ada_b': 2.863917e+01, 'norm_g': 3.949717e+01, 'mlp_w1': 9.720008e+00, 'mlp_w2': 2.115375e+01, 'gmlp_w_in': 3.306174e+00, 'gmlp_ln_g': 1.120709e+00, 'gmlp_ln_b': 4.838342e-01, 'gmlp_ws': 1.282194e+00, 'gmlp_bs': 9.074177e+00, 'gmlp_w_out': 1.224775e+01, 'kv_norm_g': 1.609257e+01, 'kv_ada_w': 9.685568e+00, 'kv_ada_b': 1.631443e+01, 'w_kv': 1.169683e+01, 'k_norm_g': 3.498337e+00, 'w_f': 2.692788e+01, 'b_f': 5.544281e+01, 'attn_wq': 1.818539e-01, 'q_norm_g': 1.775978e+00, 'attn_wo': 1.132840e+01}


def _to_microbatches(a, axis):
    t = _jnp.moveaxis(a, axis, 0)
    t = t.reshape((N_MICROBATCH, t.shape[0] // N_MICROBATCH) + t.shape[1:])
    return _jnp.moveaxis(t, 1, axis + 1)


def setup_inputs(seed: int = 0) -> dict:
    inp = _fwd_setup_inputs(seed)
    key = _jax.random.fold_in(_jax.random.key(seed), 7919)
    shape, _ = _output_shape()
    out = dict(inp)
    out["loss_target"] = _jax.random.normal(_jax.random.fold_in(key, 0), shape, _jnp.float32)
    for i, name in enumerate(TWIN_WEIGHTS):
        w = inp[name].astype(_jnp.float32)
        if MOMENT_SCALE is None:
            s = _jnp.sqrt(_jnp.mean(_jnp.square(w)) + 1e-30)
        else:
            s = MOMENT_SCALE[name]
        km, kv = _jax.random.split(_jax.random.fold_in(key, i + 1))
        out[name] = w
        out["m_" + name] = s * _jax.random.normal(km, w.shape, _jnp.float32)
        out["v_" + name] = (s * s) * _jax.random.uniform(kv, w.shape, _jnp.float32, 0.5, 1.5)
    if N_MICROBATCH > 1:
        for name, axis in PER_EXAMPLE_BATCH_AXIS.items():
            out[name] = _to_microbatches(out[name], axis)
    return {'x': out['x'], 'c': out['c'], 'ada_w': out['ada_w'], 'ada_b': out['ada_b'], 'norm_g': out['norm_g'], 'mlp_w1': out['mlp_w1'], 'mlp_w2': out['mlp_w2'], 'gmlp_w_in': out['gmlp_w_in'], 'gmlp_ln_g': out['gmlp_ln_g'], 'gmlp_ln_b': out['gmlp_ln_b'], 'gmlp_ws': out['gmlp_ws'], 'gmlp_bs': out['gmlp_bs'], 'gmlp_w_out': out['gmlp_w_out'], 'kv_norm_g': out['kv_norm_g'], 'kv_ada_w': out['kv_ada_w'], 'kv_ada_b': out['kv_ada_b'], 'w_kv': out['w_kv'], 'k_norm_g': out['k_norm_g'], 'w_f': out['w_f'], 'b_f': out['b_f'], 'attn_wq': out['attn_wq'], 'q_norm_g': out['q_norm_g'], 'attn_wo': out['attn_wo'], 'loss_target': out['loss_target'], 'm_ada_w': out['m_ada_w'], 'm_ada_b': out['m_ada_b'], 'm_norm_g': out['m_norm_g'], 'm_mlp_w1': out['m_mlp_w1'], 'm_mlp_w2': out['m_mlp_w2'], 'm_gmlp_w_in': out['m_gmlp_w_in'], 'm_gmlp_ln_g': out['m_gmlp_ln_g'], 'm_gmlp_ln_b': out['m_gmlp_ln_b'], 'm_gmlp_ws': out['m_gmlp_ws'], 'm_gmlp_bs': out['m_gmlp_bs'], 'm_gmlp_w_out': out['m_gmlp_w_out'], 'm_kv_norm_g': out['m_kv_norm_g'], 'm_kv_ada_w': out['m_kv_ada_w'], 'm_kv_ada_b': out['m_kv_ada_b'], 'm_w_kv': out['m_w_kv'], 'm_k_norm_g': out['m_k_norm_g'], 'm_w_f': out['m_w_f'], 'm_b_f': out['m_b_f'], 'm_attn_wq': out['m_attn_wq'], 'm_q_norm_g': out['m_q_norm_g'], 'm_attn_wo': out['m_attn_wo'], 'v_ada_w': out['v_ada_w'], 'v_ada_b': out['v_ada_b'], 'v_norm_g': out['v_norm_g'], 'v_mlp_w1': out['v_mlp_w1'], 'v_mlp_w2': out['v_mlp_w2'], 'v_gmlp_w_in': out['v_gmlp_w_in'], 'v_gmlp_ln_g': out['v_gmlp_ln_g'], 'v_gmlp_ln_b': out['v_gmlp_ln_b'], 'v_gmlp_ws': out['v_gmlp_ws'], 'v_gmlp_bs': out['v_gmlp_bs'], 'v_gmlp_w_out': out['v_gmlp_w_out'], 'v_kv_norm_g': out['v_kv_norm_g'], 'v_kv_ada_w': out['v_kv_ada_w'], 'v_kv_ada_b': out['v_kv_ada_b'], 'v_w_kv': out['v_w_kv'], 'v_k_norm_g': out['v_k_norm_g'], 'v_w_f': out['v_w_f'], 'v_b_f': out['v_b_f'], 'v_attn_wq': out['v_attn_wq'], 'v_q_norm_g': out['v_q_norm_g'], 'v_attn_wo': out['v_attn_wo']}


def _loss(weights, diff, rest, loss_target):
    with _jax.named_scope("forward"):
        args = {**rest, TWIN_DIFF_INPUT: diff, **{k: w.astype(_WEIGHT_DTYPES[k]) for k, w in weights.items()}}
        y = _forward(args)
    with _jax.named_scope("loss_head"):
        err = _jnp.square(y.astype(_jnp.float32) - loss_target)
        return 0.5 * _jnp.sum(_jnp.mean(err, axis=-1)) if err.ndim else 0.5 * err


def _adamw(w, g, m, v):
    m = ADAM_B1 * m + (1.0 - ADAM_B1) * g
    v = ADAM_B2 * v + (1.0 - ADAM_B2) * _jnp.square(g)
    m_hat = m / (1.0 - ADAM_B1 ** ADAM_STEP)
    v_hat = v / (1.0 - ADAM_B2 ** ADAM_STEP)
    delta = -ADAM_LR * (m_hat / (_jnp.sqrt(v_hat) + ADAM_EPS) + ADAM_WD * w)
    return delta, m, v


def reference(x, c, ada_w, ada_b, norm_g, mlp_w1, mlp_w2, gmlp_w_in, gmlp_ln_g, gmlp_ln_b, gmlp_ws, gmlp_bs, gmlp_w_out, kv_norm_g, kv_ada_w, kv_ada_b, w_kv, k_norm_g, w_f, b_f, attn_wq, q_norm_g, attn_wo, loss_target, m_ada_w, m_ada_b, m_norm_g, m_mlp_w1, m_mlp_w2, m_gmlp_w_in, m_gmlp_ln_g, m_gmlp_ln_b, m_gmlp_ws, m_gmlp_bs, m_gmlp_w_out, m_kv_norm_g, m_kv_ada_w, m_kv_ada_b, m_w_kv, m_k_norm_g, m_w_f, m_b_f, m_attn_wq, m_q_norm_g, m_attn_wo, v_ada_w, v_ada_b, v_norm_g, v_mlp_w1, v_mlp_w2, v_gmlp_w_in, v_gmlp_ln_g, v_gmlp_ln_b, v_gmlp_ws, v_gmlp_bs, v_gmlp_w_out, v_kv_norm_g, v_kv_ada_w, v_kv_ada_b, v_w_kv, v_k_norm_g, v_w_f, v_b_f, v_attn_wq, v_q_norm_g, v_attn_wo):
    given = dict(x=x, c=c, ada_w=ada_w, ada_b=ada_b, norm_g=norm_g, mlp_w1=mlp_w1, mlp_w2=mlp_w2, gmlp_w_in=gmlp_w_in, gmlp_ln_g=gmlp_ln_g, gmlp_ln_b=gmlp_ln_b, gmlp_ws=gmlp_ws, gmlp_bs=gmlp_bs, gmlp_w_out=gmlp_w_out, kv_norm_g=kv_norm_g, kv_ada_w=kv_ada_w, kv_ada_b=kv_ada_b, w_kv=w_kv, k_norm_g=k_norm_g, w_f=w_f, b_f=b_f, attn_wq=attn_wq, q_norm_g=q_norm_g, attn_wo=attn_wo, loss_target=loss_target, m_ada_w=m_ada_w, m_ada_b=m_ada_b, m_norm_g=m_norm_g, m_mlp_w1=m_mlp_w1, m_mlp_w2=m_mlp_w2, m_gmlp_w_in=m_gmlp_w_in, m_gmlp_ln_g=m_gmlp_ln_g, m_gmlp_ln_b=m_gmlp_ln_b, m_gmlp_ws=m_gmlp_ws, m_gmlp_bs=m_gmlp_bs, m_gmlp_w_out=m_gmlp_w_out, m_kv_norm_g=m_kv_norm_g, m_kv_ada_w=m_kv_ada_w, m_kv_ada_b=m_kv_ada_b, m_w_kv=m_w_kv, m_k_norm_g=m_k_norm_g, m_w_f=m_w_f, m_b_f=m_b_f, m_attn_wq=m_attn_wq, m_q_norm_g=m_q_norm_g, m_attn_wo=m_attn_wo, v_ada_w=v_ada_w, v_ada_b=v_ada_b, v_norm_g=v_norm_g, v_mlp_w1=v_mlp_w1, v_mlp_w2=v_mlp_w2, v_gmlp_w_in=v_gmlp_w_in, v_gmlp_ln_g=v_gmlp_ln_g, v_gmlp_ln_b=v_gmlp_ln_b, v_gmlp_ws=v_gmlp_ws, v_gmlp_bs=v_gmlp_bs, v_gmlp_w_out=v_gmlp_w_out, v_kv_norm_g=v_kv_norm_g, v_kv_ada_w=v_kv_ada_w, v_kv_ada_b=v_kv_ada_b, v_w_kv=v_w_kv, v_k_norm_g=v_k_norm_g, v_w_f=v_w_f, v_b_f=v_b_f, v_attn_wq=v_attn_wq, v_q_norm_g=v_q_norm_g, v_attn_wo=v_attn_wo)
    weights = {n: given[n] for n in TWIN_WEIGHTS}
    shared = {n: given[n] for n in SHARED_INPUTS}
    per_example = {n: given[n] for n in ['x', 'c']}
    grad_fn = _jax.value_and_grad(_loss, argnums=(0, 1))

    def one_microbatch(ex, loss_target):
        ex = dict(ex)
        diff = ex.pop(TWIN_DIFF_INPUT)
        return grad_fn(weights, diff, {**shared, **ex}, loss_target)

    if N_MICROBATCH == 1:
        loss, (grad_w, grad_x) = one_microbatch(per_example, given["loss_target"])
    else:
        def body(carry, xs):
            loss_sum, grad_sum = carry
            l_k, (gw_k, gx_k) = one_microbatch(xs[0], xs[1])
            with _jax.named_scope("update"):
                return (loss_sum + l_k, _jax.tree.map(_jnp.add, grad_sum, gw_k)), gx_k

        init = (_jnp.zeros((), _jnp.float32), _jax.tree.map(_jnp.zeros_like, weights))
        (loss, grad_w), grad_x = _jax.lax.scan(body, init, (per_example, given["loss_target"]))
    with _jax.named_scope("update"):
        delta_w, new_m, new_v = {}, {}, {}
        for n in TWIN_WEIGHTS:
            delta_w[n], new_m[n], new_v[n] = _adamw(weights[n], grad_w[n], given["m_" + n], given["v_" + n])
    return (loss, grad_x, *[grad_w[n] for n in TWIN_WEIGHTS], *[delta_w[n] for n in TWIN_WEIGHTS],
            *[new_m[n] for n in TWIN_WEIGHTS], *[new_v[n] for n in TWIN_WEIGHTS])
```

```python
import functools
import math

import jax
import jax.numpy as jnp
from jax import lax
from jax.experimental import pallas as pl
from jax.experimental.pallas import tpu as pltpu

F32 = jnp.float32
BF16 = jnp.bfloat16
EPS = 1e-6
HEAD_DIM = 128
GMLP_BLOCK = 128
CHUNK = 64
LANES = 128
N_MOD = 6
V7X_VMEM_BYTES = 64 * 2**20
VMEM_LIMIT = V7X_VMEM_BYTES - 8 * 2**20
PACK_COLS = 1024
NEG = -1e30
MESH = pl.DeviceIdType.MESH

ADAM_LR = 0.001
ADAM_B1 = 0.9
ADAM_B2 = 0.999
ADAM_EPS = 1e-08
ADAM_WD = 0.01
ADAM_STEP = 10

WEIGHTS = ['ada_w', 'ada_b', 'norm_g', 'mlp_w1', 'mlp_w2', 'gmlp_w_in', 'gmlp_ln_g', 'gmlp_ln_b', 'gmlp_ws',
           'gmlp_bs', 'gmlp_w_out', 'kv_norm_g', 'kv_ada_w', 'kv_ada_b', 'w_kv', 'k_norm_g', 'w_f', 'b_f',
           'attn_wq', 'q_norm_g', 'attn_wo']


def _params(sem=None):
    return pltpu.CompilerParams(dimension_semantics=sem, vmem_limit_bytes=VMEM_LIMIT)


def _sds(shape, dtype):
    return jax.ShapeDtypeStruct(tuple(shape), dtype)


def _ldims(shape):
    return (shape[0], shape[1]) if len(shape) == 2 else (shape[1], shape[0] * shape[2])


def _fit(t, ns):
    n0 = min(ns)
    if n0 <= t and all(n % n0 == 0 for n in ns):
        return n0
    d = (t // LANES) * LANES
    while d > LANES and any(n % d for n in ns):
        d -= LANES
    assert all(n % d == 0 for n in ns), (t, ns)
    return d


def _blk(shape, br, bc):
    if len(shape) == 2:
        return (br, bc), (lambda r, c: (r, c))
    per = shape[2] // bc
    assert shape[2] % bc == 0, (shape, bc)
    return (None, br, bc), (lambda r, c: (c // per, r, c % per))


def mm(a, b, *, name, ta=False, tb=False, outs, epilogue=None, extras=(), tm=1024, tn=1024, tk=2048,
       precision=None):
    ar, ac = _ldims(a.shape)
    br, bc = _ldims(b.shape)
    M, K = (ac, ar) if ta else (ar, ac)
    K2, N = (bc, br) if tb else (br, bc)
    assert K == K2, (name, a.shape, b.shape)
    cons = {"m": [M], "n": [N], "k": [K]}

    def note(shape, dim):
        if len(shape) == 3:
            cons[dim].append(shape[2])

    note(a.shape, "m" if ta else "k")
    note(b.shape, "k" if tb else "n")
    out_shapes = []
    for dt, nb in outs:
        if nb is None:
            out_shapes.append(_sds((M, N), dt))
        else:
            out_shapes.append(_sds((nb, M, N // nb), dt))
            cons["n"].append(N // nb)
    for e in extras:
        note(e.shape, "n")
    tm, tn, tk = _fit(tm, cons["m"]), _fit(tn, cons["n"]), _fit(tk, cons["k"])
    gm, gn, gk = M // tm, N // tn, K // tk

    a_bs, a_ix = _blk(a.shape, tk if ta else tm, tm if ta else tk)
    b_bs, b_ix = _blk(b.shape, tn if tb else tk, tk if tb else tn)
    in_specs = [
        pl.BlockSpec(a_bs, (lambda i, j, k: a_ix(k, i)) if ta else (lambda i, j, k: a_ix(i, k))),
        pl.BlockSpec(b_bs, (lambda i, j, k: b_ix(j, k)) if tb else (lambda i, j, k: b_ix(k, j))),
    ]
    for e in extras:
        if _ldims(e.shape)[0] == 1 and M != 1:
            if e.shape[1] == N:
                in_specs.append(pl.BlockSpec((1, tn), lambda i, j, k: (0, j)))
            else:
                in_specs.append(pl.BlockSpec(e.shape, lambda i, j, k: (0, 0)))
        else:
            e_bs, e_ix = _blk(e.shape, tm, tn)
            in_specs.append(pl.BlockSpec(e_bs, functools.partial(lambda i, j, k, ix: ix(i, j), ix=e_ix)))
    out_specs = []
    for s in out_shapes:
        o_bs, o_ix = _blk(s.shape, tm, tn)
        out_specs.append(pl.BlockSpec(o_bs, functools.partial(lambda i, j, k, ix: ix(i, j), ix=o_ix)))
    n_e, n_o = len(extras), len(outs)
    dims = (((0 if ta else 1,), (1 if tb else 0,)), ((), ()))

    def body(*refs):
        a_ref, b_ref = refs[:2]
        e_refs = refs[2:2 + n_e]
        o_refs = refs[2 + n_e:2 + n_e + n_o]
        x, w = a_ref[...], b_ref[...]
        if precision is None:
            x, w = x.astype(BF16), w.astype(BF16)
        d = lax.dot_general(x, w, dims, preferred_element_type=F32, precision=precision)

        def finish(acc):
            res = epilogue(acc, *[e[...] for e in e_refs]) if epilogue is not None else (acc,)
            for o, r in zip(o_refs, res):
                o[...] = r.astype(o.dtype)

        if gk == 1:
            finish(d)
        else:
            acc_ref = refs[-1]
            k = pl.program_id(2)

            @pl.when(k == 0)
            def _():
                acc_ref[...] = d

            @pl.when(jnp.logical_and(k > 0, k < gk - 1))
            def _():
                acc_ref[...] += d

            @pl.when(k == gk - 1)
            def _():
                finish(acc_ref[...] + d)

    return pl.pallas_call(
        body, name=name, grid=(gm, gn, gk), in_specs=in_specs, out_specs=out_specs, out_shape=out_shapes,
        scratch_shapes=[pltpu.VMEM((tm, tn), F32)] if gk > 1 else [],
        compiler_params=_params(("parallel", "parallel", "arbitrary")),
    )(a, b, *extras)


def rowmap(fn, rows, vecs, outs, reds=(), *, tr, name):
    rows = [r if isinstance(r, tuple) else (r, r.shape[1], 0) for r in rows]
    S = rows[0][0].shape[0]
    tr = min(tr, S)
    assert S % tr == 0, (name, S, tr)
    n_i, n_o = len(rows) + len(vecs), len(outs)
    in_specs = [pl.BlockSpec((tr, w), functools.partial(lambda i, c: (i, c), c=c)) for _, w, c in rows]
    in_specs += [pl.BlockSpec(v.shape, functools.partial(lambda i, n: (0,) * n, n=v.ndim)) for v in vecs]
    out_shape = [_sds((S, f), dt) for f, dt in outs] + [_sds(s, F32) for s in reds]
    out_specs = [pl.BlockSpec((tr, f), lambda i: (i, 0)) for f, _ in outs]
    out_specs += [pl.BlockSpec(s, functools.partial(lambda i, n: (0,) * n, n=len(s))) for s in reds]

    def body(*refs):
        res = fn(*[r[...] for r in refs[:n_i]])
        res = res if isinstance(res, tuple) else (res,)
        for o, r in zip(refs[n_i:n_i + n_o], res[:n_o]):
            o[...] = r.astype(o.dtype)
        if reds:
            d_refs = refs[n_i + n_o:]

            @pl.when(pl.program_id(0) == 0)
            def _():
                for d in d_refs:
                    d[...] = jnp.zeros(d.shape, F32)

            for d, r in zip(d_refs, res[n_o:]):
                d[...] += r

    return pl.pallas_call(
        body, name=name, grid=(S // tr,), in_specs=in_specs, out_specs=out_specs, out_shape=out_shape,
        compiler_params=_params(("arbitrary",) if reds else ("parallel",)),
    )(*[r[0] for r in rows], *vecs)


def _gelu(t):
    return 0.5 * t * (1.0 + lax.erf(t * (1.0 / math.sqrt(2.0))))


def _gelu_grad(t):
    cdf = 0.5 * (1.0 + lax.erf(t * (1.0 / math.sqrt(2.0))))
    return cdf + t * jnp.exp(-0.5 * t * t) * (1.0 / math.sqrt(2.0 * math.pi))


def _colsum(v):
    return jnp.sum(v, axis=0, keepdims=True)


def modnorm_fwd(x, g, scale, shift, name):
    def fn(xb, gb, sc, sh):
        rstd = lax.rsqrt(jnp.mean(xb * xb, axis=-1, keepdims=True) + EPS)
        return ((xb * rstd) * gb) * (1.0 + sc) + sh
    return rowmap(fn, [x], [g, scale, shift], [(x.shape[1], BF16)], tr=512, name=name)[0]


def modnorm_bwd(x, dh, dres, g, scale, name):
    D = x.shape[1]

    def fn(xb, dhb, drb, gb, sc):
        dhb = dhb.astype(F32)
        rstd = lax.rsqrt(jnp.mean(xb * xb, axis=-1, keepdims=True) + EPS)
        xhat = xb * rstd
        a = gb * (1.0 + sc)
        dxhat = dhb * a
        dx = rstd * (dxhat - xhat * jnp.mean(dxhat * xhat, axis=-1, keepdims=True))
        da = _colsum(dhb * xhat)
        return drb + dx, _colsum(dhb), da * (1.0 + sc), da * gb
    return rowmap(fn, [x, dh, dres], [g, scale], [(D, F32)], reds=[(1, D)] * 3, tr=256, name=name)


def gate_bwd(dx, y, gate, name):
    D = dx.shape[1]

    def fn(dxb, yb, gb):
        return gb * dxb, _colsum(dxb * yb.astype(F32))
    return rowmap(fn, [dx, y], [gate], [(D, BF16)], reds=[(1, D)], tr=512, name=name)


def _head_norm(x, g):
    parts = []
    for h in range(x.shape[1] // HEAD_DIM):
        xh = x[:, h * HEAD_DIM:(h + 1) * HEAD_DIM]
        rstd = lax.rsqrt(jnp.mean(xh * xh, axis=-1, keepdims=True) + EPS)
        parts.append(xh * rstd * g)
    return jnp.concatenate(parts, axis=1)


def head_norm_bwd(xp, dy, g, name, extra=None):
    D = xp.shape[1]

    def fn(*blocks):
        xb, dyb = blocks[0].astype(F32), blocks[1].astype(F32)
        gb = blocks[-1]
        parts, dg = [], jnp.zeros((1, HEAD_DIM), F32)
        for h in range(D // HEAD_DIM):
            xh = xb[:, h * HEAD_DIM:(h + 1) * HEAD_DIM]
            dyh = dyb[:, h * HEAD_DIM:(h + 1) * HEAD_DIM]
            rstd = lax.rsqrt(jnp.mean(xh * xh, axis=-1, keepdims=True) + EPS)
            xhat = xh * rstd
            dg = dg + _colsum(dyh * xhat)
            dxhat = dyh * gb
            parts.append(rstd * (dxhat - xhat * jnp.mean(dxhat * xhat, axis=-1, keepdims=True)))
        if extra is not None:
            parts.append(blocks[2].astype(F32))
        return jnp.concatenate(parts, axis=1), dg
    rows = [xp, dy] + ([extra] if extra is not None else [])
    width = D + (extra.shape[1] if extra is not None else 0)
    return rowmap(fn, rows, [g], [(width, BF16)], reds=[(1, HEAD_DIM)], tr=512, name=name)


def gmlp_ln_fwd(z, ln_g, ln_b, name):
    half = z.shape[1] // 2

    def fn(vb, gb, bb):
        vb = vb.astype(F32)
        mu = jnp.mean(vb, axis=-1, keepdims=True)
        var = jnp.mean(jnp.square(vb - mu), axis=-1, keepdims=True)
        return ((vb - mu) * lax.rsqrt(var + EPS)) * gb + bb
    return rowmap(fn, [(z, half, 1)], [ln_g, ln_b], [(half, BF16)], tr=256, name=name)[0]


def _mix_mask():
    r = lax.broadcasted_iota(jnp.int32, (GMLP_BLOCK, GMLP_BLOCK), 0) // CHUNK
    c = lax.broadcasted_iota(jnp.int32, (GMLP_BLOCK, GMLP_BLOCK), 1) // CHUNK
    return c <= r


def gmlp_mix_fwd(z, vn, ws, bsb, name):
    S, half = vn.shape
    G = ws.shape[0]
    gd = half // G
    tb = min(512, S // 2)

    def body(u_ref, v_ref, w_ref, b_ref, p_ref):
        w = jnp.where(_mix_mask(), w_ref[...], 0.0).astype(BF16)
        bcol = b_ref[:, 0:1]
        for r in range(tb // GMLP_BLOCK):
            rs = slice(r * GMLP_BLOCK, (r + 1) * GMLP_BLOCK)
            sv = jnp.dot(w, v_ref[rs, :], preferred_element_type=F32) + bcol
            p_ref[rs, :] = (u_ref[rs, :].astype(F32) * sv).astype(p_ref.dtype)

    return pl.pallas_call(
        body, name=name, grid=(S // tb, G),
        in_specs=[pl.BlockSpec((tb, gd), lambda n, g: (n, g)), pl.BlockSpec((tb, gd), lambda n, g: (n, g)),
                  pl.BlockSpec((None, GMLP_BLOCK, GMLP_BLOCK), lambda n, g: (g, 0, 0)),
                  pl.BlockSpec((None, GMLP_BLOCK, LANES), lambda n, g: (g, 0, 0))],
        out_specs=pl.BlockSpec((tb, gd), lambda n, g: (n, g)),
        out_shape=_sds((S, half), BF16),
        compiler_params=_params(("parallel", "parallel")),
    )(z, vn, ws, bsb)


def gmlp_mix_bwd(dp, z, vn, ws, bsb, name):
    S, half = vn.shape
    G = ws.shape[0]
    gd = half // G
    tb = min(512, S // 2)

    def body(dp_ref, u_ref, v_ref, w_ref, b_ref, du_ref, dv_ref, dw_ref, db_ref):
        n = pl.program_id(1)
        mask = _mix_mask()
        w = jnp.where(mask, w_ref[...], 0.0).astype(BF16)
        bcol = b_ref[:, 0:1]
        dw = jnp.zeros((GMLP_BLOCK, GMLP_BLOCK), F32)
        db = jnp.zeros((GMLP_BLOCK, 1), F32)
        for r in range(tb // GMLP_BLOCK):
            rs = slice(r * GMLP_BLOCK, (r + 1) * GMLP_BLOCK)
            vb = v_ref[rs, :]
            dpb = dp_ref[rs, :].astype(F32)
            sv = jnp.dot(w, vb, preferred_element_type=F32) + bcol
            du_ref[rs, :] = (dpb * sv).astype(du_ref.dtype)
            dsv = dpb * u_ref[rs, :].astype(F32)
            dsv16 = dsv.astype(BF16)
            dv_ref[rs, :] = lax.dot_general(w, dsv16, (((0,), (0,)), ((), ())),
                                            preferred_element_type=F32).astype(dv_ref.dtype)
            dw = dw + lax.dot_general(dsv16, vb, (((1,), (1,)), ((), ())), preferred_element_type=F32)
            db = db + jnp.sum(dsv, axis=1, keepdims=True)
        dw = jnp.where(mask, dw, 0.0)
        db = jnp.broadcast_to(db, (GMLP_BLOCK, LANES))

        @pl.when(n == 0)
        def _():
            dw_ref[...] = dw
            db_ref[...] = db

        @pl.when(n > 0)
        def _():
            dw_ref[...] += dw
            db_ref[...] += db

    blk = pl.BlockSpec((tb, gd), lambda g, n: (n, g))
    return pl.pallas_call(
        body, name=name, grid=(G, S // tb),
        in_specs=[blk, blk, blk, pl.BlockSpec((None, GMLP_BLOCK, GMLP_BLOCK), lambda g, n: (g, 0, 0)),
                  pl.BlockSpec((None, GMLP_BLOCK, LANES), lambda g, n: (g, 0, 0))],
        out_specs=[blk, blk, pl.BlockSpec((None, GMLP_BLOCK, GMLP_BLOCK), lambda g, n: (g, 0, 0)),
                   pl.BlockSpec((None, GMLP_BLOCK, LANES), lambda g, n: (g, 0, 0))],
        out_shape=[_sds((S, half), BF16), _sds((S, half), BF16), _sds((G, GMLP_BLOCK, GMLP_BLOCK), F32),
                   _sds((G, GMLP_BLOCK, LANES), F32)],
        compiler_params=_params(("parallel", "arbitrary")),
    )(dp, z, vn, ws, bsb)


def gmlp_act_bwd(t, du, dvn, ln_g, name):
    half = du.shape[1]

    def fn(tb_, dub, dvb, gb):
        tb_ = tb_.astype(F32)
        tu, tv = tb_[:, :half], tb_[:, half:]
        dtu = dub.astype(F32) * _gelu_grad(tu)
        v = _gelu(tv)
        mu = jnp.mean(v, axis=-1, keepdims=True)
        vc = v - mu
        rstd = lax.rsqrt(jnp.mean(vc * vc, axis=-1, keepdims=True) + EPS)
        vhat = vc * rstd
        dvb = dvb.astype(F32)
        dvhat = dvb * gb
        dv = rstd * (dvhat - jnp.mean(dvhat, axis=-1, keepdims=True)
                     - vhat * jnp.mean(dvhat * vhat, axis=-1, keepdims=True))
        dtv = dv * _gelu_grad(tv)
        return jnp.concatenate([dtu, dtv], axis=1), _colsum(dvb * vhat), _colsum(dvb)
    return rowmap(fn, [t, du, dvn], [ln_g], [(2 * half, BF16)], reds=[(1, half)] * 2, tr=128, name=name)


def fcum_fwd(logf, tb, name):
    S = logf.shape[0]
    nb = S // tb

    def body(x_ref, fc_ref, fk_ref):
        tri = (lax.broadcasted_iota(jnp.int32, (LANES, LANES), 0)
               >= lax.broadcasted_iota(jnp.int32, (LANES, LANES), 1)).astype(F32)

        def blk(b, carry):
            off = pl.multiple_of(b * LANES, LANES)
            cs = jnp.dot(tri, x_ref[pl.ds(off, LANES), :], preferred_element_type=F32,
                         precision=lax.Precision.HIGHEST) + carry
            fc_ref[pl.ds(off, LANES), :] = cs
            return cs[LANES - 1:LANES, :]

        lax.fori_loop(0, S // LANES, blk, jnp.zeros((1, LANES), F32))
        for b in range(nb):
            fk_ref[b] = fc_ref[b * tb:(b + 1) * tb, :].T

    return pl.pallas_call(
        body, name=name, out_shape=[_sds((S, LANES), F32), _sds((nb, LANES, tb), F32)],
        compiler_params=_params(),
    )(logf)


def fcum_bwd(dfk_a, dfk_b, zf, n_heads, name):
    S = zf.shape[0]

    def body(da_ref, db_ref, zf_ref, dz_ref, dsum_ref, d_ref):
        d_ref[...] = (da_ref[...] + db_ref[...]).T
        triu = (lax.broadcasted_iota(jnp.int32, (LANES, LANES), 0)
                <= lax.broadcasted_iota(jnp.int32, (LANES, LANES), 1)).astype(F32)
        nblk = S // LANES
        live = lax.broadcasted_iota(jnp.int32, (LANES, LANES), 1) < n_heads

        def blk(r, carry):
            carry_row, tot = carry
            off = pl.multiple_of((nblk - 1 - r) * LANES, LANES)
            d_blk = d_ref[pl.ds(off, LANES), :]
            cs = jnp.dot(triu, d_blk, preferred_element_type=F32, precision=lax.Precision.HIGHEST) + carry_row
            dz = jnp.where(live, cs * jax.nn.sigmoid(-zf_ref[pl.ds(off, LANES), :]), 0.0)
            dz_ref[pl.ds(off, LANES), :] = dz.astype(dz_ref.dtype)
            return carry_row + _colsum(d_blk), tot + _colsum(dz)

        _, tot = lax.fori_loop(0, nblk, blk, (jnp.zeros((1, LANES), F32), jnp.zeros((1, LANES), F32)))
        dsum_ref[...] = tot

    return pl.pallas_call(
        body, name=name, out_shape=[_sds((S, LANES), BF16), _sds((1, LANES), F32)],
        scratch_shapes=[pltpu.VMEM((S, LANES), F32)], compiler_params=_params(),
    )(dfk_a, dfk_b, zf)


def _lane_pick(block, h):
    lane = lax.broadcasted_iota(jnp.int32, block.shape, 1)
    return jnp.sum(jnp.where(lane == h, block, 0.0), axis=1, keepdims=True)


def _causal(tq):
    return (lax.broadcasted_iota(jnp.int32, (tq, tq), 1) <= lax.broadcasted_iota(jnp.int32, (tq, tq), 0))


_NT = (((1,), (1,)), ((), ()))
_TN = (((0,), (0,)), ((), ()))


def fox_fwd(q, k, v, fc, fk, tq, name):
    S, D = q.shape
    H, nq = D // HEAD_DIM, S // tq
    scale = 1.0 / math.sqrt(HEAD_DIM)

    def body(q_ref, k_ref, v_ref, fc_ref, fk_ref, o_ref, lse_ref):
        h, i = pl.program_id(0), pl.program_id(1)
        qb = q_ref[...]
        fqc = _lane_pick(fc_ref[...], h)
        hs = h % 8

        def step(j, carry, masked):
            m, l, acc = carry
            off = pl.multiple_of(j * tq, tq)
            kb, vb = k_ref[pl.ds(off, tq), :], v_ref[pl.ds(off, tq), :]
            s = lax.dot_general(qb, kb, _NT, preferred_element_type=F32) * scale
            s = s + (fqc - fk_ref[j, pl.ds(hs, 1), :])
            if masked:
                s = jnp.where(_causal(tq), s, NEG)
            m_new = jnp.maximum(m, jnp.max(s, axis=1, keepdims=True))
            alpha = jnp.exp(m - m_new)
            p = jnp.exp(s - m_new)
            l = alpha * l + jnp.sum(p, axis=1, keepdims=True)
            acc = alpha * acc + jnp.dot(p.astype(BF16), vb, preferred_element_type=F32)
            return m_new, l, acc

        init = (jnp.full((tq, 1), NEG, F32), jnp.zeros((tq, 1), F32), jnp.zeros((tq, HEAD_DIM), F32))
        carry = lax.fori_loop(0, i, lambda j, c: step(j, c, False), init)
        m, l, acc = step(i, carry, True)
        o_ref[...] = (acc / l).astype(o_ref.dtype)
        lse_ref[...] = jnp.broadcast_to(m + jnp.log(l), (tq, LANES))

    return pl.pallas_call(
        body, name=name, grid=(H, nq),
        in_specs=[pl.BlockSpec((tq, HEAD_DIM), lambda h, i: (i, h)),
                  pl.BlockSpec((S, HEAD_DIM), lambda h, i: (0, h)),
                  pl.BlockSpec((S, HEAD_DIM), lambda h, i: (0, h)),
                  pl.BlockSpec((tq, LANES), lambda h, i: (i, 0)),
                  pl.BlockSpec((nq, 8, tq), lambda h, i: (0, h // 8, 0))],
        out_specs=[pl.BlockSpec((tq, HEAD_DIM), lambda h, i: (i, h)),
                   pl.BlockSpec((None, tq, LANES), lambda h, i: (h, i, 0))],
        out_shape=[_sds((S, D), BF16), _sds((H, S, LANES), F32)],
        compiler_params=_params(("parallel", "arbitrary")),
    )(q, k, v, fc, fk)


def fox_dq(q, k, v, do, lse, fc, fk, tq, name):
    S, D = q.shape
    H, nq = D // HEAD_DIM, S // tq
    scale = 1.0 / math.sqrt(HEAD_DIM)

    def body(q_ref, k_ref, v_ref, do_ref, lse_ref, fc_ref, fk_ref, dq_ref, row_ref):
        h, i = pl.program_id(0), pl.program_id(1)
        qb, dob = q_ref[...], do_ref[...]
        lsec = lse_ref[:, 0:1]
        fqc = _lane_pick(fc_ref[...], h)
        hs = h % 8

        def p_dp(j, masked):
            off = pl.multiple_of(j * tq, tq)
            kb, vb = k_ref[pl.ds(off, tq), :], v_ref[pl.ds(off, tq), :]
            s = lax.dot_general(qb, kb, _NT, preferred_element_type=F32) * scale
            s = s + (fqc - fk_ref[j, pl.ds(hs, 1), :])
            if masked:
                s = jnp.where(_causal(tq), s, NEG)
            return jnp.exp(s - lsec), lax.dot_general(dob, vb, _NT, preferred_element_type=F32), kb

        def sums(j, carry, masked):
            p, dp, _ = p_dp(j, masked)
            return carry[0] + jnp.sum(p * dp, axis=1, keepdims=True), carry[1] + jnp.sum(p, axis=1, keepdims=True)

        zero = jnp.zeros((tq, 1), F32)
        carry = lax.fori_loop(0, i, lambda j, c: sums(j, c, False), (zero, zero))
        num, den = sums(i, carry, True)
        rowterm = num / den

        def step(j, acc, masked):
            p, dp, kb = p_dp(j, masked)
            ds = p * (dp - rowterm)
            return acc + jnp.dot(ds.astype(BF16), kb, preferred_element_type=F32)

        acc = lax.fori_loop(0, i, lambda j, c: step(j, c, False), jnp.zeros((tq, HEAD_DIM), F32))
        acc = step(i, acc, True)
        dq_ref[...] = (acc * scale).astype(dq_ref.dtype)
        row_ref[...] = jnp.broadcast_to(rowterm, (tq, LANES))

    tile = pl.BlockSpec((tq, HEAD_DIM), lambda h, i: (i, h))
    full = pl.BlockSpec((S, HEAD_DIM), lambda h, i: (0, h))
    stat = pl.BlockSpec((None, tq, LANES), lambda h, i: (h, i, 0))
    return pl.pallas_call(
        body, name=name, grid=(H, nq),
        in_specs=[tile, full, full, tile, stat, pl.BlockSpec((tq, LANES), lambda h, i: (i, 0)),
                  pl.BlockSpec((nq, 8, tq), lambda h, i: (0, h // 8, 0))],
        out_specs=[tile, stat], out_shape=[_sds((S, D), BF16), _sds((H, S, LANES), F32)],
        compiler_params=_params(("parallel", "arbitrary")),
    )(q, k, v, do, lse, fc, fk)


def fox_dkv(q, k, v, do, rowterm, lse, fc, fk, tq, name):
    S, D = q.shape
    H, nq = D // HEAD_DIM, S // tq
    scale = 1.0 / math.sqrt(HEAD_DIM)

    def body(q_ref, k_ref, v_ref, do_ref, row_ref, lse_ref, fc_ref, fk_ref, dk_ref, dv_ref, dfk_ref):
        h, j = pl.program_id(0), pl.program_id(1)
        kb, vb = k_ref[...], v_ref[...]
        fkr = fk_ref[pl.ds(h % 8, 1), :]

        def step(i, carry, masked):
            dk, dv, dfk = carry
            off = pl.multiple_of(i * tq, tq)
            qb, dob = q_ref[pl.ds(off, tq), :], do_ref[pl.ds(off, tq), :]
            delta = row_ref[pl.ds(off, tq), 0:1]
            lsec = lse_ref[pl.ds(off, tq), 0:1]
            fqc = _lane_pick(fc_ref[pl.ds(off, tq), :], h)
            s = lax.dot_general(qb, kb, _NT, preferred_element_type=F32) * scale + (fqc - fkr)
            if masked:
                s = jnp.where(_causal(tq), s, NEG)
            p = jnp.exp(s - lsec)
            dv = dv + lax.dot_general(p.astype(BF16), dob, _TN, preferred_element_type=F32)
            dp = lax.dot_general(dob, vb, _NT, preferred_element_type=F32)
            ds = p * (dp - delta)
            dk = dk + lax.dot_general(ds.astype(BF16), qb, _TN, preferred_element_type=F32)
            return dk, dv, dfk - _colsum(ds)

        init = (jnp.zeros((tq, HEAD_DIM), F32), jnp.zeros((tq, HEAD_DIM), F32), jnp.zeros((1, tq), F32))
        carry = step(j, init, True)
        dk, dv, dfk = lax.fori_loop(j + 1, nq, lambda i, c: step(i, c, False), carry)
        dk_ref[...] = (dk * scale).astype(dk_ref.dtype)
        dv_ref[...] = dv.astype(dv_ref.dtype)
        dfk_ref[...] = jnp.broadcast_to(dfk, (8, tq))

    tile = pl.BlockSpec((tq, HEAD_DIM), lambda h, j: (j, h))
    full = pl.BlockSpec((S, HEAD_DIM), lambda h, j: (0, h))
    stat = pl.BlockSpec((None, S, LANES), lambda h, j: (h, 0, 0))
    return pl.pallas_call(
        body, name=name, grid=(H, nq),
        in_specs=[full, tile, tile, full, stat, stat, pl.BlockSpec((S, LANES), lambda h, j: (0, 0)),
                  pl.BlockSpec((None, 8, tq), lambda h, j: (j, h // 8, 0))],
        out_specs=[tile, tile, pl.BlockSpec((None, None, 8, tq), lambda h, j: (h, j, 0, 0))],
        out_shape=[_sds((S, D), BF16), _sds((S, D), BF16), _sds((H, nq, 8, tq), F32)],
        compiler_params=_params(("parallel", "arbitrary")),
    )(q, k, v, do, rowterm, lse, fc, fk)


def _adamw_math(w, g, m, v):
    m = ADAM_B1 * m + (1.0 - ADAM_B1) * g
    v = ADAM_B2 * v + (1.0 - ADAM_B2) * jnp.square(g)
    m_hat = m / (1.0 - ADAM_B1 ** ADAM_STEP)
    v_hat = v / (1.0 - ADAM_B2 ** ADAM_STEP)
    delta = -ADAM_LR * (m_hat / (jnp.sqrt(v_hat) + ADAM_EPS) + ADAM_WD * w)
    return delta, m, v


def adamw(w, g, m, v, name):
    shape = w.shape
    cols = shape[-1]
    two_d = lambda a: a.reshape(-1, cols)
    rows = max(1, w.size // cols)
    tr = rows if rows * cols * 4 <= 2**21 else max(8, (2**21 // (cols * 4)) // 8 * 8)
    while rows % tr:
        tr -= 8
    res = rowmap(_adamw_math, [two_d(w), two_d(g), two_d(m), two_d(v)], [], [(cols, F32)] * 3, tr=tr, name=name)
    return [r.reshape(shape) for r in res]


def adamw_outer(w, sct, dm, m, v, name):
    L, R, C = w.shape
    B = sct.shape[1]
    tr = min(R, 256)

    def body(w_ref, s_ref, d_ref, m_ref, v_ref, g_out, dl_out, m_out, v_out):
        g = jnp.dot(s_ref[...], d_ref[...], preferred_element_type=F32, precision=lax.Precision.HIGHEST)
        delta, mn, vn = _adamw_math(w_ref[...], g, m_ref[...], v_ref[...])
        g_out[...] = g
        dl_out[...] = delta
        m_out[...] = mn
        v_out[...] = vn

    big = pl.BlockSpec((None, tr, C), lambda l, i: (l, i, 0))
    return pl.pallas_call(
        body, name=name, grid=(L, R // tr),
        in_specs=[big, pl.BlockSpec((tr, B), lambda l, i: (i, 0)), pl.BlockSpec((None, B, C), lambda l, i: (l, 0, 0)),
                  big, big],
        out_specs=[big] * 4, out_shape=[_sds((L, R, C), F32)] * 4,
        compiler_params=_params(("parallel", "parallel")),
    )(w, sct, dm, m, v)


def _place():
    x, y, c = lax.axis_index("x"), lax.axis_index("y"), lax.axis_index("c")
    return x, y, c


def _other_chips(x, y):
    return [(1 - x, y), (x, 1 - y), (1 - x, 1 - y)]


def allgather_small(block, name):
    m_per, n = block.shape

    def body(x_ref, out_ref, send_sems, recv_sems, local_sem):
        x, y, c = _place()
        me, sibling = (x, y, c), (x, y, 1 - c)
        chips = _other_chips(x, y)

        def rows(px, py, pc):
            return out_ref.at[pl.ds((4 * px + 2 * py + pc) * m_per, m_per), :]

        def copy(k, block_of, to, src=None):
            return pltpu.make_async_remote_copy(
                src_ref=rows(*block_of) if src is None else src, dst_ref=rows(*block_of),
                send_sem=send_sems.at[k], recv_sem=recv_sems.at[k], device_id=to, device_id_type=MESH)

        mine = pltpu.make_async_copy(x_ref, rows(*me), local_sem)
        mine.start()
        first = [copy(0, me, sibling, src=x_ref)]
        first += [copy(1 + j, me, (*chip, c), src=x_ref) for j, chip in enumerate(chips)]
        for cp in first:
            cp.start()
        passed = [copy(4 + j, (*chip, c), sibling) for j, chip in enumerate(chips)]
        for j, chip in enumerate(chips):
            copy(1 + j, (*chip, c), me).wait_recv()
            passed[j].start()
        copy(0, sibling, me).wait_recv()
        for j, chip in enumerate(chips):
            copy(4 + j, (*chip, 1 - c), me).wait_recv()
        for cp in first + passed:
            cp.wait_send()
        mine.wait()

    return pl.pallas_call(
        body, name=name, out_shape=_sds((8 * m_per, n), block.dtype),
        in_specs=[pl.BlockSpec(memory_space=pltpu.VMEM)], out_specs=pl.BlockSpec(memory_space=pltpu.VMEM),
        scratch_shapes=[pltpu.SemaphoreType.DMA((7,)), pltpu.SemaphoreType.DMA((7,)), pltpu.SemaphoreType.DMA],
        compiler_params=_params(),
    )(block)


def _half(ref, which):
    n = ref.shape[-2] // 2
    idx = (slice(None),) * (len(ref.shape) - 2) + (pl.ds(which * n, n), slice(None))
    return ref.at[idx]


_ANY = pl.BlockSpec(memory_space=pl.ANY)


def allgather_weights(shards, name):
    T = len(shards)

    def body(*refs):
        ins, outs = refs[:T], refs[T:2 * T]
        send_sems, recv_sems, fsend_sems, frecv_sems, local_sems = refs[2 * T:]
        x, y, c = _place()
        sibling = (x, y, 1 - c)
        my_chip = 2 * x + y
        chips = _other_chips(x, y)
        local = [pltpu.make_async_copy(ins[t], outs[t].at[my_chip], local_sems.at[t]) for t in range(T)]
        for cp in local:
            cp.start()

        def ici(t, j, chip_of_data, to):
            return pltpu.make_async_remote_copy(
                src_ref=_half(ins[t], c), dst_ref=_half(outs[t].at[chip_of_data], c),
                send_sem=send_sems.at[t, j], recv_sem=recv_sems.at[t, j], device_id=to, device_id_type=MESH)

        def d2d(t, j, chip_of_data, which):
            blk = _half(outs[t].at[chip_of_data], which)
            return pltpu.make_async_remote_copy(
                src_ref=blk, dst_ref=blk, send_sem=fsend_sems.at[t, j], recv_sem=frecv_sems.at[t, j],
                device_id=sibling, device_id_type=MESH)

        sends = [ici(t, j, my_chip, (*chip, c)) for j, chip in enumerate(chips) for t in range(T)]
        for cp in sends:
            cp.start()
        passed = []
        for j, (cx, cy) in enumerate(chips):
            for t in range(T):
                ici(t, j, 2 * cx + cy, (x, y, c)).wait_recv()
                fwd = d2d(t, j, 2 * cx + cy, c)
                fwd.start()
                passed.append(fwd)
        for j, (cx, cy) in enumerate(chips):
            for t in range(T):
                d2d(t, j, 2 * cx + cy, 1 - c).wait_recv()
        for cp in sends + passed:
            cp.wait_send()
        for cp in local:
            cp.wait()

    sem = lambda: pltpu.SemaphoreType.DMA((T, 3))
    return pl.pallas_call(
        body, name=name, out_shape=[_sds((4,) + s.shape, s.dtype) for s in shards],
        in_specs=[_ANY] * T, out_specs=[_ANY] * T,
        scratch_shapes=[sem(), sem(), sem(), sem(), pltpu.SemaphoreType.DMA((T,))],
        compiler_params=_params(),
    )(*shards)


def exchange_halves(full, name):
    T = len(full)

    def body(*refs):
        ins, outs = refs[:T], refs[T:2 * T]
        send_sems, recv_sems = refs[2 * T:]
        x, y, c = _place()
        cps = [pltpu.make_async_remote_copy(
            src_ref=_half(ins[t], 1 - c), dst_ref=outs[t], send_sem=send_sems.at[t], recv_sem=recv_sems.at[t],
            device_id=(x, y, 1 - c), device_id_type=MESH) for t in range(T)]
        for cp in cps:
            cp.start()
        for cp in cps:
            cp.wait()

    return pl.pallas_call(
        body, name=name,
        out_shape=[_sds((4, f.shape[1] // 2, f.shape[2]), f.dtype) for f in full],
        in_specs=[_ANY] * T, out_specs=[_ANY] * T,
        scratch_shapes=[pltpu.SemaphoreType.DMA((T,)), pltpu.SemaphoreType.DMA((T,))],
        compiler_params=_params(),
    )(*full)


def scatter_chip_sums(sums, name):
    T = len(sums)

    def body(*refs):
        ins, outs = refs[:T], refs[T:2 * T]
        send_sems, recv_sems = refs[2 * T:]
        x, y, c = _place()
        cps = []
        for j, (cx, cy) in enumerate(_other_chips(x, y)):
            for t in range(T):
                cps.append(pltpu.make_async_remote_copy(
                    src_ref=ins[t].at[2 * cx + cy], dst_ref=outs[t].at[j], send_sem=send_sems.at[t, j],
                    recv_sem=recv_sems.at[t, j], device_id=(cx, cy, c), device_id_type=MESH))
        for cp in cps:
            cp.start()
        for cp in cps:
            cp.wait()

    return pl.pallas_call(
        body, name=name, out_shape=[_sds((3,) + s.shape[1:], s.dtype) for s in sums],
        in_specs=[_ANY] * T, out_specs=[_ANY] * T,
        scratch_shapes=[pltpu.SemaphoreType.DMA((T, 3)), pltpu.SemaphoreType.DMA((T, 3))],
        compiler_params=_params(),
    )(*sums)


def join_halves(halves, groups, name):
    T = len(halves)
    G = len(groups)

    def body(*refs):
        ins, outs = refs[:T], refs[T:T + G]
        send_sems, recv_sems, local_sems = refs[T + G:]
        x, y, c = _place()
        remote, local = [], []
        for gi, (_, members) in enumerate(groups):
            for l, t in enumerate(members):
                dst = _half(outs[gi].at[l], c)
                local.append(pltpu.make_async_copy(ins[t], dst, local_sems.at[t]))
                remote.append(pltpu.make_async_remote_copy(
                    src_ref=ins[t], dst_ref=dst, send_sem=send_sems.at[t], recv_sem=recv_sems.at[t],
                    device_id=(x, y, 1 - c), device_id_type=MESH))
        for cp in local + remote:
            cp.start()
        for cp in remote:
            cp.wait()
        for cp in local:
            cp.wait()

    return pl.pallas_call(
        body, name=name, out_shape=[_sds(shape, F32) for shape, _ in groups],
        in_specs=[_ANY] * T, out_specs=[_ANY] * G,
        scratch_shapes=[pltpu.SemaphoreType.DMA((T,)), pltpu.SemaphoreType.DMA((T,)), pltpu.SemaphoreType.DMA((T,))],
        compiler_params=_params(),
    )(*halves)


def add_halves(full, got, which, name):
    nb, R, C = full.shape
    rh = R // 2
    tr = _fit(512, [rh])
    per = rh // tr

    def body(w_ref, a_ref, b_ref, o_ref):
        o_ref[...] = (a_ref[...].astype(F32) + b_ref[...].astype(F32)).astype(o_ref.dtype)

    return pl.pallas_call(
        body, name=name,
        grid_spec=pltpu.PrefetchScalarGridSpec(
            num_scalar_prefetch=1, grid=(nb, per),
            in_specs=[pl.BlockSpec((None, tr, C), lambda b, i, w: (b, w[0] * per + i, 0)),
                      pl.BlockSpec((None, tr, C), lambda b, i, w: (b, i, 0))],
            out_specs=pl.BlockSpec((None, tr, C), lambda b, i, w: (b, i, 0))),
        out_shape=_sds((nb, rh, C), BF16),
        compiler_params=_params(("parallel", "parallel")),
    )(which, full, got)


def finish_sum(sums, got, chip, name):
    nb, rh, C = sums.shape
    tr = _fit(512, [rh])

    def body(w_ref, s_ref, g_ref, o_ref):
        acc = s_ref[...].astype(F32)
        for j in range(3):
            acc = acc + g_ref[j].astype(F32)
        o_ref[...] = acc

    return pl.pallas_call(
        body, name=name,
        grid_spec=pltpu.PrefetchScalarGridSpec(
            num_scalar_prefetch=1, grid=(rh // tr,),
            in_specs=[pl.BlockSpec((None, tr, C), lambda i, w: (w[0], i, 0)),
                      pl.BlockSpec((3, tr, C), lambda i, w: (0, i, 0))],
            out_specs=pl.BlockSpec((tr, C), lambda i, w: (i, 0))),
        out_shape=_sds((rh, C), F32),
        compiler_params=_params(("parallel",)),
    )(chip, sums, got)


def sum_devices(gathered, name):
    n_dev, M, N = gathered.shape

    def body(g_ref, o_ref):
        acc = g_ref[0]
        for d in range(1, n_dev):
            acc = acc + g_ref[d]
        o_ref[...] = acc

    tr = 8
    return pl.pallas_call(
        body, name=name, grid=(M // tr,),
        in_specs=[pl.BlockSpec((n_dev, tr, N), lambda i: (0, i, 0))], out_specs=pl.BlockSpec((tr, N), lambda i: (i, 0)),
        out_shape=_sds((M, N), F32), compiler_params=_params(("parallel",)),
    )(gathered)


def _pack(arrays):
    flat = jnp.concatenate([a.reshape(-1).astype(F32) for a in arrays])
    unit = 8 * PACK_COLS
    pad = (-flat.shape[0]) % unit
    return jnp.pad(flat, (0, pad)).reshape(-1, PACK_COLS)


def _unpack(packed, shapes):
    flat = packed.reshape(packed.shape[:-2] + (-1,))
    out, off = [], 0
    for s in shapes:
        n = math.prod(s)
        out.append(flat[..., off:off + n].reshape(packed.shape[:-2] + tuple(s)))
        off += n
    return out


def to_bf16(w, l, name):
    _, R, C = w.shape
    tr = _fit(512, [R])

    def body(w_ref, o_ref):
        o_ref[...] = w_ref[...].astype(BF16)

    return pl.pallas_call(
        body, name=name, grid=(R // tr,), in_specs=[pl.BlockSpec((None, tr, C), lambda i: (l, i, 0))],
        out_specs=pl.BlockSpec((tr, C), lambda i: (i, 0)), out_shape=_sds((R, C), BF16),
        compiler_params=_params(("parallel",)),
    )(w)


def kernel(x, c, ada_w, ada_b, norm_g, mlp_w1, mlp_w2, gmlp_w_in, gmlp_ln_g, gmlp_ln_b, gmlp_ws, gmlp_bs, gmlp_w_out, kv_norm_g, kv_ada_w, kv_ada_b, w_kv, k_norm_g, w_f, b_f, attn_wq, q_norm_g, attn_wo, loss_target, m_ada_w, m_ada_b, m_norm_g, m_mlp_w1, m_mlp_w2, m_gmlp_w_in, m_gmlp_ln_g, m_gmlp_ln_b, m_gmlp_ws, m_gmlp_bs, m_gmlp_w_out, m_kv_norm_g, m_kv_ada_w, m_kv_ada_b, m_w_kv, m_k_norm_g, m_w_f, m_b_f, m_attn_wq, m_q_norm_g, m_attn_wo, v_ada_w, v_ada_b, v_norm_g, v_mlp_w1, v_mlp_w2, v_gmlp_w_in, v_gmlp_ln_g, v_gmlp_ln_b, v_gmlp_ws, v_gmlp_bs, v_gmlp_w_out, v_kv_norm_g, v_kv_ada_w, v_kv_ada_b, v_w_kv, v_k_norm_g, v_w_f, v_b_f, v_attn_wq, v_q_norm_g, v_attn_wo):
    given = dict(locals())
    S, D = x.shape[1], x.shape[2]
    depth = ada_w.shape[0]
    n_a = gmlp_w_in.shape[0]
    H = D // HEAD_DIM
    G = gmlp_ws.shape[1]
    half = gmlp_w_out.shape[1] * 4
    n_dev = 8
    tq = min(512, S // 4)
    ax, ay, ac = _place()
    chip = 2 * ax + ay
    me = 4 * ax + 2 * ay + ac
    row = lambda v: v.reshape(1, -1)
    x0 = x[0]
    tgt = loss_target[0]

    small_in = [c, w_f, norm_g, gmlp_ln_g, gmlp_ln_b]
    g1 = allgather_small(_pack(small_in), "gather_small_params").reshape(n_dev, -1, PACK_COLS)
    c_all, wf_all, ng_all, lg_all, lb_all = _unpack(g1, [a.shape for a in small_in])
    c_all = c_all[:, 0, :]
    per_chip = lambda a: [a[2 * j] for j in range(4)]
    w_f_full = jnp.concatenate(per_chip(wf_all), axis=0)
    norm_g_full = jnp.concatenate(per_chip(ng_all), axis=-1)
    ln_g_full = jnp.concatenate(per_chip(lg_all), axis=-1)
    ln_b_full = jnp.concatenate(per_chip(lb_all), axis=-1)
    w_f_pad = jnp.pad(w_f_full, ((0, 0), (0, LANES - H))).astype(BF16)
    b_f_pad = jnp.pad(b_f, (0, LANES - H)).reshape(1, LANES)

    sc_all = rowmap(lambda cb: cb * jax.nn.sigmoid(cb), [c_all], [], [(D, F32)], tr=8, name="silu")[0]
    n_loc = ada_w.shape[2]
    ada_b_loc = lax.dynamic_slice_in_dim(ada_b, chip * n_loc, n_loc, axis=1).reshape(1, -1)
    add_bias = lambda acc, b: (acc + b,)
    mod_loc = mm(sc_all, ada_w, outs=[(F32, None)], epilogue=add_bias, extras=[ada_b_loc], name="ada_mod")[0]
    kv_loc_n = kv_ada_w.shape[1]
    kv_b_loc = lax.dynamic_slice_in_dim(kv_ada_b, chip * kv_loc_n, kv_loc_n).reshape(1, -1)
    kvmod_loc = mm(sc_all, kv_ada_w, outs=[(F32, None)], epilogue=add_bias, extras=[kv_b_loc], name="kv_ada_mod")[0]
    g2 = allgather_small(_pack([mod_loc, kvmod_loc]), "gather_mod").reshape(n_dev, -1, PACK_COLS)
    mod_all, kvmod_all = _unpack(g2, [mod_loc.shape, kvmod_loc.shape])
    mod_me = jnp.concatenate(
        [lax.dynamic_index_in_dim(m, me, 0, keepdims=False).reshape(depth, n_loc) for m in per_chip(mod_all)], axis=1)
    kvmod_me = jnp.concatenate([lax.dynamic_index_in_dim(m, me, 0, keepdims=False) for m in per_chip(kvmod_all)])
    mods = [[row(v) for v in jnp.split(mod_me[l], N_MOD)] for l in range(depth)]
    kv_shift, kv_scale = [row(v) for v in jnp.split(kvmod_me, 2)]

    big = ['mlp_w1', 'mlp_w2', 'gmlp_w_in', 'gmlp_w_out', 'w_kv', 'attn_wq', 'attn_wo']
    stacked = {n: (given[n] if given[n].ndim == 3 else given[n][None]) for n in big}
    owner = [(n, l) for n in big for l in range(stacked[n].shape[0])]
    gathered = allgather_weights([to_bf16(stacked[n], l, f"cast_{n}_{l}") for n, l in owner], "gather_weights")
    W = dict(zip(owner, gathered))
    rows_of = lambda a: a.reshape(-1, a.shape[-1])
    W1 = [W['mlp_w1', l] for l in range(depth)]
    W2 = [rows_of(W['mlp_w2', l]) for l in range(depth)]
    Win = [W['gmlp_w_in', a] for a in range(n_a)]
    Wout = [rows_of(W['gmlp_w_out', a]) for a in range(n_a)]
    Wkv = W['w_kv', 0]
    Wq = [rows_of(W['attn_wq', b]) for b in range(depth - n_a)]
    Wo = [rows_of(W['attn_wo', b]) for b in range(depth - n_a)]
    bsb = jnp.broadcast_to(gmlp_bs[..., None], gmlp_bs.shape + (LANES,))

    def resid(acc, xr, gate):
        return xr + gate * acc, acc

    saved = []
    xs = x0
    kv = None
    for l in range(depth):
        sh1, sc1, gt1, sh2, sc2, gt2 = mods[l]
        ng0, ng1 = row(norm_g_full[l, 0]), row(norm_g_full[l, 1])
        st = {"x": xs}
        h1 = modnorm_fwd(xs, ng0, sc1, sh1, f"norm1_{l}")
        st["h1"] = h1
        if l < n_a:
            t, z = mm(h1, Win[l], outs=[(BF16, None), (BF16, None)], epilogue=lambda acc: (acc, _gelu(acc)),
                      name=f"gmlp_in_{l}")
            vn = gmlp_ln_fwd(z, row(ln_g_full[l]), row(ln_b_full[l]), f"gmlp_ln_{l}")
            p = gmlp_mix_fwd(z, vn, gmlp_ws[l], bsb[l], f"gmlp_mix_{l}")
            x1, y = mm(p, Wout[l], outs=[(F32, None), (BF16, None)], epilogue=resid, extras=[xs, gt1],
                       name=f"gmlp_out_{l}")
            st.update(t=t, z=z, vn=vn, p=p, y=y)
        else:
            if kv is None:
                hk = modnorm_fwd(xs, row(kv_norm_g), kv_scale, kv_shift, "kv_norm")
                kg = row(k_norm_g)
                kp, kk = mm(hk, Wkv[:2], outs=[(BF16, None), (BF16, None)], extras=[kg],
                            epilogue=lambda acc, g: (acc, _head_norm(acc, g)), name="kv_k")
                vv = mm(hk, Wkv[2:], outs=[(BF16, None)], name="kv_v")[0]
                zf, logf = mm(hk, w_f_pad, outs=[(F32, None), (F32, None)], extras=[b_f_pad],
                              epilogue=lambda acc, b: (acc + b, jax.nn.log_sigmoid(acc + b)), name="kv_f")
                fc, fk = fcum_fwd(logf, tq, "fcum")
                kv = dict(x=xs, hk=hk, kp=kp, k=kk, v=vv, zf=zf, fc=fc, fk=fk)
            b = l - n_a
            qp, q = mm(h1, Wq[b], outs=[(BF16, None), (BF16, None)], extras=[row(q_norm_g[b])],
                       epilogue=lambda acc, g: (acc, _head_norm(acc, g)), name=f"attn_q_{l}")
            o, lse = fox_fwd(q, kv["k"], kv["v"], kv["fc"], kv["fk"], tq, f"fox_fwd_{l}")
            x1, y = mm(o, Wo[b], outs=[(F32, None), (BF16, None)], epilogue=resid, extras=[xs, gt1],
                       name=f"attn_o_{l}")
            st.update(qp=qp, q=q, o=o, lse=lse, y=y)
        st["x1"] = x1
        h2 = modnorm_fwd(x1, ng1, sc2, sh2, f"norm2_{l}")
        a_pre, a_sq = mm(h2, W1[l], outs=[(BF16, None), (BF16, None)],
                         epilogue=lambda acc: (acc, jnp.square(jnp.maximum(acc, 0.0))), name=f"mlp_up_{l}")
        xs, y2 = mm(a_sq, W2[l], outs=[(F32, None), (BF16, None)], epilogue=resid, extras=[x1, gt2],
                    name=f"mlp_down_{l}")
        st.update(h2=h2, a=a_pre, a_sq=a_sq, y2=y2)
        saved.append(st)

    def loss_fn(yb, tb_):
        e = yb - tb_
        return e * (1.0 / D), _colsum(e * e)
    dx, sq = rowmap(loss_fn, [xs, tgt], [], [(D, F32)], reds=[(1, D)], tr=512, name="loss")
    loss = lax.psum(0.5 * jnp.sum(sq) / D, ("x", "y", "c"))

    big_grads = {n: [None] * stacked[n].shape[0] for n in big}
    dmod = [None] * depth
    d_norm_g = [[None, None] for _ in range(depth)]
    small = {}
    d_ws, d_bs, d_lg, d_lb, d_qg = [None] * n_a, [None] * n_a, [None] * n_a, [None] * n_a, [None] * (depth - n_a)
    dk_parts, dv_parts, dfk_parts = [], [], []
    quarter = lambda g: g.reshape(4, g.shape[0] // 4, g.shape[1])
    for l in reversed(range(depth)):
        sh1, sc1, gt1, sh2, sc2, gt2 = mods[l]
        ng0, ng1 = row(norm_g_full[l, 0]), row(norm_g_full[l, 1])
        st = saved[l]
        dy2, dgt2 = gate_bwd(dx, st["y2"], gt2, f"gate2_bwd_{l}")
        big_grads['mlp_w2'][l] = quarter(mm(st["a_sq"], dy2, ta=True, outs=[(BF16, None)], name=f"mlp_w2_grad_{l}")[0])
        da = mm(dy2, W2[l], tb=True, outs=[(BF16, None)], extras=[st["a"]],
                epilogue=lambda acc, a: (acc * (2.0 * jnp.maximum(a.astype(F32), 0.0)),), name=f"mlp_down_bwd_{l}")[0]
        big_grads['mlp_w1'][l] = mm(st["h2"], da, ta=True, outs=[(BF16, 4)], name=f"mlp_w1_grad_{l}")[0]
        dh2 = mm(da, W1[l], tb=True, outs=[(BF16, None)], name=f"mlp_up_bwd_{l}")[0]
        dx, dsh2, d_norm_g[l][1], dsc2 = modnorm_bwd(st["x1"], dh2, dx, ng1, sc2, f"norm2_bwd_{l}")
        dy, dgt1 = gate_bwd(dx, st["y"], gt1, f"gate1_bwd_{l}")
        if l < n_a:
            big_grads['gmlp_w_out'][l] = quarter(
                mm(st["p"], dy, ta=True, outs=[(BF16, None)], name=f"gmlp_w_out_grad_{l}")[0])
            dp = mm(dy, Wout[l], tb=True, outs=[(BF16, None)], name=f"gmlp_out_bwd_{l}")[0]
            du, dvn, d_ws[l], db = gmlp_mix_bwd(dp, st["z"], st["vn"], gmlp_ws[l], bsb[l], f"gmlp_mix_bwd_{l}")
            d_bs[l] = db[:, :, 0]
            dt, d_lg[l], d_lb[l] = gmlp_act_bwd(st["t"], du, dvn, row(ln_g_full[l]), f"gmlp_act_bwd_{l}")
            big_grads['gmlp_w_in'][l] = mm(st["h1"], dt, ta=True, outs=[(BF16, 4)], name=f"gmlp_w_in_grad_{l}")[0]
            dh1 = mm(dt, Win[l], tb=True, outs=[(BF16, None)], name=f"gmlp_in_bwd_{l}")[0]
        else:
            b = l - n_a
            big_grads['attn_wo'][b] = quarter(
                mm(st["o"], dy, ta=True, outs=[(BF16, None)], name=f"attn_wo_grad_{l}")[0])
            do = mm(dy, Wo[b], tb=True, outs=[(BF16, None)], name=f"attn_o_bwd_{l}")[0]
            dq, rowterm = fox_dq(st["q"], kv["k"], kv["v"], do, st["lse"], kv["fc"], kv["fk"], tq, f"fox_dq_{l}")
            dk_l, dv_l, dfk_l = fox_dkv(st["q"], kv["k"], kv["v"], do, rowterm, st["lse"], kv["fc"], kv["fk"], tq,
                                        f"fox_dkv_{l}")
            dk_parts.append(dk_l)
            dv_parts.append(dv_l)
            dfk_parts.append(jnp.pad(dfk_l[:, :, 0, :].reshape(H, S), ((0, LANES - H), (0, 0))))
            dqp, d_qg[b] = head_norm_bwd(st["qp"], dq, row(q_norm_g[b]), f"q_norm_bwd_{l}")
            big_grads['attn_wq'][b] = quarter(
                mm(st["h1"], dqp, ta=True, outs=[(BF16, None)], name=f"attn_wq_grad_{l}")[0])
            dh1 = mm(dqp, Wq[b], tb=True, outs=[(BF16, None)], name=f"attn_q_bwd_{l}")[0]
        dx, dsh1, d_norm_g[l][0], dsc1 = modnorm_bwd(st["x"], dh1, dx, ng0, sc1, f"norm1_bwd_{l}")
        dmod[l] = jnp.concatenate([dsh1, dsc1, dgt1, dsh2, dsc2, dgt2], axis=1)
        if l == n_a:
            add2 = lambda a, b_: a.astype(F32) + b_.astype(F32)
            dk_sum = rowmap(add2, dk_parts, [], [(D, BF16)], tr=512, name="dk_sum")[0]
            dv_sum = rowmap(add2, dv_parts, [], [(D, BF16)], tr=512, name="dv_sum")[0]
            dkvp, small['k_norm_g'] = head_norm_bwd(kv["kp"], dk_sum, row(k_norm_g), "k_norm_bwd", extra=dv_sum)
            dzf, db_f = fcum_bwd(dfk_parts[0], dfk_parts[1], kv["zf"], H, "fcum_bwd")
            small['b_f'] = db_f[0, :H]
            g_wkv = mm(kv["hk"], dkvp, ta=True, outs=[(BF16, 4)], name="w_kv_grad")[0]
            small['w_f'] = mm(kv["hk"], dzf, ta=True, outs=[(F32, None)], name="w_f_grad")[0][:, :H]
            dhk_f = mm(dzf, w_f_pad, tb=True, outs=[(BF16, None)], name="kv_f_bwd")[0]
            dhk = mm(dkvp, Wkv, tb=True, outs=[(BF16, None)], extras=[dhk_f],
                     epilogue=lambda acc, e: (acc + e.astype(F32),), name="kv_bwd")[0]
            dx, dkv_shift, small['kv_norm_g'], dkv_scale = modnorm_bwd(
                kv["x"], dhk, dx, row(kv_norm_g), kv_scale, "kv_norm_bwd")
            dkvmod = jnp.concatenate([dkv_shift, dkv_scale], axis=1)
    grad_x = dx[None]

    small['norm_g'] = jnp.stack([jnp.concatenate(p, axis=0) for p in d_norm_g])
    small['gmlp_ln_g'] = jnp.concatenate(d_lg, axis=0)
    small['gmlp_ln_b'] = jnp.concatenate(d_lb, axis=0)
    small['gmlp_ws'] = jnp.stack(d_ws)
    small['gmlp_bs'] = jnp.stack(d_bs)
    small['q_norm_g'] = jnp.concatenate(d_qg, axis=0)
    small['ada_b'] = jnp.concatenate(dmod, axis=0)
    small['kv_ada_b'] = dkvmod
    names = sorted(small)
    shapes = [small[n].shape for n in names]
    g3 = allgather_small(_pack([small[n] for n in names]), "gather_small_grads").reshape(n_dev, -1, PACK_COLS)
    summed = dict(zip(names, _unpack(sum_devices(g3, "sum_small_grads"), shapes)))
    each = dict(zip(names, _unpack(g3, shapes)))
    local_cols = lambda a, n: lax.dynamic_slice_in_dim(a, chip * n, n, axis=a.ndim - 1)
    grads = {
        'ada_b': summed['ada_b'], 'gmlp_ws': summed['gmlp_ws'], 'gmlp_bs': summed['gmlp_bs'],
        'kv_norm_g': summed['kv_norm_g'].reshape(-1), 'kv_ada_b': summed['kv_ada_b'].reshape(-1),
        'k_norm_g': summed['k_norm_g'].reshape(-1), 'b_f': summed['b_f'], 'q_norm_g': summed['q_norm_g'],
        'norm_g': local_cols(summed['norm_g'], norm_g.shape[2]),
        'gmlp_ln_g': local_cols(summed['gmlp_ln_g'], gmlp_ln_g.shape[1]),
        'gmlp_ln_b': local_cols(summed['gmlp_ln_b'], gmlp_ln_b.shape[1]),
        'w_f': lax.dynamic_slice_in_dim(summed['w_f'], chip * w_f.shape[0], w_f.shape[0], axis=0),
    }
    dmod_all = each['ada_b'].reshape(n_dev, depth, N_MOD * D)
    dm_loc = jnp.transpose(local_cols(dmod_all, n_loc), (1, 0, 2))
    dkv_loc = local_cols(each['kv_ada_b'].reshape(n_dev, 2 * D), kv_loc_n)[None]

    big_grads['w_kv'] = [g_wkv]
    flat = [big_grads[n][l] for n, l in owner]
    which = ac.reshape(1).astype(jnp.int32)
    chip_i = chip.reshape(1).astype(jnp.int32)
    got = exchange_halves(flat, "grad_exchange_halves")
    sums = [add_halves(f, g, which, f"grad_chip_sum_{n}_{l}") for f, g, (n, l) in zip(flat, got, owner)]
    landed = scatter_chip_sums(sums, "grad_scatter")
    finished = [finish_sum(s, g, chip_i, f"grad_finish_{n}_{l}") for s, g, (n, l) in zip(sums, landed, owner)]
    groups = []
    for n in big:
        members = [t for t, (nn, _) in enumerate(owner) if nn == n]
        groups.append((stacked[n].shape, members))
    joined = join_halves(finished, groups, "grad_join")
    for n, g in zip(big, joined):
        grads[n] = g if n != 'w_kv' else g[0]

    delta, new_m, new_v = {}, {}, {}
    sct = jnp.transpose(sc_all)
    grads['ada_w'], delta['ada_w'], new_m['ada_w'], new_v['ada_w'] = adamw_outer(
        ada_w, sct, dm_loc, m_ada_w, v_ada_w, "adamw_ada_w")
    r = adamw_outer(kv_ada_w[None], sct, dkv_loc, m_kv_ada_w[None], v_kv_ada_w[None], "adamw_kv_ada_w")
    grads['kv_ada_w'], delta['kv_ada_w'], new_m['kv_ada_w'], new_v['kv_ada_w'] = [a[0] for a in r]
    for n in WEIGHTS:
        if n in delta:
            continue
        grads[n] = grads[n].reshape(given[n].shape)
        delta[n], new_m[n], new_v[n] = adamw(given[n], grads[n], given["m_" + n], given["v_" + n], "adamw_" + n)
    return (loss, grad_x, *[grads[n] for n in WEIGHTS], *[delta[n] for n in WEIGHTS],
            *[new_m[n] for n in WEIGHTS], *[new_v[n] for n in WEIGHTS])
```

```python
import functools
import math

import jax
import jax.numpy as jnp
from jax import lax
from jax.experimental import pallas as pl
from jax.experimental.pallas import tpu as pltpu

F32 = jnp.float32
BF16 = jnp.bfloat16
EPS = 1e-6
HEAD_DIM = 128
GMLP_BLOCK = 128
CHUNK = 64
LANES = 128
N_MOD = 6
V7X_VMEM_BYTES = 64 * 2**20
VMEM_LIMIT = V7X_VMEM_BYTES - 8 * 2**20
PACK_COLS = 1024
NEG = -1e30
MESH = pl.DeviceIdType.MESH

ADAM_LR = 0.001
ADAM_B1 = 0.9
ADAM_B2 = 0.999
ADAM_EPS = 1e-08
ADAM_WD = 0.01
ADAM_STEP = 10

WEIGHTS = ['ada_w', 'ada_b', 'norm_g', 'mlp_w1', 'mlp_w2', 'gmlp_w_in', 'gmlp_ln_g', 'gmlp_ln_b', 'gmlp_ws',
           'gmlp_bs', 'gmlp_w_out', 'kv_norm_g', 'kv_ada_w', 'kv_ada_b', 'w_kv', 'k_norm_g', 'w_f', 'b_f',
           'attn_wq', 'q_norm_g', 'attn_wo']


def _params(sem=None):
    return pltpu.CompilerParams(dimension_semantics=sem, vmem_limit_bytes=VMEM_LIMIT)


def _sds(shape, dtype):
    return jax.ShapeDtypeStruct(tuple(shape), dtype)


def _ldims(shape):
    return (shape[0], shape[1]) if len(shape) == 2 else (shape[1], shape[0] * shape[2])


def _fit(t, ns):
    n0 = min(ns)
    if n0 <= t and all(n % n0 == 0 for n in ns):
        return n0
    d = (t // LANES) * LANES
    while d > LANES and any(n % d for n in ns):
        d -= LANES
    assert all(n % d == 0 for n in ns), (t, ns)
    return d


def _blk(shape, br, bc):
    if len(shape) == 2:
        return (br, bc), (lambda r, c: (r, c))
    per = shape[2] // bc
    assert shape[2] % bc == 0, (shape, bc)
    return (None, br, bc), (lambda r, c: (c // per, r, c % per))


def mm(a, b, *, name, ta=False, tb=False, outs, epilogue=None, extras=(), tm=1024, tn=1024, tk=2048,
       precision=None):
    ar, ac = _ldims(a.shape)
    br, bc = _ldims(b.shape)
    M, K = (ac, ar) if ta else (ar, ac)
    K2, N = (bc, br) if tb else (br, bc)
    assert K == K2, (name, a.shape, b.shape)
    cons = {"m": [M], "n": [N], "k": [K]}

    def note(shape, dim):
        if len(shape) == 3:
            cons[dim].append(shape[2])

    note(a.shape, "m" if ta else "k")
    note(b.shape, "k" if tb else "n")
    out_shapes = []
    for dt, nb in outs:
        if nb is None:
            out_shapes.append(_sds((M, N), dt))
        else:
            out_shapes.append(_sds((nb, M, N // nb), dt))
            cons["n"].append(N // nb)
    for e in extras:
        note(e.shape, "n")
    tm, tn, tk = _fit(tm, cons["m"]), _fit(tn, cons["n"]), _fit(tk, cons["k"])
    gm, gn, gk = M // tm, N // tn, K // tk

    a_bs, a_ix = _blk(a.shape, tk if ta else tm, tm if ta else tk)
    b_bs, b_ix = _blk(b.shape, tn if tb else tk, tk if tb else tn)
    in_specs = [
        pl.BlockSpec(a_bs, (lambda i, j, k: a_ix(k, i)) if ta else (lambda i, j, k: a_ix(i, k))),
        pl.BlockSpec(b_bs, (lambda i, j, k: b_ix(j, k)) if tb else (lambda i, j, k: b_ix(k, j))),
    ]
    for e in extras:
        if _ldims(e.shape)[0] == 1 and M != 1:
            if e.shape[1] == N:
                in_specs.append(pl.BlockSpec((1, tn), lambda i, j, k: (0, j)))
            else:
                in_specs.append(pl.BlockSpec(e.shape, lambda i, j, k: (0, 0)))
        else:
            e_bs, e_ix = _blk(e.shape, tm, tn)
            in_specs.append(pl.BlockSpec(e_bs, functools.partial(lambda i, j, k, ix: ix(i, j), ix=e_ix)))
    out_specs = []
    for s in out_shapes:
        o_bs, o_ix = _blk(s.shape, tm, tn)
        out_specs.append(pl.BlockSpec(o_bs, functools.partial(lambda i, j, k, ix: ix(i, j), ix=o_ix)))
    n_e, n_o = len(extras), len(outs)
    dims = (((0 if ta else 1,), (1 if tb else 0,)), ((), ()))

    def body(*refs):
        a_ref, b_ref = refs[:2]
        e_refs = refs[2:2 + n_e]
        o_refs = refs[2 + n_e:2 + n_e + n_o]
        x, w = a_ref[...], b_ref[...]
        if precision is None:
            x, w = x.astype(BF16), w.astype(BF16)
        d = lax.dot_general(x, w, dims, preferred_element_type=F32, precision=precision)

        def finish(acc):
            res = epilogue(acc, *[e[...] for e in e_refs]) if epilogue is not None else (acc,)
            for o, r in zip(o_refs, res):
                o[...] = r.astype(o.dtype)

        if gk == 1:
            finish(d)
        else:
            acc_ref = refs[-1]
            k = pl.program_id(2)

            @pl.when(k == 0)
            def _():
                acc_ref[...] = d

            @pl.when(jnp.logical_and(k > 0, k < gk - 1))
            def _():
                acc_ref[...] += d

            @pl.when(k == gk - 1)
            def _():
                finish(acc_ref[...] + d)

    return pl.pallas_call(
        body, name=name, grid=(gm, gn, gk), in_specs=in_specs, out_specs=out_specs, out_shape=out_shapes,
        scratch_shapes=[pltpu.VMEM((tm, tn), F32)] if gk > 1 else [],
        compiler_params=_params(("parallel", "parallel", "arbitrary")),
    )(a, b, *extras)


def rowmap(fn, rows, vecs, outs, reds=(), *, tr, name):
    rows = [r if isinstance(r, tuple) else (r, r.shape[1], 0) for r in rows]
    S = rows[0][0].shape[0]
    tr = min(tr, S)
    assert S % tr == 0, (name, S, tr)
    n_i, n_o = len(rows) + len(vecs), len(outs)
    in_specs = [pl.BlockSpec((tr, w), functools.partial(lambda i, c: (i, c), c=c)) for _, w, c in rows]
    in_specs += [pl.BlockSpec(v.shape, functools.partial(lambda i, n: (0,) * n, n=v.ndim)) for v in vecs]
    out_shape = [_sds((S, f), dt) for f, dt in outs] + [_sds(s, F32) for s in reds]
    out_specs = [pl.BlockSpec((tr, f), lambda i: (i, 0)) for f, _ in outs]
    out_specs += [pl.BlockSpec(s, functools.partial(lambda i, n: (0,) * n, n=len(s))) for s in reds]

    def body(*refs):
        res = fn(*[r[...] for r in refs[:n_i]])
        res = res if isinstance(res, tuple) else (res,)
        for o, r in zip(refs[n_i:n_i + n_o], res[:n_o]):
            o[...] = r.astype(o.dtype)
        if reds:
            d_refs = refs[n_i + n_o:]

            @pl.when(pl.program_id(0) == 0)
            def _():
                for d in d_refs:
                    d[...] = jnp.zeros(d.shape, F32)

            for d, r in zip(d_refs, res[n_o:]):
                d[...] += r

    return pl.pallas_call(
        body, name=name, grid=(S // tr,), in_specs=in_specs, out_specs=out_specs, out_shape=out_shape,
        compiler_params=_params(("arbitrary",) if reds else ("parallel",)),
    )(*[r[0] for r in rows], *vecs)


def _gelu(t):
    return 0.5 * t * (1.0 + lax.erf(t * (1.0 / math.sqrt(2.0))))


def _gelu_grad(t):
    cdf = 0.5 * (1.0 + lax.erf(t * (1.0 / math.sqrt(2.0))))
    return cdf + t * jnp.exp(-0.5 * t * t) * (1.0 / math.sqrt(2.0 * math.pi))


def _colsum(v):
    return jnp.sum(v, axis=0, keepdims=True)


def modnorm_fwd(x, g, scale, shift, name):
    def fn(xb, gb, sc, sh):
        rstd = lax.rsqrt(jnp.mean(xb * xb, axis=-1, keepdims=True) + EPS)
        return ((xb * rstd) * gb) * (1.0 + sc) + sh
    return rowmap(fn, [x], [g, scale, shift], [(x.shape[1], BF16)], tr=512, name=name)[0]


def modnorm_bwd(x, dh, dres, g, scale, name):
    D = x.shape[1]

    def fn(xb, dhb, drb, gb, sc):
        dhb = dhb.astype(F32)
        rstd = lax.rsqrt(jnp.mean(xb * xb, axis=-1, keepdims=True) + EPS)
        xhat = xb * rstd
        a = gb * (1.0 + sc)
        dxhat = dhb * a
        dx = rstd * (dxhat - xhat * jnp.mean(dxhat * xhat, axis=-1, keepdims=True))
        da = _colsum(dhb * xhat)
        return drb + dx, _colsum(dhb), da * (1.0 + sc), da * gb
    return rowmap(fn, [x, dh, dres], [g, scale], [(D, F32)], reds=[(1, D)] * 3, tr=256, name=name)


def gate_bwd(dx, y, gate, name):
    D = dx.shape[1]

    def fn(dxb, yb, gb):
        return gb * dxb, _colsum(dxb * yb.astype(F32))
    return rowmap(fn, [dx, y], [gate], [(D, BF16)], reds=[(1, D)], tr=512, name=name)


def _head_norm(x, g):
    parts = []
    for h in range(x.shape[1] // HEAD_DIM):
        xh = x[:, h * HEAD_DIM:(h + 1) * HEAD_DIM]
        rstd = lax.rsqrt(jnp.mean(xh * xh, axis=-1, keepdims=True) + EPS)
        parts.append(xh * rstd * g)
    return jnp.concatenate(parts, axis=1)


def head_norm_bwd(xp, dy, g, name, extra=None):
    D = xp.shape[1]

    def fn(*blocks):
        xb, dyb = blocks[0].astype(F32), blocks[1].astype(F32)
        gb = blocks[-1]
        parts, dg = [], jnp.zeros((1, HEAD_DIM), F32)
        for h in range(D // HEAD_DIM):
            xh = xb[:, h * HEAD_DIM:(h + 1) * HEAD_DIM]
            dyh = dyb[:, h * HEAD_DIM:(h + 1) * HEAD_DIM]
            rstd = lax.rsqrt(jnp.mean(xh * xh, axis=-1, keepdims=True) + EPS)
            xhat = xh * rstd
            dg = dg + _colsum(dyh * xhat)
            dxhat = dyh * gb
            parts.append(rstd * (dxhat - xhat * jnp.mean(dxhat * xhat, axis=-1, keepdims=True)))
        if extra is not None:
            parts.append(blocks[2].astype(F32))
        return jnp.concatenate(parts, axis=1), dg
    rows = [xp, dy] + ([extra] if extra is not None else [])
    width = D + (extra.shape[1] if extra is not None else 0)
    return rowmap(fn, rows, [g], [(width, BF16)], reds=[(1, HEAD_DIM)], tr=512, name=name)


def gmlp_ln_fwd(z, ln_g, ln_b, name):
    half = z.shape[1] // 2

    def fn(vb, gb, bb):
        vb = vb.astype(F32)
        mu = jnp.mean(vb, axis=-1, keepdims=True)
        var = jnp.mean(jnp.square(vb - mu), axis=-1, keepdims=True)
        return ((vb - mu) * lax.rsqrt(var + EPS)) * gb + bb
    return rowmap(fn, [(z, half, 1)], [ln_g, ln_b], [(half, BF16)], tr=256, name=name)[0]


def _mix_mask():
    r = lax.broadcasted_iota(jnp.int32, (GMLP_BLOCK, GMLP_BLOCK), 0) // CHUNK
    c = lax.broadcasted_iota(jnp.int32, (GMLP_BLOCK, GMLP_BLOCK), 1) // CHUNK
    return c <= r


def gmlp_mix_fwd(z, vn, ws, bsb, name):
    S, half = vn.shape
    G = ws.shape[0]
    gd = half // G
    tb = min(512, S // 2)

    def body(u_ref, v_ref, w_ref, b_ref, p_ref):
        w = jnp.where(_mix_mask(), w_ref[...], 0.0).astype(BF16)
        bcol = b_ref[:, 0:1]
        for r in range(tb // GMLP_BLOCK):
            rs = slice(r * GMLP_BLOCK, (r + 1) * GMLP_BLOCK)
            sv = jnp.dot(w, v_ref[rs, :], preferred_element_type=F32) + bcol
            p_ref[rs, :] = (u_ref[rs, :].astype(F32) * sv).astype(p_ref.dtype)

    return pl.pallas_call(
        body, name=name, grid=(S // tb, G),
        in_specs=[pl.BlockSpec((tb, gd), lambda n, g: (n, g)), pl.BlockSpec((tb, gd), lambda n, g: (n, g)),
                  pl.BlockSpec((None, GMLP_BLOCK, GMLP_BLOCK), lambda n, g: (g, 0, 0)),
                  pl.BlockSpec((None, GMLP_BLOCK, LANES), lambda n, g: (g, 0, 0))],
        out_specs=pl.BlockSpec((tb, gd), lambda n, g: (n, g)),
        out_shape=_sds((S, half), BF16),
        compiler_params=_params(("parallel", "parallel")),
    )(z, vn, ws, bsb)


def gmlp_mix_bwd(dp, z, vn, ws, bsb, name):
    S, half = vn.shape
    G = ws.shape[0]
    gd = half // G
    tb = min(512, S // 2)

    def body(dp_ref, u_ref, v_ref, w_ref, b_ref, du_ref, dv_ref, dw_ref, db_ref):
        n = pl.program_id(1)
        mask = _mix_mask()
        w = jnp.where(mask, w_ref[...], 0.0).astype(BF16)
        bcol = b_ref[:, 0:1]
        dw = jnp.zeros((GMLP_BLOCK, GMLP_BLOCK), F32)
        db = jnp.zeros((GMLP_BLOCK, 1), F32)
        for r in range(tb // GMLP_BLOCK):
            rs = slice(r * GMLP_BLOCK, (r + 1) * GMLP_BLOCK)
            vb = v_ref[rs, :]
            dpb = dp_ref[rs, :].astype(F32)
            sv = jnp.dot(w, vb, preferred_element_type=F32) + bcol
            du_ref[rs, :] = (dpb * sv).astype(du_ref.dtype)
            dsv = dpb * u_ref[rs, :].astype(F32)
            dsv16 = dsv.astype(BF16)
            dv_ref[rs, :] = lax.dot_general(w, dsv16, (((0,), (0,)), ((), ())),
                                            preferred_element_type=F32).astype(dv_ref.dtype)
            dw = dw + lax.dot_general(dsv16, vb, (((1,), (1,)), ((), ())), preferred_element_type=F32)
            db = db + jnp.sum(dsv, axis=1, keepdims=True)
        dw = jnp.where(mask, dw, 0.0)
        db = jnp.broadcast_to(db, (GMLP_BLOCK, LANES))

        @pl.when(n == 0)
        def _():
            dw_ref[...] = dw
            db_ref[...] = db

        @pl.when(n > 0)
        def _():
            dw_ref[...] += dw
            db_ref[...] += db

    blk = pl.BlockSpec((tb, gd), lambda g, n: (n, g))
    return pl.pallas_call(
        body, name=name, grid=(G, S // tb),
        in_specs=[blk, blk, blk, pl.BlockSpec((None, GMLP_BLOCK, GMLP_BLOCK), lambda g, n: (g, 0, 0)),
                  pl.BlockSpec((None, GMLP_BLOCK, LANES), lambda g, n: (g, 0, 0))],
        out_specs=[blk, blk, pl.BlockSpec((None, GMLP_BLOCK, GMLP_BLOCK), lambda g, n: (g, 0, 0)),
                   pl.BlockSpec((None, GMLP_BLOCK, LANES), lambda g, n: (g, 0, 0))],
        out_shape=[_sds((S, half), BF16), _sds((S, half), BF16), _sds((G, GMLP_BLOCK, GMLP_BLOCK), F32),
                   _sds((G, GMLP_BLOCK, LANES), F32)],
        compiler_params=_params(("parallel", "arbitrary")),
    )(dp, z, vn, ws, bsb)


def gmlp_act_bwd(t, du, dvn, ln_g, name):
    half = du.shape[1]

    def fn(tb_, dub, dvb, gb):
        tb_ = tb_.astype(F32)
        tu, tv = tb_[:, :half], tb_[:, half:]
        dtu = dub.astype(F32) * _gelu_grad(tu)
        v = _gelu(tv)
        mu = jnp.mean(v, axis=-1, keepdims=True)
        vc = v - mu
        rstd = lax.rsqrt(jnp.mean(vc * vc, axis=-1, keepdims=True) + EPS)
        vhat = vc * rstd
        dvb = dvb.astype(F32)
        dvhat = dvb * gb
        dv = rstd * (dvhat - jnp.mean(dvhat, axis=-1, keepdims=True)
                     - vhat * jnp.mean(dvhat * vhat, axis=-1, keepdims=True))
        dtv = dv * _gelu_grad(tv)
        return jnp.concatenate([dtu, dtv], axis=1), _colsum(dvb * vhat), _colsum(dvb)
    return rowmap(fn, [t, du, dvn], [ln_g], [(2 * half, BF16)], reds=[(1, half)] * 2, tr=128, name=name)


def fcum_fwd(logf, tb, name):
    S = logf.shape[0]
    nb = S // tb

    def body(x_ref, fc_ref, fk_ref):
        tri = (lax.broadcasted_iota(jnp.int32, (LANES, LANES), 0)
               >= lax.broadcasted_iota(jnp.int32, (LANES, LANES), 1)).astype(F32)

        def blk(b, carry):
            off = pl.multiple_of(b * LANES, LANES)
            cs = jnp.dot(tri, x_ref[pl.ds(off, LANES), :], preferred_element_type=F32,
                         precision=lax.Precision.HIGHEST) + carry
            fc_ref[pl.ds(off, LANES), :] = cs
            return cs[LANES - 1:LANES, :]

        lax.fori_loop(0, S // LANES, blk, jnp.zeros((1, LANES), F32))
        for b in range(nb):
            fk_ref[b] = fc_ref[b * tb:(b + 1) * tb, :].T

    return pl.pallas_call(
        body, name=name, out_shape=[_sds((S, LANES), F32), _sds((nb, LANES, tb), F32)],
        compiler_params=_params(),
    )(logf)


def fcum_bwd(dfk_a, dfk_b, zf, n_heads, name):
    S = zf.shape[0]

    def body(da_ref, db_ref, zf_ref, dz_ref, dsum_ref, d_ref):
        d_ref[...] = (da_ref[...] + db_ref[...]).T
        triu = (lax.broadcasted_iota(jnp.int32, (LANES, LANES), 0)
                <= lax.broadcasted_iota(jnp.int32, (LANES, LANES), 1)).astype(F32)
        nblk = S // LANES
        live = lax.broadcasted_iota(jnp.int32, (LANES, LANES), 1) < n_heads

        def blk(r, carry):
            carry_row, tot = carry
            off = pl.multiple_of((nblk - 1 - r) * LANES, LANES)
            d_blk = d_ref[pl.ds(off, LANES), :]
            cs = jnp.dot(triu, d_blk, preferred_element_type=F32, precision=lax.Precision.HIGHEST) + carry_row
            dz = jnp.where(live, cs * jax.nn.sigmoid(-zf_ref[pl.ds(off, LANES), :]), 0.0)
            dz_ref[pl.ds(off, LANES), :] = dz.astype(dz_ref.dtype)
            return carry_row + _colsum(d_blk), tot + _colsum(dz)

        _, tot = lax.fori_loop(0, nblk, blk, (jnp.zeros((1, LANES), F32), jnp.zeros((1, LANES), F32)))
        dsum_ref[...] = tot

    return pl.pallas_call(
        body, name=name, out_shape=[_sds((S, LANES), BF16), _sds((1, LANES), F32)],
        scratch_shapes=[pltpu.VMEM((S, LANES), F32)], compiler_params=_params(),
    )(dfk_a, dfk_b, zf)


def _causal(tq):
    return (lax.broadcasted_iota(jnp.int32, (tq, tq), 1) <= lax.broadcasted_iota(jnp.int32, (tq, tq), 0))


_NT = (((1,), (1,)), ((), ()))
_TN = (((0,), (0,)), ((), ()))


def fox_fwd(q, k, v, fk, tq, name):
    S, D = q.shape
    H, nq = D // HEAD_DIM, S // tq

    def body(q_ref, k_ref, v_ref, fk_ref, o_ref, lse_ref):
        h, i = pl.program_id(0), pl.program_id(1)
        qb = q_ref[...]
        hs = h % 8

        def step(j, carry, masked):
            m, l, acc = carry
            off = pl.multiple_of(j * tq, tq)
            kb, vb = k_ref[pl.ds(off, tq), :], v_ref[pl.ds(off, tq), :]
            s = lax.dot_general(qb, kb, _NT, preferred_element_type=F32) - fk_ref[j, pl.ds(hs, 1), :]
            if masked:
                s = jnp.where(_causal(tq), s, NEG)
            m_new = jnp.maximum(m, jnp.max(s, axis=1, keepdims=True))
            alpha = jnp.exp(m - m_new)
            p = jnp.exp(s - m_new)
            l = alpha * l + jnp.sum(p, axis=1, keepdims=True)
            acc = alpha * acc + jnp.dot(p.astype(BF16), vb, preferred_element_type=F32)
            return m_new, l, acc

        init = (jnp.full((tq, 1), NEG, F32), jnp.zeros((tq, 1), F32), jnp.zeros((tq, HEAD_DIM), F32))
        carry = lax.fori_loop(0, i, lambda j, c: step(j, c, False), init)
        m, l, acc = step(i, carry, True)
        o_ref[...] = (acc / l).astype(o_ref.dtype)
        lse_ref[...] = jnp.broadcast_to(m + jnp.log(l), (tq, LANES))

    return pl.pallas_call(
        body, name=name, grid=(H, nq),
        in_specs=[pl.BlockSpec((tq, HEAD_DIM), lambda h, i: (i, h)),
                  pl.BlockSpec((S, HEAD_DIM), lambda h, i: (0, h)),
                  pl.BlockSpec((S, HEAD_DIM), lambda h, i: (0, h)),
                  pl.BlockSpec((nq, 8, tq), lambda h, i: (0, h // 8, 0))],
        out_specs=[pl.BlockSpec((tq, HEAD_DIM), lambda h, i: (i, h)),
                   pl.BlockSpec((None, tq, LANES), lambda h, i: (h, i, 0))],
        out_shape=[_sds((S, D), BF16), _sds((H, S, LANES), F32)],
        compiler_params=_params(("parallel", "arbitrary")),
    )(q, k, v, fk)


def fox_dq(q, k, v, do, lse, fk, tq, name):
    S, D = q.shape
    H, nq = D // HEAD_DIM, S // tq
    scale = 1.0 / math.sqrt(HEAD_DIM)

    def body(q_ref, k_ref, v_ref, do_ref, lse_ref, fk_ref, dq_ref, row_ref):
        h, i = pl.program_id(0), pl.program_id(1)
        qb, dob = q_ref[...], do_ref[...]
        lsec = lse_ref[:, 0:1]
        hs = h % 8

        def p_dp(j, masked):
            off = pl.multiple_of(j * tq, tq)
            kb, vb = k_ref[pl.ds(off, tq), :], v_ref[pl.ds(off, tq), :]
            s = lax.dot_general(qb, kb, _NT, preferred_element_type=F32) - fk_ref[j, pl.ds(hs, 1), :]
            if masked:
                s = jnp.where(_causal(tq), s, NEG)
            return jnp.exp(s - lsec), lax.dot_general(dob, vb, _NT, preferred_element_type=F32), kb

        def sums(j, carry, masked):
            p, dp, _ = p_dp(j, masked)
            return carry[0] + jnp.sum(p * dp, axis=1, keepdims=True), carry[1] + jnp.sum(p, axis=1, keepdims=True)

        zero = jnp.zeros((tq, 1), F32)
        carry = lax.fori_loop(0, i, lambda j, c: sums(j, c, False), (zero, zero))
        num, den = sums(i, carry, True)
        rowterm = num / den

        def step(j, acc, masked):
            p, dp, kb = p_dp(j, masked)
            ds = p * (dp - rowterm)
            return acc + jnp.dot(ds.astype(BF16), kb, preferred_element_type=F32)

        acc = lax.fori_loop(0, i, lambda j, c: step(j, c, False), jnp.zeros((tq, HEAD_DIM), F32))
        acc = step(i, acc, True)
        dq_ref[...] = (acc * scale).astype(dq_ref.dtype)
        row_ref[...] = jnp.broadcast_to(rowterm, (tq, LANES))

    tile = pl.BlockSpec((tq, HEAD_DIM), lambda h, i: (i, h))
    full = pl.BlockSpec((S, HEAD_DIM), lambda h, i: (0, h))
    stat = pl.BlockSpec((None, tq, LANES), lambda h, i: (h, i, 0))
    return pl.pallas_call(
        body, name=name, grid=(H, nq),
        in_specs=[tile, full, full, tile, stat, pl.BlockSpec((nq, 8, tq), lambda h, i: (0, h // 8, 0))],
        out_specs=[tile, stat], out_shape=[_sds((S, D), BF16), _sds((H, S, LANES), F32)],
        compiler_params=_params(("parallel", "arbitrary")),
    )(q, k, v, do, lse, fk)


def fox_dkv(q, k, v, do, rowterm, lse, fk, tq, name):
    S, D = q.shape
    H, nq = D // HEAD_DIM, S // tq

    def body(q_ref, k_ref, v_ref, do_ref, row_ref, lse_ref, fk_ref, dk_ref, dv_ref, dfk_ref):
        h, j = pl.program_id(0), pl.program_id(1)
        kb, vb = k_ref[...], v_ref[...]
        fkr = fk_ref[pl.ds(h % 8, 1), :]

        def step(i, carry, masked):
            dk, dv, dfk = carry
            off = pl.multiple_of(i * tq, tq)
            qb, dob = q_ref[pl.ds(off, tq), :], do_ref[pl.ds(off, tq), :]
            delta = row_ref[pl.ds(off, tq), 0:1]
            lsec = lse_ref[pl.ds(off, tq), 0:1]
            s = lax.dot_general(qb, kb, _NT, preferred_element_type=F32) - fkr
            if masked:
                s = jnp.where(_causal(tq), s, NEG)
            p = jnp.exp(s - lsec)
            dv = dv + lax.dot_general(p.astype(BF16), dob, _TN, preferred_element_type=F32)
            dp = lax.dot_general(dob, vb, _NT, preferred_element_type=F32)
            ds = p * (dp - delta)
            dk = dk + lax.dot_general(ds.astype(BF16), qb, _TN, preferred_element_type=F32)
            return dk, dv, dfk - _colsum(ds)

        init = (jnp.zeros((tq, HEAD_DIM), F32), jnp.zeros((tq, HEAD_DIM), F32), jnp.zeros((1, tq), F32))
        carry = step(j, init, True)
        dk, dv, dfk = lax.fori_loop(j + 1, nq, lambda i, c: step(i, c, False), carry)
        dk_ref[...] = dk.astype(dk_ref.dtype)
        dv_ref[...] = dv.astype(dv_ref.dtype)
        dfk_ref[...] = jnp.broadcast_to(dfk, (8, tq))

    tile = pl.BlockSpec((tq, HEAD_DIM), lambda h, j: (j, h))
    full = pl.BlockSpec((S, HEAD_DIM), lambda h, j: (0, h))
    stat = pl.BlockSpec((None, S, LANES), lambda h, j: (h, 0, 0))
    return pl.pallas_call(
        body, name=name, grid=(H, nq),
        in_specs=[full, tile, tile, full, stat, stat, pl.BlockSpec((None, 8, tq), lambda h, j: (j, h // 8, 0))],
        out_specs=[tile, tile, pl.BlockSpec((None, None, 8, tq), lambda h, j: (h, j, 0, 0))],
        out_shape=[_sds((S, D), BF16), _sds((S, D), BF16), _sds((H, nq, 8, tq), F32)],
        compiler_params=_params(("parallel", "arbitrary")),
    )(q, k, v, do, rowterm, lse, fk)


def _adamw_math(w, g, m, v):
    m = ADAM_B1 * m + (1.0 - ADAM_B1) * g
    v = ADAM_B2 * v + (1.0 - ADAM_B2) * jnp.square(g)
    m_hat = m / (1.0 - ADAM_B1 ** ADAM_STEP)
    v_hat = v / (1.0 - ADAM_B2 ** ADAM_STEP)
    delta = -ADAM_LR * (m_hat / (jnp.sqrt(v_hat) + ADAM_EPS) + ADAM_WD * w)
    return delta, m, v


def adamw(w, g, m, v, name):
    shape = w.shape
    cols = shape[-1]
    two_d = lambda a: a.reshape(-1, cols)
    rows = max(1, w.size // cols)
    tr = rows if rows * cols * 4 <= 2**21 else max(8, (2**21 // (cols * 4)) // 8 * 8)
    while rows % tr:
        tr -= 8
    res = rowmap(_adamw_math, [two_d(w), two_d(g), two_d(m), two_d(v)], [], [(cols, F32)] * 3, tr=tr, name=name)
    return [r.reshape(shape) for r in res]


def adamw_outer(w, sct, dm, m, v, name):
    L, R, C = w.shape
    B = sct.shape[1]
    tr = min(R, 256)

    def body(w_ref, s_ref, d_ref, m_ref, v_ref, g_out, dl_out, m_out, v_out):
        g = jnp.dot(s_ref[...], d_ref[...], preferred_element_type=F32, precision=lax.Precision.HIGHEST)
        delta, mn, vn = _adamw_math(w_ref[...], g, m_ref[...], v_ref[...])
        g_out[...] = g
        dl_out[...] = delta
        m_out[...] = mn
        v_out[...] = vn

    big = pl.BlockSpec((None, tr, C), lambda l, i: (l, i, 0))
    return pl.pallas_call(
        body, name=name, grid=(L, R // tr),
        in_specs=[big, pl.BlockSpec((tr, B), lambda l, i: (i, 0)), pl.BlockSpec((None, B, C), lambda l, i: (l, 0, 0)),
                  big, big],
        out_specs=[big] * 4, out_shape=[_sds((L, R, C), F32)] * 4,
        compiler_params=_params(("parallel", "parallel")),
    )(w, sct, dm, m, v)


def _place():
    x, y, c = lax.axis_index("x"), lax.axis_index("y"), lax.axis_index("c")
    return x, y, c


def _other_chips(x, y):
    return [(1 - x, y), (x, 1 - y), (1 - x, 1 - y)]


def allgather_small(block, name):
    m_per, n = block.shape

    def body(x_ref, out_ref, send_sems, recv_sems, local_sem):
        x, y, c = _place()
        me, sibling = (x, y, c), (x, y, 1 - c)
        chips = _other_chips(x, y)

        def rows(px, py, pc):
            return out_ref.at[pl.ds((4 * px + 2 * py + pc) * m_per, m_per), :]

        def copy(k, block_of, to, src=None):
            return pltpu.make_async_remote_copy(
                src_ref=rows(*block_of) if src is None else src, dst_ref=rows(*block_of),
                send_sem=send_sems.at[k], recv_sem=recv_sems.at[k], device_id=to, device_id_type=MESH)

        mine = pltpu.make_async_copy(x_ref, rows(*me), local_sem)
        mine.start()
        first = [copy(0, me, sibling, src=x_ref)]
        first += [copy(1 + j, me, (*chip, c), src=x_ref) for j, chip in enumerate(chips)]
        for cp in first:
            cp.start()
        passed = [copy(4 + j, (*chip, c), sibling) for j, chip in enumerate(chips)]
        for j, chip in enumerate(chips):
            copy(1 + j, (*chip, c), me).wait_recv()
            passed[j].start()
        copy(0, sibling, me).wait_recv()
        for j, chip in enumerate(chips):
            copy(4 + j, (*chip, 1 - c), me).wait_recv()
        for cp in first + passed:
            cp.wait_send()
        mine.wait()

    return pl.pallas_call(
        body, name=name, out_shape=_sds((8 * m_per, n), block.dtype),
        in_specs=[pl.BlockSpec(memory_space=pltpu.VMEM)], out_specs=pl.BlockSpec(memory_space=pltpu.VMEM),
        scratch_shapes=[pltpu.SemaphoreType.DMA((7,)), pltpu.SemaphoreType.DMA((7,)), pltpu.SemaphoreType.DMA],
        compiler_params=_params(),
    )(block)


def _half(ref, which):
    n = ref.shape[-2] // 2
    idx = (slice(None),) * (len(ref.shape) - 2) + (pl.ds(which * n, n), slice(None))
    return ref.at[idx]


_ANY = pl.BlockSpec(memory_space=pl.ANY)


def allgather_weights(bufs, name):
    T = len(bufs)

    def body(*refs):
        ins, outs = refs[:T], refs[T:2 * T]
        send_sems, recv_sems, fsend_sems, frecv_sems = refs[2 * T:]
        x, y, c = _place()
        sibling = (x, y, 1 - c)
        my_chip = 2 * x + y
        chips = _other_chips(x, y)

        def ici(t, j, chip_of_data, to):
            return pltpu.make_async_remote_copy(
                src_ref=_half(ins[t].at[my_chip], c), dst_ref=_half(outs[t].at[chip_of_data], c),
                send_sem=send_sems.at[t, j], recv_sem=recv_sems.at[t, j], device_id=to, device_id_type=MESH)

        def d2d(t, j, chip_of_data, which):
            blk = _half(outs[t].at[chip_of_data], which)
            return pltpu.make_async_remote_copy(
                src_ref=blk, dst_ref=blk, send_sem=fsend_sems.at[t, j], recv_sem=frecv_sems.at[t, j],
                device_id=sibling, device_id_type=MESH)

        sends = [ici(t, j, my_chip, (*chip, c)) for j, chip in enumerate(chips) for t in range(T)]
        for cp in sends:
            cp.start()
        passed = []
        for j, (cx, cy) in enumerate(chips):
            for t in range(T):
                ici(t, j, 2 * cx + cy, (x, y, c)).wait_recv()
                fwd = d2d(t, j, 2 * cx + cy, c)
                fwd.start()
                passed.append(fwd)
        for j, (cx, cy) in enumerate(chips):
            for t in range(T):
                d2d(t, j, 2 * cx + cy, 1 - c).wait_recv()
        for cp in sends + passed:
            cp.wait_send()

    sem = lambda: pltpu.SemaphoreType.DMA((T, 3))
    return pl.pallas_call(
        body, name=name, out_shape=[_sds(b.shape, b.dtype) for b in bufs],
        in_specs=[_ANY] * T, out_specs=[_ANY] * T, input_output_aliases={t: t for t in range(T)},
        scratch_shapes=[sem(), sem(), sem(), sem()],
        compiler_params=_params(),
    )(*bufs)


def exchange_halves(full, name):
    T = len(full)

    def body(*refs):
        ins, outs = refs[:T], refs[T:2 * T]
        send_sems, recv_sems = refs[2 * T:]
        x, y, c = _place()
        cps = [pltpu.make_async_remote_copy(
            src_ref=_half(ins[t], 1 - c), dst_ref=outs[t], send_sem=send_sems.at[t], recv_sem=recv_sems.at[t],
            device_id=(x, y, 1 - c), device_id_type=MESH) for t in range(T)]
        for cp in cps:
            cp.start()
        for cp in cps:
            cp.wait()

    return pl.pallas_call(
        body, name=name,
        out_shape=[_sds((4, f.shape[1] // 2, f.shape[2]), f.dtype) for f in full],
        in_specs=[_ANY] * T, out_specs=[_ANY] * T,
        scratch_shapes=[pltpu.SemaphoreType.DMA((T,)), pltpu.SemaphoreType.DMA((T,))],
        compiler_params=_params(),
    )(*full)


def scatter_chip_sums(sums, name):
    T = len(sums)

    def body(*refs):
        ins, outs = refs[:T], refs[T:2 * T]
        send_sems, recv_sems = refs[2 * T:]
        x, y, c = _place()
        cps = []
        for j, (cx, cy) in enumerate(_other_chips(x, y)):
            for t in range(T):
                cps.append(pltpu.make_async_remote_copy(
                    src_ref=ins[t].at[2 * cx + cy], dst_ref=outs[t].at[j], send_sem=send_sems.at[t, j],
                    recv_sem=recv_sems.at[t, j], device_id=(cx, cy, c), device_id_type=MESH))
        for cp in cps:
            cp.start()
        for cp in cps:
            cp.wait()

    return pl.pallas_call(
        body, name=name, out_shape=[_sds((3,) + s.shape[1:], s.dtype) for s in sums],
        in_specs=[_ANY] * T, out_specs=[_ANY] * T,
        scratch_shapes=[pltpu.SemaphoreType.DMA((T, 3)), pltpu.SemaphoreType.DMA((T, 3))],
        compiler_params=_params(),
    )(*sums)


def join_halves(bufs, name):
    G = len(bufs)
    layers = [(g, l) for g in range(G) for l in range(bufs[g].shape[0])]
    T = len(layers)

    def body(*refs):
        ins, outs = refs[:G], refs[G:2 * G]
        send_sems, recv_sems = refs[2 * G:]
        x, y, c = _place()
        cps = [pltpu.make_async_remote_copy(
            src_ref=_half(ins[g].at[l], c), dst_ref=_half(outs[g].at[l], c), send_sem=send_sems.at[t],
            recv_sem=recv_sems.at[t], device_id=(x, y, 1 - c), device_id_type=MESH) for t, (g, l) in enumerate(layers)]
        for cp in cps:
            cp.start()
        for cp in cps:
            cp.wait()

    return pl.pallas_call(
        body, name=name, out_shape=[_sds(b.shape, b.dtype) for b in bufs],
        in_specs=[_ANY] * G, out_specs=[_ANY] * G, input_output_aliases={g: g for g in range(G)},
        scratch_shapes=[pltpu.SemaphoreType.DMA((T,)), pltpu.SemaphoreType.DMA((T,))],
        compiler_params=_params(),
    )(*bufs)


def _my_chip():
    return 2 * lax.axis_index("x") + lax.axis_index("y")


def add_halves(full, got, name):
    nb, R, C = full.shape
    rh = R // 2
    tr = _fit(512, [rh])
    per = rh // tr

    def body(a_ref, b_ref, o_ref):
        o_ref[...] = (a_ref[...].astype(F32) + b_ref[...].astype(F32)).astype(o_ref.dtype)

    return pl.pallas_call(
        body, name=name, grid=(nb, per),
        in_specs=[pl.BlockSpec((None, tr, C), lambda b, i: (b, lax.axis_index("c") * per + i, 0)),
                  pl.BlockSpec((None, tr, C), lambda b, i: (b, i, 0))],
        out_specs=pl.BlockSpec((None, tr, C), lambda b, i: (b, i, 0)),
        out_shape=_sds((nb, rh, C), BF16),
        compiler_params=_params(("parallel", "parallel")),
    )(full, got)


def finish_sum(sums, got, stacked, n_layers, l, name):
    nb, rh, C = sums.shape
    tr = _fit(512, [rh])
    per = rh // tr

    def body(s_ref, g_ref, *rest):
        o_ref = rest[-1]
        acc = s_ref[...].astype(F32)
        for j in range(3):
            acc = acc + g_ref[j].astype(F32)
        o_ref[...] = acc

    in_specs = [pl.BlockSpec((None, tr, C), lambda i: (_my_chip(), i, 0)),
                pl.BlockSpec((3, tr, C), lambda i: (0, i, 0))]
    args = [sums, got]
    aliases = {}
    if stacked is not None:
        in_specs.append(_ANY)
        args.append(stacked)
        aliases = {2: 0}
    return pl.pallas_call(
        body, name=name, grid=(per,), in_specs=in_specs,
        out_specs=pl.BlockSpec((None, tr, C), lambda i: (l, lax.axis_index("c") * per + i, 0)),
        out_shape=_sds((n_layers, 2 * rh, C), F32), input_output_aliases=aliases,
        compiler_params=_params(("arbitrary",)),
    )(*args)


def sum_devices(gathered, name):
    n_dev, M, N = gathered.shape

    def body(g_ref, o_ref):
        acc = g_ref[0]
        for d in range(1, n_dev):
            acc = acc + g_ref[d]
        o_ref[...] = acc

    tr = 8
    return pl.pallas_call(
        body, name=name, grid=(M // tr,),
        in_specs=[pl.BlockSpec((n_dev, tr, N), lambda i: (0, i, 0))], out_specs=pl.BlockSpec((tr, N), lambda i: (i, 0)),
        out_shape=_sds((M, N), F32), compiler_params=_params(("parallel",)),
    )(gathered)


def _pack(arrays):
    flat = jnp.concatenate([a.reshape(-1).astype(F32) for a in arrays])
    unit = 8 * PACK_COLS
    pad = (-flat.shape[0]) % unit
    return jnp.pad(flat, (0, pad)).reshape(-1, PACK_COLS)


def _unpack(packed, shapes):
    flat = packed.reshape(packed.shape[:-2] + (-1,))
    out, off = [], 0
    for s in shapes:
        n = math.prod(s)
        out.append(flat[..., off:off + n].reshape(packed.shape[:-2] + tuple(s)))
        off += n
    return out


def to_bf16(w, l, name):
    _, R, C = w.shape
    tr = _fit(512, [R])

    def body(w_ref, o_ref):
        o_ref[...] = w_ref[...].astype(BF16)

    return pl.pallas_call(
        body, name=name, grid=(R // tr,),
        in_specs=[pl.BlockSpec((None, tr, C), lambda i: (l, i, 0))],
        out_specs=pl.BlockSpec((None, tr, C), lambda i: (_my_chip(), i, 0)),
        out_shape=_sds((4, R, C), BF16),
        compiler_params=_params(("parallel",)),
    )(w)


def kernel(x, c, ada_w, ada_b, norm_g, mlp_w1, mlp_w2, gmlp_w_in, gmlp_ln_g, gmlp_ln_b, gmlp_ws, gmlp_bs, gmlp_w_out, kv_norm_g, kv_ada_w, kv_ada_b, w_kv, k_norm_g, w_f, b_f, attn_wq, q_norm_g, attn_wo, loss_target, m_ada_w, m_ada_b, m_norm_g, m_mlp_w1, m_mlp_w2, m_gmlp_w_in, m_gmlp_ln_g, m_gmlp_ln_b, m_gmlp_ws, m_gmlp_bs, m_gmlp_w_out, m_kv_norm_g, m_kv_ada_w, m_kv_ada_b, m_w_kv, m_k_norm_g, m_w_f, m_b_f, m_attn_wq, m_q_norm_g, m_attn_wo, v_ada_w, v_ada_b, v_norm_g, v_mlp_w1, v_mlp_w2, v_gmlp_w_in, v_gmlp_ln_g, v_gmlp_ln_b, v_gmlp_ws, v_gmlp_bs, v_gmlp_w_out, v_kv_norm_g, v_kv_ada_w, v_kv_ada_b, v_w_kv, v_k_norm_g, v_w_f, v_b_f, v_attn_wq, v_q_norm_g, v_attn_wo):
    given = dict(locals())
    S, D = x.shape[1], x.shape[2]
    depth = ada_w.shape[0]
    n_a = gmlp_w_in.shape[0]
    H = D // HEAD_DIM
    G = gmlp_ws.shape[1]
    half = gmlp_w_out.shape[1] * 4
    n_dev = 8
    tq = min(512, S // 4)
    ax, ay, ac = _place()
    chip = 2 * ax + ay
    me = 4 * ax + 2 * ay + ac
    row = lambda v: v.reshape(1, -1)
    x0 = x[0]
    tgt = loss_target[0]

    small_in = [c, w_f, norm_g, gmlp_ln_g, gmlp_ln_b]
    g1 = allgather_small(_pack(small_in), "gather_small_params").reshape(n_dev, -1, PACK_COLS)
    c_all, wf_all, ng_all, lg_all, lb_all = _unpack(g1, [a.shape for a in small_in])
    c_all = c_all[:, 0, :]
    per_chip = lambda a: [a[2 * j] for j in range(4)]
    w_f_full = jnp.concatenate(per_chip(wf_all), axis=0)
    norm_g_full = jnp.concatenate(per_chip(ng_all), axis=-1)
    ln_g_full = jnp.concatenate(per_chip(lg_all), axis=-1)
    ln_b_full = jnp.concatenate(per_chip(lb_all), axis=-1)
    w_f_pad = jnp.pad(w_f_full, ((0, 0), (0, LANES - H))).astype(BF16)
    b_f_pad = jnp.pad(b_f, (0, LANES - H)).reshape(1, LANES)

    sc_all = rowmap(lambda cb: cb * jax.nn.sigmoid(cb), [c_all], [], [(D, F32)], tr=8, name="silu")[0]
    n_loc = ada_w.shape[2]
    ada_b_loc = lax.dynamic_slice_in_dim(ada_b, chip * n_loc, n_loc, axis=1).reshape(1, -1)
    add_bias = lambda acc, b: (acc + b,)
    mod_loc = mm(sc_all, ada_w, outs=[(F32, None)], epilogue=add_bias, extras=[ada_b_loc], name="ada_mod")[0]
    kv_loc_n = kv_ada_w.shape[1]
    kv_b_loc = lax.dynamic_slice_in_dim(kv_ada_b, chip * kv_loc_n, kv_loc_n).reshape(1, -1)
    kvmod_loc = mm(sc_all, kv_ada_w, outs=[(F32, None)], epilogue=add_bias, extras=[kv_b_loc], name="kv_ada_mod")[0]
    g2 = allgather_small(_pack([mod_loc, kvmod_loc]), "gather_mod").reshape(n_dev, -1, PACK_COLS)
    mod_all, kvmod_all = _unpack(g2, [mod_loc.shape, kvmod_loc.shape])
    mod_me = jnp.concatenate(
        [lax.dynamic_index_in_dim(m, me, 0, keepdims=False).reshape(depth, n_loc) for m in per_chip(mod_all)], axis=1)
    kvmod_me = jnp.concatenate([lax.dynamic_index_in_dim(m, me, 0, keepdims=False) for m in per_chip(kvmod_all)])
    mods = [[row(v) for v in jnp.split(mod_me[l], N_MOD)] for l in range(depth)]
    kv_shift, kv_scale = [row(v) for v in jnp.split(kvmod_me, 2)]

    big = ['mlp_w1', 'mlp_w2', 'gmlp_w_in', 'gmlp_w_out', 'w_kv', 'attn_wq', 'attn_wo']
    stacked = {n: (given[n] if given[n].ndim == 3 else given[n][None]) for n in big}
    owner = [(n, l) for n in big for l in range(stacked[n].shape[0])]
    gathered = allgather_weights([to_bf16(stacked[n], l, f"cast_{n}_{l}") for n, l in owner], "gather_weights")
    W = dict(zip(owner, gathered))
    rows_of = lambda a: a.reshape(-1, a.shape[-1])
    W1 = [W['mlp_w1', l] for l in range(depth)]
    W2 = [rows_of(W['mlp_w2', l]) for l in range(depth)]
    Win = [W['gmlp_w_in', a] for a in range(n_a)]
    Wout = [rows_of(W['gmlp_w_out', a]) for a in range(n_a)]
    Wkv = W['w_kv', 0]
    Wq = [rows_of(W['attn_wq', b]) for b in range(depth - n_a)]
    Wo = [rows_of(W['attn_wo', b]) for b in range(depth - n_a)]
    bsb = jnp.broadcast_to(gmlp_bs[..., None], gmlp_bs.shape + (LANES,))

    def resid(acc, xr, gate):
        return xr + gate * acc, acc

    saved = []
    xs = x0
    kv = None
    for l in range(depth):
        sh1, sc1, gt1, sh2, sc2, gt2 = mods[l]
        ng0, ng1 = row(norm_g_full[l, 0]), row(norm_g_full[l, 1])
        st = {"x": xs}
        h1 = modnorm_fwd(xs, ng0, sc1, sh1, f"norm1_{l}")
        st["h1"] = h1
        if l < n_a:
            t, z = mm(h1, Win[l], outs=[(BF16, None), (BF16, None)], epilogue=lambda acc: (acc, _gelu(acc)),
                      name=f"gmlp_in_{l}")
            vn = gmlp_ln_fwd(z, row(ln_g_full[l]), row(ln_b_full[l]), f"gmlp_ln_{l}")
            p = gmlp_mix_fwd(z, vn, gmlp_ws[l], bsb[l], f"gmlp_mix_{l}")
            x1, y = mm(p, Wout[l], outs=[(F32, None), (BF16, None)], epilogue=resid, extras=[xs, gt1],
                       name=f"gmlp_out_{l}")
            st.update(t=t, z=z, vn=vn, p=p, y=y)
        else:
            if kv is None:
                hk = modnorm_fwd(xs, row(kv_norm_g), kv_scale, kv_shift, "kv_norm")
                kg = row(k_norm_g)
                kp, kk = mm(hk, Wkv[:2], outs=[(BF16, None), (BF16, None)], extras=[kg],
                            epilogue=lambda acc, g: (acc, _head_norm(acc, g)), name="kv_k")
                vv = mm(hk, Wkv[2:], outs=[(BF16, None)], name="kv_v")[0]
                zf, logf = mm(hk, w_f_pad, outs=[(F32, None), (F32, None)], extras=[b_f_pad],
                              epilogue=lambda acc, b: (acc + b, jax.nn.log_sigmoid(acc + b)), name="kv_f")
                fc, fk = fcum_fwd(logf, tq, "fcum")
                kv = dict(x=xs, hk=hk, kp=kp, k=kk, v=vv, zf=zf, fc=fc, fk=fk)
            b = l - n_a
            qp, q = mm(h1, Wq[b], outs=[(BF16, None), (BF16, None)], extras=[row(q_norm_g[b])],
                       epilogue=lambda acc, g: (acc, _head_norm(acc, g) * (1.0 / math.sqrt(HEAD_DIM))),
                       name=f"attn_q_{l}")
            o, lse = fox_fwd(q, kv["k"], kv["v"], kv["fk"], tq, f"fox_fwd_{l}")
            x1, y = mm(o, Wo[b], outs=[(F32, None), (BF16, None)], epilogue=resid, extras=[xs, gt1],
                       name=f"attn_o_{l}")
            st.update(qp=qp, q=q, o=o, lse=lse, y=y)
        st["x1"] = x1
        h2 = modnorm_fwd(x1, ng1, sc2, sh2, f"norm2_{l}")
        a_pre, a_sq = mm(h2, W1[l], outs=[(BF16, None), (BF16, None)],
                         epilogue=lambda acc: (acc, jnp.square(jnp.maximum(acc, 0.0))), name=f"mlp_up_{l}")
        xs, y2 = mm(a_sq, W2[l], outs=[(F32, None), (BF16, None)], epilogue=resid, extras=[x1, gt2],
                    name=f"mlp_down_{l}")
        st.update(h2=h2, a=a_pre, a_sq=a_sq, y2=y2)
        saved.append(st)

    def loss_fn(yb, tb_):
        e = yb - tb_
        return e * (1.0 / D), _colsum(e * e)
    dx, sq = rowmap(loss_fn, [xs, tgt], [], [(D, F32)], reds=[(1, D)], tr=512, name="loss")
    loss = lax.psum(0.5 * jnp.sum(sq) / D, ("x", "y", "c"))

    big_grads = {n: [None] * stacked[n].shape[0] for n in big}
    dmod = [None] * depth
    d_norm_g = [[None, None] for _ in range(depth)]
    small = {}
    d_ws, d_bs, d_lg, d_lb, d_qg = [None] * n_a, [None] * n_a, [None] * n_a, [None] * n_a, [None] * (depth - n_a)
    dk_parts, dv_parts, dfk_parts = [], [], []
    quarter = lambda g: g.reshape(4, g.shape[0] // 4, g.shape[1])
    for l in reversed(range(depth)):
        sh1, sc1, gt1, sh2, sc2, gt2 = mods[l]
        ng0, ng1 = row(norm_g_full[l, 0]), row(norm_g_full[l, 1])
        st = saved[l]
        dy2, dgt2 = gate_bwd(dx, st["y2"], gt2, f"gate2_bwd_{l}")
        big_grads['mlp_w2'][l] = quarter(mm(st["a_sq"], dy2, ta=True, outs=[(BF16, None)], name=f"mlp_w2_grad_{l}")[0])
        da = mm(dy2, W2[l], tb=True, outs=[(BF16, None)], extras=[st["a"]],
                epilogue=lambda acc, a: (acc * (2.0 * jnp.maximum(a.astype(F32), 0.0)),), name=f"mlp_down_bwd_{l}")[0]
        big_grads['mlp_w1'][l] = mm(st["h2"], da, ta=True, outs=[(BF16, 4)], name=f"mlp_w1_grad_{l}")[0]
        dh2 = mm(da, W1[l], tb=True, outs=[(BF16, None)], name=f"mlp_up_bwd_{l}")[0]
        dx, dsh2, d_norm_g[l][1], dsc2 = modnorm_bwd(st["x1"], dh2, dx, ng1, sc2, f"norm2_bwd_{l}")
        dy, dgt1 = gate_bwd(dx, st["y"], gt1, f"gate1_bwd_{l}")
        if l < n_a:
            big_grads['gmlp_w_out'][l] = quarter(
                mm(st["p"], dy, ta=True, outs=[(BF16, None)], name=f"gmlp_w_out_grad_{l}")[0])
            dp = mm(dy, Wout[l], tb=True, outs=[(BF16, None)], name=f"gmlp_out_bwd_{l}")[0]
            du, dvn, d_ws[l], db = gmlp_mix_bwd(dp, st["z"], st["vn"], gmlp_ws[l], bsb[l], f"gmlp_mix_bwd_{l}")
            d_bs[l] = db[:, :, 0]
            dt, d_lg[l], d_lb[l] = gmlp_act_bwd(st["t"], du, dvn, row(ln_g_full[l]), f"gmlp_act_bwd_{l}")
            big_grads['gmlp_w_in'][l] = mm(st["h1"], dt, ta=True, outs=[(BF16, 4)], name=f"gmlp_w_in_grad_{l}")[0]
            dh1 = mm(dt, Win[l], tb=True, outs=[(BF16, None)], name=f"gmlp_in_bwd_{l}")[0]
        else:
            b = l - n_a
            big_grads['attn_wo'][b] = quarter(
                mm(st["o"], dy, ta=True, outs=[(BF16, None)], name=f"attn_wo_grad_{l}")[0])
            do = mm(dy, Wo[b], tb=True, outs=[(BF16, None)], name=f"attn_o_bwd_{l}")[0]
            dq, rowterm = fox_dq(st["q"], kv["k"], kv["v"], do, st["lse"], kv["fk"], tq, f"fox_dq_{l}")
            dk_l, dv_l, dfk_l = fox_dkv(st["q"], kv["k"], kv["v"], do, rowterm, st["lse"], kv["fk"], tq, f"fox_dkv_{l}")
            dk_parts.append(dk_l)
            dv_parts.append(dv_l)
            dfk_parts.append(jnp.pad(dfk_l[:, :, 0, :].reshape(H, S), ((0, LANES - H), (0, 0))))
            dqp, d_qg[b] = head_norm_bwd(st["qp"], dq, row(q_norm_g[b]), f"q_norm_bwd_{l}")
            big_grads['attn_wq'][b] = quarter(
                mm(st["h1"], dqp, ta=True, outs=[(BF16, None)], name=f"attn_wq_grad_{l}")[0])
            dh1 = mm(dqp, Wq[b], tb=True, outs=[(BF16, None)], name=f"attn_q_bwd_{l}")[0]
        dx, dsh1, d_norm_g[l][0], dsc1 = modnorm_bwd(st["x"], dh1, dx, ng0, sc1, f"norm1_bwd_{l}")
        dmod[l] = jnp.concatenate([dsh1, dsc1, dgt1, dsh2, dsc2, dgt2], axis=1)
        if l == n_a:
            add2 = lambda a, b_: a.astype(F32) + b_.astype(F32)
            dk_sum = rowmap(add2, dk_parts, [], [(D, BF16)], tr=512, name="dk_sum")[0]
            dv_sum = rowmap(add2, dv_parts, [], [(D, BF16)], tr=512, name="dv_sum")[0]
            dkvp, small['k_norm_g'] = head_norm_bwd(kv["kp"], dk_sum, row(k_norm_g), "k_norm_bwd", extra=dv_sum)
            dzf, db_f = fcum_bwd(dfk_parts[0], dfk_parts[1], kv["zf"], H, "fcum_bwd")
            small['b_f'] = db_f[0, :H]
            g_wkv = mm(kv["hk"], dkvp, ta=True, outs=[(BF16, 4)], name="w_kv_grad")[0]
            small['w_f'] = mm(kv["hk"], dzf, ta=True, outs=[(F32, None)], name="w_f_grad")[0][:, :H]
            dhk_f = mm(dzf, w_f_pad, tb=True, outs=[(BF16, None)], name="kv_f_bwd")[0]
            dhk = mm(dkvp, Wkv, tb=True, outs=[(BF16, None)], extras=[dhk_f],
                     epilogue=lambda acc, e: (acc + e.astype(F32),), name="kv_bwd")[0]
            dx, dkv_shift, small['kv_norm_g'], dkv_scale = modnorm_bwd(
                kv["x"], dhk, dx, row(kv_norm_g), kv_scale, "kv_norm_bwd")
            dkvmod = jnp.concatenate([dkv_shift, dkv_scale], axis=1)
    grad_x = dx[None]

    small['norm_g'] = jnp.stack([jnp.concatenate(p, axis=0) for p in d_norm_g])
    small['gmlp_ln_g'] = jnp.concatenate(d_lg, axis=0)
    small['gmlp_ln_b'] = jnp.concatenate(d_lb, axis=0)
    small['gmlp_ws'] = jnp.stack(d_ws)
    small['gmlp_bs'] = jnp.stack(d_bs)
    small['q_norm_g'] = jnp.concatenate(d_qg, axis=0)
    small['ada_b'] = jnp.concatenate(dmod, axis=0)
    small['kv_ada_b'] = dkvmod
    names = sorted(small)
    shapes = [small[n].shape for n in names]
    g3 = allgather_small(_pack([small[n] for n in names]), "gather_small_grads").reshape(n_dev, -1, PACK_COLS)
    summed = dict(zip(names, _unpack(sum_devices(g3, "sum_small_grads"), shapes)))
    each = dict(zip(names, _unpack(g3, shapes)))
    local_cols = lambda a, n: lax.dynamic_slice_in_dim(a, chip * n, n, axis=a.ndim - 1)
    grads = {
        'ada_b': summed['ada_b'], 'gmlp_ws': summed['gmlp_ws'], 'gmlp_bs': summed['gmlp_bs'],
        'kv_norm_g': summed['kv_norm_g'].reshape(-1), 'kv_ada_b': summed['kv_ada_b'].reshape(-1),
        'k_norm_g': summed['k_norm_g'].reshape(-1), 'b_f': summed['b_f'], 'q_norm_g': summed['q_norm_g'],
        'norm_g': local_cols(summed['norm_g'], norm_g.shape[2]),
        'gmlp_ln_g': local_cols(summed['gmlp_ln_g'], gmlp_ln_g.shape[1]),
        'gmlp_ln_b': local_cols(summed['gmlp_ln_b'], gmlp_ln_b.shape[1]),
        'w_f': lax.dynamic_slice_in_dim(summed['w_f'], chip * w_f.shape[0], w_f.shape[0], axis=0),
    }
    dmod_all = each['ada_b'].reshape(n_dev, depth, N_MOD * D)
    dm_loc = jnp.transpose(local_cols(dmod_all, n_loc), (1, 0, 2))
    dkv_loc = local_cols(each['kv_ada_b'].reshape(n_dev, 2 * D), kv_loc_n)[None]

    big_grads['w_kv'] = [g_wkv]
    flat = [big_grads[n][l] for n, l in owner]
    got = exchange_halves(flat, "grad_exchange_halves")
    sums = [add_halves(f, g, f"grad_chip_sum_{n}_{l}") for f, g, (n, l) in zip(flat, got, owner)]
    landed = scatter_chip_sums(sums, "grad_scatter")
    stacks = {n: None for n in big}
    for s, g, (n, l) in zip(sums, landed, owner):
        stacks[n] = finish_sum(s, g, stacks[n], stacked[n].shape[0], l, f"grad_finish_{n}_{l}")
    joined = join_halves([stacks[n] for n in big], "grad_join")
    for n, g in zip(big, joined):
        grads[n] = g if n != 'w_kv' else g[0]

    delta, new_m, new_v = {}, {}, {}
    sct = jnp.transpose(sc_all)
    grads['ada_w'], delta['ada_w'], new_m['ada_w'], new_v['ada_w'] = adamw_outer(
        ada_w, sct, dm_loc, m_ada_w, v_ada_w, "adamw_ada_w")
    r = adamw_outer(kv_ada_w[None], sct, dkv_loc, m_kv_ada_w[None], v_kv_ada_w[None], "adamw_kv_ada_w")
    grads['kv_ada_w'], delta['kv_ada_w'], new_m['kv_ada_w'], new_v['kv_ada_w'] = [a[0] for a in r]
    for n in WEIGHTS:
        if n in delta:
            continue
        grads[n] = grads[n].reshape(given[n].shape)
        delta[n], new_m[n], new_v[n] = adamw(given[n], grads[n], given["m_" + n], given["v_" + n], "adamw_" + n)
    return (loss, grad_x, *[grads[n] for n in WEIGHTS], *[delta[n] for n in WEIGHTS],
            *[new_m[n] for n in WEIGHTS], *[new_v[n] for n in WEIGHTS])
```

```python
import functools
import math

import jax
import jax.numpy as jnp
from jax import lax
from jax.experimental import pallas as pl
from jax.experimental.pallas import tpu as pltpu

F32 = jnp.float32
BF16 = jnp.bfloat16
EPS = 1e-6
HEAD_DIM = 128
GMLP_BLOCK = 128
CHUNK = 64
LANES = 128
N_MOD = 6
V7X_VMEM_BYTES = 64 * 2**20
VMEM_LIMIT = V7X_VMEM_BYTES - 8 * 2**20
PACK_COLS = 1024
NEG = -1e30
MESH = pl.DeviceIdType.MESH

ADAM_LR = 0.001
ADAM_B1 = 0.9
ADAM_B2 = 0.999
ADAM_EPS = 1e-08
ADAM_WD = 0.01
ADAM_STEP = 10

WEIGHTS = ['ada_w', 'ada_b', 'norm_g', 'mlp_w1', 'mlp_w2', 'gmlp_w_in', 'gmlp_ln_g', 'gmlp_ln_b', 'gmlp_ws',
           'gmlp_bs', 'gmlp_w_out', 'kv_norm_g', 'kv_ada_w', 'kv_ada_b', 'w_kv', 'k_norm_g', 'w_f', 'b_f',
           'attn_wq', 'q_norm_g', 'attn_wo']


def _params(sem=None):
    return pltpu.CompilerParams(dimension_semantics=sem, vmem_limit_bytes=VMEM_LIMIT)


def _sds(shape, dtype):
    return jax.ShapeDtypeStruct(tuple(shape), dtype)


def _ldims(shape):
    return (shape[0], shape[1]) if len(shape) == 2 else (shape[1], shape[0] * shape[2])


def _fit(t, ns):
    n0 = min(ns)
    if n0 <= t and all(n % n0 == 0 for n in ns):
        return n0
    d = (t // LANES) * LANES
    while d > LANES and any(n % d for n in ns):
        d -= LANES
    assert all(n % d == 0 for n in ns), (t, ns)
    return d


def _blk(shape, br, bc):
    if len(shape) == 2:
        return (br, bc), (lambda r, c: (r, c))
    per = shape[2] // bc
    assert shape[2] % bc == 0, (shape, bc)
    return (None, br, bc), (lambda r, c: (c // per, r, c % per))


def mm(a, b, *, name, ta=False, tb=False, outs, epilogue=None, extras=(), tm=1024, tn=1024, tk=2048,
       precision=None):
    ar, ac = _ldims(a.shape)
    br, bc = _ldims(b.shape)
    M, K = (ac, ar) if ta else (ar, ac)
    K2, N = (bc, br) if tb else (br, bc)
    assert K == K2, (name, a.shape, b.shape)
    cons = {"m": [M], "n": [N], "k": [K]}

    def note(shape, dim):
        if len(shape) == 3:
            cons[dim].append(shape[2])

    note(a.shape, "m" if ta else "k")
    note(b.shape, "k" if tb else "n")
    out_shapes = []
    for dt, nb in outs:
        if nb is None:
            out_shapes.append(_sds((M, N), dt))
        else:
            out_shapes.append(_sds((nb, M, N // nb), dt))
            cons["n"].append(N // nb)
    for e in extras:
        note(e.shape, "n")
    tm, tn, tk = _fit(tm, cons["m"]), _fit(tn, cons["n"]), _fit(tk, cons["k"])
    gm, gn, gk = M // tm, N // tn, K // tk

    a_bs, a_ix = _blk(a.shape, tk if ta else tm, tm if ta else tk)
    b_bs, b_ix = _blk(b.shape, tn if tb else tk, tk if tb else tn)
    in_specs = [
        pl.BlockSpec(a_bs, (lambda i, j, k: a_ix(k, i)) if ta else (lambda i, j, k: a_ix(i, k))),
        pl.BlockSpec(b_bs, (lambda i, j, k: b_ix(j, k)) if tb else (lambda i, j, k: b_ix(k, j))),
    ]
    for e in extras:
        if _ldims(e.shape)[0] == 1 and M != 1:
            if e.shape[1] == N:
                in_specs.append(pl.BlockSpec((1, tn), lambda i, j, k: (0, j)))
            else:
                in_specs.append(pl.BlockSpec(e.shape, lambda i, j, k: (0, 0)))
        else:
            e_bs, e_ix = _blk(e.shape, tm, tn)
            in_specs.append(pl.BlockSpec(e_bs, functools.partial(lambda i, j, k, ix: ix(i, j), ix=e_ix)))
    out_specs = []
    for s in out_shapes:
        o_bs, o_ix = _blk(s.shape, tm, tn)
        out_specs.append(pl.BlockSpec(o_bs, functools.partial(lambda i, j, k, ix: ix(i, j), ix=o_ix)))
    n_e, n_o = len(extras), len(outs)
    dims = (((0 if ta else 1,), (1 if tb else 0,)), ((), ()))

    def body(*refs):
        a_ref, b_ref = refs[:2]
        e_refs = refs[2:2 + n_e]
        o_refs = refs[2 + n_e:2 + n_e + n_o]
        x, w = a_ref[...], b_ref[...]
        if precision is None:
            x, w = x.astype(BF16), w.astype(BF16)
        d = lax.dot_general(x, w, dims, preferred_element_type=F32, precision=precision)

        def finish(acc):
            res = epilogue(acc, *[e[...] for e in e_refs]) if epilogue is not None else (acc,)
            for o, r in zip(o_refs, res):
                o[...] = r.astype(o.dtype)

        if gk == 1:
            finish(d)
        else:
            acc_ref = refs[-1]
            k = pl.program_id(2)

            @pl.when(k == 0)
            def _():
                acc_ref[...] = d

            @pl.when(jnp.logical_and(k > 0, k < gk - 1))
            def _():
                acc_ref[...] += d

            @pl.when(k == gk - 1)
            def _():
                finish(acc_ref[...] + d)

    return pl.pallas_call(
        body, name=name, grid=(gm, gn, gk), in_specs=in_specs, out_specs=out_specs, out_shape=out_shapes,
        scratch_shapes=[pltpu.VMEM((tm, tn), F32)] if gk > 1 else [],
        compiler_params=_params(("parallel", "parallel", "arbitrary")),
    )(a, b, *extras)


def rowmap(fn, rows, vecs, outs, reds=(), *, tr, name):
    rows = [r if isinstance(r, tuple) else (r, r.shape[1], 0) for r in rows]
    S = rows[0][0].shape[0]
    tr = min(tr, S)
    assert S % tr == 0, (name, S, tr)
    n_i, n_o = len(rows) + len(vecs), len(outs)
    in_specs = [pl.BlockSpec((tr, w), functools.partial(lambda i, c: (i, c), c=c)) for _, w, c in rows]
    in_specs += [pl.BlockSpec(v.shape, functools.partial(lambda i, n: (0,) * n, n=v.ndim)) for v in vecs]
    out_shape = [_sds((S, f), dt) for f, dt in outs] + [_sds(s, F32) for s in reds]
    out_specs = [pl.BlockSpec((tr, f), lambda i: (i, 0)) for f, _ in outs]
    out_specs += [pl.BlockSpec(s, functools.partial(lambda i, n: (0,) * n, n=len(s))) for s in reds]

    def body(*refs):
        res = fn(*[r[...] for r in refs[:n_i]])
        res = res if isinstance(res, tuple) else (res,)
        for o, r in zip(refs[n_i:n_i + n_o], res[:n_o]):
            o[...] = r.astype(o.dtype)
        if reds:
            d_refs = refs[n_i + n_o:]

            @pl.when(pl.program_id(0) == 0)
            def _():
                for d in d_refs:
                    d[...] = jnp.zeros(d.shape, F32)

            for d, r in zip(d_refs, res[n_o:]):
                d[...] += r

    return pl.pallas_call(
        body, name=name, grid=(S // tr,), in_specs=in_specs, out_specs=out_specs, out_shape=out_shape,
        compiler_params=_params(("arbitrary",) if reds else ("parallel",)),
    )(*[r[0] for r in rows], *vecs)


def _gelu(t):
    return 0.5 * t * (1.0 + lax.erf(t * (1.0 / math.sqrt(2.0))))


def _gelu_grad(t):
    cdf = 0.5 * (1.0 + lax.erf(t * (1.0 / math.sqrt(2.0))))
    return cdf + t * jnp.exp(-0.5 * t * t) * (1.0 / math.sqrt(2.0 * math.pi))


def _colsum(v):
    return jnp.sum(v, axis=0, keepdims=True)


def modnorm_fwd(x, g, scale, shift, name):
    def fn(xb, gb, sc, sh):
        rstd = lax.rsqrt(jnp.mean(xb * xb, axis=-1, keepdims=True) + EPS)
        return ((xb * rstd) * gb) * (1.0 + sc) + sh
    return rowmap(fn, [x], [g, scale, shift], [(x.shape[1], BF16)], tr=512, name=name)[0]


def modnorm_bwd(x, dh, dres, g, scale, name):
    D = x.shape[1]

    def fn(xb, dhb, drb, gb, sc):
        dhb = dhb.astype(F32)
        rstd = lax.rsqrt(jnp.mean(xb * xb, axis=-1, keepdims=True) + EPS)
        xhat = xb * rstd
        a = gb * (1.0 + sc)
        dxhat = dhb * a
        dx = rstd * (dxhat - xhat * jnp.mean(dxhat * xhat, axis=-1, keepdims=True))
        da = _colsum(dhb * xhat)
        return drb + dx, _colsum(dhb), da * (1.0 + sc), da * gb
    return rowmap(fn, [x, dh, dres], [g, scale], [(D, F32)], reds=[(1, D)] * 3, tr=256, name=name)


def gate_bwd(dx, y, gate, name):
    D = dx.shape[1]

    def fn(dxb, yb, gb):
        return gb * dxb, _colsum(dxb * yb.astype(F32))
    return rowmap(fn, [dx, y], [gate], [(D, BF16)], reds=[(1, D)], tr=512, name=name)


def _head_norm(x, g):
    parts = []
    for h in range(x.shape[1] // HEAD_DIM):
        xh = x[:, h * HEAD_DIM:(h + 1) * HEAD_DIM]
        rstd = lax.rsqrt(jnp.mean(xh * xh, axis=-1, keepdims=True) + EPS)
        parts.append(xh * rstd * g)
    return jnp.concatenate(parts, axis=1)


def head_norm_bwd(xp, dy, g, name, extra=None):
    D = xp.shape[1]

    def fn(*blocks):
        xb, dyb = blocks[0].astype(F32), blocks[1].astype(F32)
        gb = blocks[-1]
        parts, dg = [], jnp.zeros((1, HEAD_DIM), F32)
        for h in range(D // HEAD_DIM):
            xh = xb[:, h * HEAD_DIM:(h + 1) * HEAD_DIM]
            dyh = dyb[:, h * HEAD_DIM:(h + 1) * HEAD_DIM]
            rstd = lax.rsqrt(jnp.mean(xh * xh, axis=-1, keepdims=True) + EPS)
            xhat = xh * rstd
            dg = dg + _colsum(dyh * xhat)
            dxhat = dyh * gb
            parts.append(rstd * (dxhat - xhat * jnp.mean(dxhat * xhat, axis=-1, keepdims=True)))
        if extra is not None:
            parts.append(blocks[2].astype(F32))
        return jnp.concatenate(parts, axis=1), dg
    rows = [xp, dy] + ([extra] if extra is not None else [])
    width = D + (extra.shape[1] if extra is not None else 0)
    return rowmap(fn, rows, [g], [(width, BF16)], reds=[(1, HEAD_DIM)], tr=512, name=name)


def gmlp_ln_fwd(z, ln_g, ln_b, name):
    half = z.shape[1] // 2

    def fn(vb, gb, bb):
        vb = vb.astype(F32)
        mu = jnp.mean(vb, axis=-1, keepdims=True)
        var = jnp.mean(jnp.square(vb - mu), axis=-1, keepdims=True)
        return ((vb - mu) * lax.rsqrt(var + EPS)) * gb + bb
    return rowmap(fn, [(z, half, 1)], [ln_g, ln_b], [(half, BF16)], tr=256, name=name)[0]


def _mix_mask():
    r = lax.broadcasted_iota(jnp.int32, (GMLP_BLOCK, GMLP_BLOCK), 0) // CHUNK
    c = lax.broadcasted_iota(jnp.int32, (GMLP_BLOCK, GMLP_BLOCK), 1) // CHUNK
    return c <= r


def gmlp_mix_fwd(z, vn, ws, bsb, name):
    S, half = vn.shape
    G = ws.shape[0]
    gd = half // G
    tb = min(512, S // 2)

    def body(u_ref, v_ref, w_ref, b_ref, p_ref):
        w = jnp.where(_mix_mask(), w_ref[...], 0.0).astype(BF16)
        bcol = b_ref[:, 0:1]
        for r in range(tb // GMLP_BLOCK):
            rs = slice(r * GMLP_BLOCK, (r + 1) * GMLP_BLOCK)
            sv = jnp.dot(w, v_ref[rs, :], preferred_element_type=F32) + bcol
            p_ref[rs, :] = (u_ref[rs, :].astype(F32) * sv).astype(p_ref.dtype)

    return pl.pallas_call(
        body, name=name, grid=(S // tb, G),
        in_specs=[pl.BlockSpec((tb, gd), lambda n, g: (n, g)), pl.BlockSpec((tb, gd), lambda n, g: (n, g)),
                  pl.BlockSpec((None, GMLP_BLOCK, GMLP_BLOCK), lambda n, g: (g, 0, 0)),
                  pl.BlockSpec((None, GMLP_BLOCK, LANES), lambda n, g: (g, 0, 0))],
        out_specs=pl.BlockSpec((tb, gd), lambda n, g: (n, g)),
        out_shape=_sds((S, half), BF16),
        compiler_params=_params(("parallel", "parallel")),
    )(z, vn, ws, bsb)


def gmlp_mix_bwd(dp, z, vn, ws, bsb, name):
    S, half = vn.shape
    G = ws.shape[0]
    gd = half // G
    tb = min(512, S // 2)

    def body(dp_ref, u_ref, v_ref, w_ref, b_ref, du_ref, dv_ref, dw_ref, db_ref):
        n = pl.program_id(1)
        mask = _mix_mask()
        w = jnp.where(mask, w_ref[...], 0.0).astype(BF16)
        bcol = b_ref[:, 0:1]
        dw = jnp.zeros((GMLP_BLOCK, GMLP_BLOCK), F32)
        db = jnp.zeros((GMLP_BLOCK, 1), F32)
        for r in range(tb // GMLP_BLOCK):
            rs = slice(r * GMLP_BLOCK, (r + 1) * GMLP_BLOCK)
            vb = v_ref[rs, :]
            dpb = dp_ref[rs, :].astype(F32)
            sv = jnp.dot(w, vb, preferred_element_type=F32) + bcol
            du_ref[rs, :] = (dpb * sv).astype(du_ref.dtype)
            dsv = dpb * u_ref[rs, :].astype(F32)
            dsv16 = dsv.astype(BF16)
            dv_ref[rs, :] = lax.dot_general(w, dsv16, (((0,), (0,)), ((), ())),
                                            preferred_element_type=F32).astype(dv_ref.dtype)
            dw = dw + lax.dot_general(dsv16, vb, (((1,), (1,)), ((), ())), preferred_element_type=F32)
            db = db + jnp.sum(dsv, axis=1, keepdims=True)
        dw = jnp.where(mask, dw, 0.0)
        db = jnp.broadcast_to(db, (GMLP_BLOCK, LANES))

        @pl.when(n == 0)
        def _():
            dw_ref[...] = dw
            db_ref[...] = db

        @pl.when(n > 0)
        def _():
            dw_ref[...] += dw
            db_ref[...] += db

    blk = pl.BlockSpec((tb, gd), lambda g, n: (n, g))
    return pl.pallas_call(
        body, name=name, grid=(G, S // tb),
        in_specs=[blk, blk, blk, pl.BlockSpec((None, GMLP_BLOCK, GMLP_BLOCK), lambda g, n: (g, 0, 0)),
                  pl.BlockSpec((None, GMLP_BLOCK, LANES), lambda g, n: (g, 0, 0))],
        out_specs=[blk, blk, pl.BlockSpec((None, GMLP_BLOCK, GMLP_BLOCK), lambda g, n: (g, 0, 0)),
                   pl.BlockSpec((None, GMLP_BLOCK, LANES), lambda g, n: (g, 0, 0))],
        out_shape=[_sds((S, half), BF16), _sds((S, half), BF16), _sds((G, GMLP_BLOCK, GMLP_BLOCK), F32),
                   _sds((G, GMLP_BLOCK, LANES), F32)],
        compiler_params=_params(("parallel", "arbitrary")),
    )(dp, z, vn, ws, bsb)


def gmlp_act_bwd(t, du, dvn, ln_g, name):
    half = du.shape[1]

    def fn(tb_, dub, dvb, gb):
        tb_ = tb_.astype(F32)
        tu, tv = tb_[:, :half], tb_[:, half:]
        dtu = dub.astype(F32) * _gelu_grad(tu)
        v = _gelu(tv)
        mu = jnp.mean(v, axis=-1, keepdims=True)
        vc = v - mu
        rstd = lax.rsqrt(jnp.mean(vc * vc, axis=-1, keepdims=True) + EPS)
        vhat = vc * rstd
        dvb = dvb.astype(F32)
        dvhat = dvb * gb
        dv = rstd * (dvhat - jnp.mean(dvhat, axis=-1, keepdims=True)
                     - vhat * jnp.mean(dvhat * vhat, axis=-1, keepdims=True))
        dtv = dv * _gelu_grad(tv)
        return jnp.concatenate([dtu, dtv], axis=1), _colsum(dvb * vhat), _colsum(dvb)
    return rowmap(fn, [t, du, dvn], [ln_g], [(2 * half, BF16)], reds=[(1, half)] * 2, tr=128, name=name)


def fcum_fwd(logf, tb, name):
    S = logf.shape[0]
    nb = S // tb

    def body(x_ref, fc_ref, fk_ref):
        tri = (lax.broadcasted_iota(jnp.int32, (LANES, LANES), 0)
               >= lax.broadcasted_iota(jnp.int32, (LANES, LANES), 1)).astype(F32)

        def blk(b, carry):
            off = pl.multiple_of(b * LANES, LANES)
            cs = jnp.dot(tri, x_ref[pl.ds(off, LANES), :], preferred_element_type=F32,
                         precision=lax.Precision.HIGHEST) + carry
            fc_ref[pl.ds(off, LANES), :] = cs
            return cs[LANES - 1:LANES, :]

        lax.fori_loop(0, S // LANES, blk, jnp.zeros((1, LANES), F32))
        for b in range(nb):
            fk_ref[b] = fc_ref[b * tb:(b + 1) * tb, :].T

    return pl.pallas_call(
        body, name=name, out_shape=[_sds((S, LANES), F32), _sds((nb, LANES, tb), F32)],
        compiler_params=_params(),
    )(logf)


def fcum_bwd(dfk_a, dfk_b, zf, n_heads, name):
    S = zf.shape[0]

    def body(da_ref, db_ref, zf_ref, dz_ref, dsum_ref, d_ref):
        d_ref[...] = (da_ref[...] + db_ref[...]).T
        triu = (lax.broadcasted_iota(jnp.int32, (LANES, LANES), 0)
                <= lax.broadcasted_iota(jnp.int32, (LANES, LANES), 1)).astype(F32)
        nblk = S // LANES
        live = lax.broadcasted_iota(jnp.int32, (LANES, LANES), 1) < n_heads

        def blk(r, carry):
            carry_row, tot = carry
            off = pl.multiple_of((nblk - 1 - r) * LANES, LANES)
            d_blk = d_ref[pl.ds(off, LANES), :]
            cs = jnp.dot(triu, d_blk, preferred_element_type=F32, precision=lax.Precision.HIGHEST) + carry_row
            dz = jnp.where(live, cs * jax.nn.sigmoid(-zf_ref[pl.ds(off, LANES), :]), 0.0)
            dz_ref[pl.ds(off, LANES), :] = dz.astype(dz_ref.dtype)
            return carry_row + _colsum(d_blk), tot + _colsum(dz)

        _, tot = lax.fori_loop(0, nblk, blk, (jnp.zeros((1, LANES), F32), jnp.zeros((1, LANES), F32)))
        dsum_ref[...] = tot

    return pl.pallas_call(
        body, name=name, out_shape=[_sds((S, LANES), BF16), _sds((1, LANES), F32)],
        scratch_shapes=[pltpu.VMEM((S, LANES), F32)], compiler_params=_params(),
    )(dfk_a, dfk_b, zf)


def _causal(tq):
    return (lax.broadcasted_iota(jnp.int32, (tq, tq), 1) <= lax.broadcasted_iota(jnp.int32, (tq, tq), 0))


_NT = (((1,), (1,)), ((), ()))
_TN = (((0,), (0,)), ((), ()))


def fox_fwd(q, k, v, fk, tq, name):
    S, D = q.shape
    H, nq = D // HEAD_DIM, S // tq

    def body(q_ref, k_ref, v_ref, fk_ref, o_ref, lse_ref):
        h, i = pl.program_id(0), pl.program_id(1)
        qb = q_ref[...]
        hs = h % 8

        def step(j, carry, masked):
            m, l, acc = carry
            off = pl.multiple_of(j * tq, tq)
            kb, vb = k_ref[pl.ds(off, tq), :], v_ref[pl.ds(off, tq), :]
            s = lax.dot_general(qb, kb, _NT, preferred_element_type=F32) - fk_ref[j, pl.ds(hs, 1), :]
            if masked:
                s = jnp.where(_causal(tq), s, NEG)
            m_new = jnp.maximum(m, jnp.max(s, axis=1, keepdims=True))
            alpha = jnp.exp(m - m_new)
            p = jnp.exp(s - m_new)
            l = alpha * l + jnp.sum(p, axis=1, keepdims=True)
            acc = alpha * acc + jnp.dot(p.astype(BF16), vb, preferred_element_type=F32)
            return m_new, l, acc

        init = (jnp.full((tq, 1), NEG, F32), jnp.zeros((tq, 1), F32), jnp.zeros((tq, HEAD_DIM), F32))
        carry = lax.fori_loop(0, i, lambda j, c: step(j, c, False), init)
        m, l, acc = step(i, carry, True)
        o_ref[...] = (acc / l).astype(o_ref.dtype)
        lse_ref[...] = jnp.broadcast_to(m + jnp.log(l), (tq, LANES))

    return pl.pallas_call(
        body, name=name, grid=(H, nq),
        in_specs=[pl.BlockSpec((tq, HEAD_DIM), lambda h, i: (i, h)),
                  pl.BlockSpec((S, HEAD_DIM), lambda h, i: (0, h)),
                  pl.BlockSpec((S, HEAD_DIM), lambda h, i: (0, h)),
                  pl.BlockSpec((nq, 8, tq), lambda h, i: (0, h // 8, 0))],
        out_specs=[pl.BlockSpec((tq, HEAD_DIM), lambda h, i: (i, h)),
                   pl.BlockSpec((None, tq, LANES), lambda h, i: (h, i, 0))],
        out_shape=[_sds((S, D), BF16), _sds((H, S, LANES), F32)],
        compiler_params=_params(("parallel", "arbitrary")),
    )(q, k, v, fk)


def fox_dq(q, k, v, do, lse, fk, tq, name):
    S, D = q.shape
    H, nq = D // HEAD_DIM, S // tq
    scale = 1.0 / math.sqrt(HEAD_DIM)

    def body(q_ref, k_ref, v_ref, do_ref, lse_ref, fk_ref, dq_ref, row_ref):
        h, i = pl.program_id(0), pl.program_id(1)
        qb, dob = q_ref[...], do_ref[...]
        lsec = lse_ref[:, 0:1]
        hs = h % 8

        def p_dp(j, masked):
            off = pl.multiple_of(j * tq, tq)
            kb, vb = k_ref[pl.ds(off, tq), :], v_ref[pl.ds(off, tq), :]
            s = lax.dot_general(qb, kb, _NT, preferred_element_type=F32) - fk_ref[j, pl.ds(hs, 1), :]
            if masked:
                s = jnp.where(_causal(tq), s, NEG)
            return jnp.exp(s - lsec), lax.dot_general(dob, vb, _NT, preferred_element_type=F32), kb

        def sums(j, carry, masked):
            p, dp, _ = p_dp(j, masked)
            return carry[0] + jnp.sum(p * dp, axis=1, keepdims=True), carry[1] + jnp.sum(p, axis=1, keepdims=True)

        zero = jnp.zeros((tq, 1), F32)
        carry = lax.fori_loop(0, i, lambda j, c: sums(j, c, False), (zero, zero))
        num, den = sums(i, carry, True)
        rowterm = num / den

        def step(j, acc, masked):
            p, dp, kb = p_dp(j, masked)
            ds = p * (dp - rowterm)
            return acc + jnp.dot(ds.astype(BF16), kb, preferred_element_type=F32)

        acc = lax.fori_loop(0, i, lambda j, c: step(j, c, False), jnp.zeros((tq, HEAD_DIM), F32))
        acc = step(i, acc, True)
        dq_ref[...] = (acc * scale).astype(dq_ref.dtype)
        row_ref[...] = jnp.broadcast_to(rowterm, (tq, LANES))

    tile = pl.BlockSpec((tq, HEAD_DIM), lambda h, i: (i, h))
    full = pl.BlockSpec((S, HEAD_DIM), lambda h, i: (0, h))
    stat = pl.BlockSpec((None, tq, LANES), lambda h, i: (h, i, 0))
    return pl.pallas_call(
        body, name=name, grid=(H, nq),
        in_specs=[tile, full, full, tile, stat, pl.BlockSpec((nq, 8, tq), lambda h, i: (0, h // 8, 0))],
        out_specs=[tile, stat], out_shape=[_sds((S, D), BF16), _sds((H, S, LANES), F32)],
        compiler_params=_params(("parallel", "arbitrary")),
    )(q, k, v, do, lse, fk)


def fox_dkv(q, k, v, do, rowterm, lse, fk, tq, name):
    S, D = q.shape
    H, nq = D // HEAD_DIM, S // tq

    def body(q_ref, k_ref, v_ref, do_ref, row_ref, lse_ref, fk_ref, dk_ref, dv_ref, dfk_ref):
        h, j = pl.program_id(0), pl.program_id(1)
        kb, vb = k_ref[...], v_ref[...]
        fkr = fk_ref[pl.ds(h % 8, 1), :]

        def step(i, carry, masked):
            dk, dv, dfk = carry
            off = pl.multiple_of(i * tq, tq)
            qb, dob = q_ref[pl.ds(off, tq), :], do_ref[pl.ds(off, tq), :]
            delta = row_ref[pl.ds(off, tq), 0:1]
            lsec = lse_ref[pl.ds(off, tq), 0:1]
            s = lax.dot_general(qb, kb, _NT, preferred_element_type=F32) - fkr
            if masked:
                s = jnp.where(_causal(tq), s, NEG)
            p = jnp.exp(s - lsec)
            dv = dv + lax.dot_general(p.astype(BF16), dob, _TN, preferred_element_type=F32)
            dp = lax.dot_general(dob, vb, _NT, preferred_element_type=F32)
            ds = p * (dp - delta)
            dk = dk + lax.dot_general(ds.astype(BF16), qb, _TN, preferred_element_type=F32)
            return dk, dv, dfk - _colsum(ds)

        init = (jnp.zeros((tq, HEAD_DIM), F32), jnp.zeros((tq, HEAD_DIM), F32), jnp.zeros((1, tq), F32))
        carry = step(j, init, True)
        dk, dv, dfk = lax.fori_loop(j + 1, nq, lambda i, c: step(i, c, False), carry)
        dk_ref[...] = dk.astype(dk_ref.dtype)
        dv_ref[...] = dv.astype(dv_ref.dtype)
        dfk_ref[...] = jnp.broadcast_to(dfk, (8, tq))

    tile = pl.BlockSpec((tq, HEAD_DIM), lambda h, j: (j, h))
    full = pl.BlockSpec((S, HEAD_DIM), lambda h, j: (0, h))
    stat = pl.BlockSpec((None, S, LANES), lambda h, j: (h, 0, 0))
    return pl.pallas_call(
        body, name=name, grid=(H, nq),
        in_specs=[full, tile, tile, full, stat, stat, pl.BlockSpec((None, 8, tq), lambda h, j: (j, h // 8, 0))],
        out_specs=[tile, tile, pl.BlockSpec((None, None, 8, tq), lambda h, j: (h, j, 0, 0))],
        out_shape=[_sds((S, D), BF16), _sds((S, D), BF16), _sds((H, nq, 8, tq), F32)],
        compiler_params=_params(("parallel", "arbitrary")),
    )(q, k, v, do, rowterm, lse, fk)


def _adamw_math(w, g, m, v):
    m = ADAM_B1 * m + (1.0 - ADAM_B1) * g
    v = ADAM_B2 * v + (1.0 - ADAM_B2) * jnp.square(g)
    m_hat = m / (1.0 - ADAM_B1 ** ADAM_STEP)
    v_hat = v / (1.0 - ADAM_B2 ** ADAM_STEP)
    delta = -ADAM_LR * (m_hat / (jnp.sqrt(v_hat) + ADAM_EPS) + ADAM_WD * w)
    return delta, m, v


def adamw(w, g, m, v, name):
    shape = w.shape
    cols = shape[-1]
    two_d = lambda a: a.reshape(-1, cols)
    rows = max(1, w.size // cols)
    tr = rows if rows * cols * 4 <= 2**21 else max(8, (2**21 // (cols * 4)) // 8 * 8)
    while rows % tr:
        tr -= 8
    res = rowmap(_adamw_math, [two_d(w), two_d(g), two_d(m), two_d(v)], [], [(cols, F32)] * 3, tr=tr, name=name)
    return [r.reshape(shape) for r in res]


def adamw_outer(w, sct, dm, m, v, name):
    L, R, C = w.shape
    B = sct.shape[1]
    tr = min(R, 256)

    def body(w_ref, s_ref, d_ref, m_ref, v_ref, g_out, dl_out, m_out, v_out):
        g = jnp.dot(s_ref[...], d_ref[...], preferred_element_type=F32, precision=lax.Precision.HIGHEST)
        delta, mn, vn = _adamw_math(w_ref[...], g, m_ref[...], v_ref[...])
        g_out[...] = g
        dl_out[...] = delta
        m_out[...] = mn
        v_out[...] = vn

    big = pl.BlockSpec((None, tr, C), lambda l, i: (l, i, 0))
    return pl.pallas_call(
        body, name=name, grid=(L, R // tr),
        in_specs=[big, pl.BlockSpec((tr, B), lambda l, i: (i, 0)), pl.BlockSpec((None, B, C), lambda l, i: (l, 0, 0)),
                  big, big],
        out_specs=[big] * 4, out_shape=[_sds((L, R, C), F32)] * 4,
        compiler_params=_params(("parallel", "parallel")),
    )(w, sct, dm, m, v)


def _place():
    x, y, c = lax.axis_index("x"), lax.axis_index("y"), lax.axis_index("c")
    return x, y, c


def _other_chips(x, y):
    return [(1 - x, y), (x, 1 - y), (1 - x, 1 - y)]


def allgather_small(block, name):
    m_per, n = block.shape

    def body(x_ref, out_ref, send_sems, recv_sems, local_sem):
        x, y, c = _place()
        me, sibling = (x, y, c), (x, y, 1 - c)
        chips = _other_chips(x, y)

        def rows(px, py, pc):
            return out_ref.at[pl.ds((4 * px + 2 * py + pc) * m_per, m_per), :]

        def copy(k, block_of, to, src=None):
            return pltpu.make_async_remote_copy(
                src_ref=rows(*block_of) if src is None else src, dst_ref=rows(*block_of),
                send_sem=send_sems.at[k], recv_sem=recv_sems.at[k], device_id=to, device_id_type=MESH)

        mine = pltpu.make_async_copy(x_ref, rows(*me), local_sem)
        mine.start()
        first = [copy(0, me, sibling, src=x_ref)]
        first += [copy(1 + j, me, (*chip, c), src=x_ref) for j, chip in enumerate(chips)]
        for cp in first:
            cp.start()
        passed = [copy(4 + j, (*chip, c), sibling) for j, chip in enumerate(chips)]
        for j, chip in enumerate(chips):
            copy(1 + j, (*chip, c), me).wait_recv()
            passed[j].start()
        copy(0, sibling, me).wait_recv()
        for j, chip in enumerate(chips):
            copy(4 + j, (*chip, 1 - c), me).wait_recv()
        for cp in first + passed:
            cp.wait_send()
        mine.wait()

    return pl.pallas_call(
        body, name=name, out_shape=_sds((8 * m_per, n), block.dtype),
        in_specs=[pl.BlockSpec(memory_space=pltpu.VMEM)], out_specs=pl.BlockSpec(memory_space=pltpu.VMEM),
        scratch_shapes=[pltpu.SemaphoreType.DMA((7,)), pltpu.SemaphoreType.DMA((7,)), pltpu.SemaphoreType.DMA],
        compiler_params=_params(),
    )(block)


def _half(ref, which):
    n = ref.shape[-2] // 2
    idx = (slice(None),) * (len(ref.shape) - 2) + (pl.ds(which * n, n), slice(None))
    return ref.at[idx]


_ANY = pl.BlockSpec(memory_space=pl.ANY)


def exchange_halves(full, name):
    T = len(full)

    def body(*refs):
        ins, outs = refs[:T], refs[T:2 * T]
        send_sems, recv_sems = refs[2 * T:]
        x, y, c = _place()
        cps = [pltpu.make_async_remote_copy(
            src_ref=_half(ins[t], 1 - c), dst_ref=outs[t], send_sem=send_sems.at[t], recv_sem=recv_sems.at[t],
            device_id=(x, y, 1 - c), device_id_type=MESH) for t in range(T)]
        for cp in cps:
            cp.start()
        for cp in cps:
            cp.wait()

    return pl.pallas_call(
        body, name=name,
        out_shape=[_sds((4, f.shape[1] // 2, f.shape[2]), f.dtype) for f in full],
        in_specs=[_ANY] * T, out_specs=[_ANY] * T,
        scratch_shapes=[pltpu.SemaphoreType.DMA((T,)), pltpu.SemaphoreType.DMA((T,))],
        compiler_params=_params(),
    )(*full)


_HBM = pl.BlockSpec(memory_space=pltpu.HBM)
_SEM = pl.BlockSpec(memory_space=pltpu.SEMAPHORE)
_EFFECT = pltpu.SideEffectType.DATAFLOW_SIDE_EFFECTING


def _in_hbm(a):
    return pltpu.with_memory_space_constraint(a, pltpu.HBM)


def _split_params():
    return pltpu.CompilerParams(has_side_effects=_EFFECT, vmem_limit_bytes=VMEM_LIMIT)


def _gather_copy(bufs, t, j, chip_of_data, to, send_sems, recv_sems, c):
    return pltpu.make_async_remote_copy(
        src_ref=_half(bufs[t].at[_my_chip()], c), dst_ref=_half(bufs[t].at[chip_of_data], c),
        send_sem=send_sems.at[3 * t + j], recv_sem=recv_sems.at[3 * t + j], device_id=to, device_id_type=MESH)


def gather_start(bufs, after, name):
    T = len(bufs)

    def body(*refs):
        ins, send_sems, recv_sems, token = refs[:T], refs[T + 1], refs[T + 2], refs[-1]
        x, y, c = _place()
        for j, chip in enumerate(_other_chips(x, y)):
            for t in range(T):
                _gather_copy(ins, t, j, _my_chip(), (*chip, c), send_sems, recv_sems, c).start()
        token[...] = jnp.zeros(token.shape, token.dtype)

    sem = pltpu.SemaphoreType.DMA((3 * T,))
    res = pl.pallas_call(
        body, name=name,
        out_shape=(sem, sem, *[pltpu.HBM(b.shape, b.dtype) for b in bufs], _sds((8, LANES), F32)),
        in_specs=[_HBM] * T + [_ANY], out_specs=(_SEM, _SEM, *[_HBM] * T, pl.BlockSpec(memory_space=pltpu.VMEM)),
        input_output_aliases={t: 2 + t for t in range(T)}, compiler_params=_split_params(),
    )(*[_in_hbm(b) for b in bufs], after)
    return res[0], res[1], list(res[2:2 + T]), res[-1]


def gather_wait(bufs, send_sems, recv_sems, after, name):
    T = len(bufs)

    def body(*refs):
        ins, ssem, rsem = refs[:T], refs[T], refs[T + 1]
        x, y, c = _place()
        for j, (cx, cy) in enumerate(_other_chips(x, y)):
            for t in range(T):
                cp = _gather_copy(ins, t, j, 2 * cx + cy, (x, y, c), ssem, rsem, c)
                cp.wait_send()
                cp.wait_recv()

    return pl.pallas_call(
        body, name=name, out_shape=[pltpu.HBM(b.shape, b.dtype) for b in bufs],
        in_specs=[_HBM] * T + [_SEM, _SEM, _ANY], out_specs=[_HBM] * T,
        input_output_aliases={t: t for t in range(T)}, compiler_params=_split_params(),
    )(*bufs, send_sems, recv_sems, after)


def gather_pass(bufs, name):
    T = len(bufs)

    def body(*refs):
        ins, outs = refs[:T], refs[T:2 * T]
        send_sems, recv_sems = refs[2 * T:]
        x, y, c = _place()
        chips = _other_chips(x, y)

        def d2d(t, j, chip_of_data, which):
            return pltpu.make_async_remote_copy(
                src_ref=_half(ins[t].at[chip_of_data], which), dst_ref=_half(outs[t].at[chip_of_data], which),
                send_sem=send_sems.at[t, j], recv_sem=recv_sems.at[t, j], device_id=(x, y, 1 - c),
                device_id_type=MESH)

        passed = [d2d(t, j, 2 * cx + cy, c) for j, (cx, cy) in enumerate(chips) for t in range(T)]
        for cp in passed:
            cp.start()
        for j, (cx, cy) in enumerate(chips):
            for t in range(T):
                d2d(t, j, 2 * cx + cy, 1 - c).wait_recv()
        for cp in passed:
            cp.wait_send()

    sem = lambda: pltpu.SemaphoreType.DMA((T, 3))
    return pl.pallas_call(
        body, name=name, out_shape=[_sds(b.shape, b.dtype) for b in bufs],
        in_specs=[_ANY] * T, out_specs=[_ANY] * T, input_output_aliases={t: t for t in range(T)},
        scratch_shapes=[sem(), sem()], compiler_params=_params(),
    )(*bufs)


def _scatter_copy(sums, lands, t, j, chip_xy, c, send_sems, recv_sems):
    cx, cy = chip_xy
    return pltpu.make_async_remote_copy(
        src_ref=sums[t].at[2 * cx + cy], dst_ref=lands[t].at[j], send_sem=send_sems.at[3 * t + j],
        recv_sem=recv_sems.at[3 * t + j], device_id=(cx, cy, c), device_id_type=MESH)


def scatter_start(sums, name):
    T = len(sums)
    lands = [lax.empty((3,) + s.shape[1:], s.dtype) for s in sums]

    def body(*refs):
        s_in, l_in = refs[:T], refs[T:2 * T]
        send_sems, recv_sems, token = refs[2 * T], refs[2 * T + 1], refs[-1]
        x, y, c = _place()
        for j, chip in enumerate(_other_chips(x, y)):
            for t in range(T):
                _scatter_copy(s_in, l_in, t, j, chip, c, send_sems, recv_sems).start()
        token[...] = jnp.zeros(token.shape, token.dtype)

    sem = pltpu.SemaphoreType.DMA((3 * T,))
    both = list(sums) + lands
    res = pl.pallas_call(
        body, name=name,
        out_shape=(sem, sem, *[pltpu.HBM(b.shape, b.dtype) for b in both], _sds((8, LANES), F32)),
        in_specs=[_HBM] * (2 * T), out_specs=(_SEM, _SEM, *[_HBM] * (2 * T), pl.BlockSpec(memory_space=pltpu.VMEM)),
        input_output_aliases={t: 2 + t for t in range(2 * T)}, compiler_params=_split_params(),
    )(*[_in_hbm(b) for b in both])
    return res[0], res[1], list(res[2:2 + T]), list(res[2 + T:2 + 2 * T]), res[-1]


def scatter_wait(sums, lands, send_sems, recv_sems, after, name):
    T = len(sums)

    def body(*refs):
        s_in, l_in, ssem, rsem = refs[:T], refs[T:2 * T], refs[2 * T], refs[2 * T + 1]
        x, y, c = _place()
        for j, chip in enumerate(_other_chips(x, y)):
            for t in range(T):
                cp = _scatter_copy(s_in, l_in, t, j, chip, c, ssem, rsem)
                cp.wait_send()
                cp.wait_recv()

    both = list(sums) + list(lands)
    res = pl.pallas_call(
        body, name=name, out_shape=[pltpu.HBM(b.shape, b.dtype) for b in both],
        in_specs=[_HBM] * (2 * T) + [_SEM, _SEM, _ANY], out_specs=[_HBM] * (2 * T),
        input_output_aliases={t: t for t in range(2 * T)}, compiler_params=_split_params(),
    )(*both, send_sems, recv_sems, after)
    return list(res[:T]), list(res[T:])


def join_halves(bufs, name):
    G = len(bufs)
    layers = [(g, l) for g in range(G) for l in range(bufs[g].shape[0])]
    T = len(layers)

    def body(*refs):
        ins, outs = refs[:G], refs[G:2 * G]
        send_sems, recv_sems = refs[2 * G:]
        x, y, c = _place()
        cps = [pltpu.make_async_remote_copy(
            src_ref=_half(ins[g].at[l], c), dst_ref=_half(outs[g].at[l], c), send_sem=send_sems.at[t],
            recv_sem=recv_sems.at[t], device_id=(x, y, 1 - c), device_id_type=MESH) for t, (g, l) in enumerate(layers)]
        for cp in cps:
            cp.start()
        for cp in cps:
            cp.wait()

    return pl.pallas_call(
        body, name=name, out_shape=[_sds(b.shape, b.dtype) for b in bufs],
        in_specs=[_ANY] * G, out_specs=[_ANY] * G, input_output_aliases={g: g for g in range(G)},
        scratch_shapes=[pltpu.SemaphoreType.DMA((T,)), pltpu.SemaphoreType.DMA((T,))],
        compiler_params=_params(),
    )(*bufs)


def _my_chip():
    return 2 * lax.axis_index("x") + lax.axis_index("y")


def add_halves(full, got, name):
    nb, R, C = full.shape
    rh = R // 2
    tr = _fit(512, [rh])
    per = rh // tr

    def body(a_ref, b_ref, o_ref):
        o_ref[...] = (a_ref[...].astype(F32) + b_ref[...].astype(F32)).astype(o_ref.dtype)

    return pl.pallas_call(
        body, name=name, grid=(nb, per),
        in_specs=[pl.BlockSpec((None, tr, C), lambda b, i: (b, lax.axis_index("c") * per + i, 0)),
                  pl.BlockSpec((None, tr, C), lambda b, i: (b, i, 0))],
        out_specs=pl.BlockSpec((None, tr, C), lambda b, i: (b, i, 0)),
        out_shape=_sds((nb, rh, C), BF16),
        compiler_params=_params(("parallel", "parallel")),
    )(full, got)


def finish_sum(sums, got, stacked, n_layers, l, name):
    nb, rh, C = sums.shape
    tr = _fit(512, [rh])
    per = rh // tr

    def body(s_ref, g_ref, *rest):
        o_ref = rest[-1]
        acc = s_ref[...].astype(F32)
        for j in range(3):
            acc = acc + g_ref[j].astype(F32)
        o_ref[...] = acc

    in_specs = [pl.BlockSpec((None, tr, C), lambda i: (_my_chip(), i, 0)),
                pl.BlockSpec((3, tr, C), lambda i: (0, i, 0))]
    args = [sums, got]
    aliases = {}
    if stacked is not None:
        in_specs.append(_ANY)
        args.append(stacked)
        aliases = {2: 0}
    return pl.pallas_call(
        body, name=name, grid=(per,), in_specs=in_specs,
        out_specs=pl.BlockSpec((None, tr, C), lambda i: (l, lax.axis_index("c") * per + i, 0)),
        out_shape=_sds((n_layers, 2 * rh, C), F32), input_output_aliases=aliases,
        compiler_params=_params(("arbitrary",)),
    )(*args)


def sum_devices(gathered, name):
    n_dev, M, N = gathered.shape

    def body(g_ref, o_ref):
        acc = g_ref[0]
        for d in range(1, n_dev):
            acc = acc + g_ref[d]
        o_ref[...] = acc

    tr = 8
    return pl.pallas_call(
        body, name=name, grid=(M // tr,),
        in_specs=[pl.BlockSpec((n_dev, tr, N), lambda i: (0, i, 0))], out_specs=pl.BlockSpec((tr, N), lambda i: (i, 0)),
        out_shape=_sds((M, N), F32), compiler_params=_params(("parallel",)),
    )(gathered)


def _pack(arrays):
    flat = jnp.concatenate([a.reshape(-1).astype(F32) for a in arrays])
    unit = 8 * PACK_COLS
    pad = (-flat.shape[0]) % unit
    return jnp.pad(flat, (0, pad)).reshape(-1, PACK_COLS)


def _unpack(packed, shapes):
    flat = packed.reshape(packed.shape[:-2] + (-1,))
    out, off = [], 0
    for s in shapes:
        n = math.prod(s)
        out.append(flat[..., off:off + n].reshape(packed.shape[:-2] + tuple(s)))
        off += n
    return out


def to_bf16(w, l, name):
    _, R, C = w.shape
    tr = _fit(512, [R])

    def body(w_ref, o_ref):
        o_ref[...] = w_ref[...].astype(BF16)

    return pl.pallas_call(
        body, name=name, grid=(R // tr,),
        in_specs=[pl.BlockSpec((None, tr, C), lambda i: (l, i, 0))],
        out_specs=pl.BlockSpec((None, tr, C), lambda i: (_my_chip(), i, 0)),
        out_shape=_sds((4, R, C), BF16),
        compiler_params=_params(("parallel",)),
    )(w)


def kernel(x, c, ada_w, ada_b, norm_g, mlp_w1, mlp_w2, gmlp_w_in, gmlp_ln_g, gmlp_ln_b, gmlp_ws, gmlp_bs, gmlp_w_out, kv_norm_g, kv_ada_w, kv_ada_b, w_kv, k_norm_g, w_f, b_f, attn_wq, q_norm_g, attn_wo, loss_target, m_ada_w, m_ada_b, m_norm_g, m_mlp_w1, m_mlp_w2, m_gmlp_w_in, m_gmlp_ln_g, m_gmlp_ln_b, m_gmlp_ws, m_gmlp_bs, m_gmlp_w_out, m_kv_norm_g, m_kv_ada_w, m_kv_ada_b, m_w_kv, m_k_norm_g, m_w_f, m_b_f, m_attn_wq, m_q_norm_g, m_attn_wo, v_ada_w, v_ada_b, v_norm_g, v_mlp_w1, v_mlp_w2, v_gmlp_w_in, v_gmlp_ln_g, v_gmlp_ln_b, v_gmlp_ws, v_gmlp_bs, v_gmlp_w_out, v_kv_norm_g, v_kv_ada_w, v_kv_ada_b, v_w_kv, v_k_norm_g, v_w_f, v_b_f, v_attn_wq, v_q_norm_g, v_attn_wo):
    given = dict(locals())
    S, D = x.shape[1], x.shape[2]
    depth = ada_w.shape[0]
    n_a = gmlp_w_in.shape[0]
    H = D // HEAD_DIM
    G = gmlp_ws.shape[1]
    half = gmlp_w_out.shape[1] * 4
    n_dev = 8
    tq = min(512, S // 4)
    ax, ay, ac = _place()
    chip = 2 * ax + ay
    me = 4 * ax + 2 * ay + ac
    row = lambda v: v.reshape(1, -1)
    x0 = x[0]
    tgt = loss_target[0]

    small_in = [c, w_f, norm_g, gmlp_ln_g, gmlp_ln_b]
    g1 = allgather_small(_pack(small_in), "gather_small_params").reshape(n_dev, -1, PACK_COLS)
    c_all, wf_all, ng_all, lg_all, lb_all = _unpack(g1, [a.shape for a in small_in])
    c_all = c_all[:, 0, :]
    per_chip = lambda a: [a[2 * j] for j in range(4)]
    w_f_full = jnp.concatenate(per_chip(wf_all), axis=0)
    norm_g_full = jnp.concatenate(per_chip(ng_all), axis=-1)
    ln_g_full = jnp.concatenate(per_chip(lg_all), axis=-1)
    ln_b_full = jnp.concatenate(per_chip(lb_all), axis=-1)
    w_f_pad = jnp.pad(w_f_full, ((0, 0), (0, LANES - H))).astype(BF16)
    b_f_pad = jnp.pad(b_f, (0, LANES - H)).reshape(1, LANES)

    sc_all = rowmap(lambda cb: cb * jax.nn.sigmoid(cb), [c_all], [], [(D, F32)], tr=8, name="silu")[0]
    n_loc = ada_w.shape[2]
    ada_b_loc = lax.dynamic_slice_in_dim(ada_b, chip * n_loc, n_loc, axis=1).reshape(1, -1)
    add_bias = lambda acc, b: (acc + b,)
    mod_loc = mm(sc_all, ada_w, outs=[(F32, None)], epilogue=add_bias, extras=[ada_b_loc], name="ada_mod")[0]
    kv_loc_n = kv_ada_w.shape[1]
    kv_b_loc = lax.dynamic_slice_in_dim(kv_ada_b, chip * kv_loc_n, kv_loc_n).reshape(1, -1)
    kvmod_loc = mm(sc_all, kv_ada_w, outs=[(F32, None)], epilogue=add_bias, extras=[kv_b_loc], name="kv_ada_mod")[0]
    g2 = allgather_small(_pack([mod_loc, kvmod_loc]), "gather_mod").reshape(n_dev, -1, PACK_COLS)
    mod_all, kvmod_all = _unpack(g2, [mod_loc.shape, kvmod_loc.shape])
    mod_me = jnp.concatenate(
        [lax.dynamic_index_in_dim(m, me, 0, keepdims=False).reshape(depth, n_loc) for m in per_chip(mod_all)], axis=1)
    kvmod_me = jnp.concatenate([lax.dynamic_index_in_dim(m, me, 0, keepdims=False) for m in per_chip(kvmod_all)])
    mods = [[row(v) for v in jnp.split(mod_me[l], N_MOD)] for l in range(depth)]
    kv_shift, kv_scale = [row(v) for v in jnp.split(kvmod_me, 2)]

    big = ['mlp_w1', 'mlp_w2', 'gmlp_w_in', 'gmlp_w_out', 'w_kv', 'attn_wq', 'attn_wo']
    stacked = {n: (given[n] if given[n].ndim == 3 else given[n][None]) for n in big}
    owner = [(n, l) for n in big for l in range(stacked[n].shape[0])]
    bufs = {o: to_bf16(stacked[o[0]], o[1], f"cast_{o[0]}_{o[1]}") for o in owner}
    W = {}

    def layer_group(l):
        if l < n_a:
            return [('gmlp_w_in', l), ('gmlp_w_out', l), ('mlp_w1', l), ('mlp_w2', l)]
        b = l - n_a
        return ([('w_kv', 0)] if b == 0 else []) + [('attn_wq', b), ('attn_wo', b), ('mlp_w1', l), ('mlp_w2', l)]

    def start_gather(l, after):
        return gather_start([bufs[k] for k in layer_group(l)], after, f"gather_start_{l}")

    def finish_gather(l, pending, after):
        send_sems, recv_sems, thru, _ = pending
        done = gather_wait(thru, send_sems, recv_sems, after, f"gather_wait_{l}")
        W.update(zip(layer_group(l), gather_pass(done, f"gather_pass_{l}")))

    pending = start_gather(0, c)
    finish_gather(0, pending, pending[3])
    rows_of = lambda a: a.reshape(-1, a.shape[-1])
    W1 = lambda l: W['mlp_w1', l]
    W2 = lambda l: rows_of(W['mlp_w2', l])
    Win = lambda a: W['gmlp_w_in', a]
    Wout = lambda a: rows_of(W['gmlp_w_out', a])
    Wkv = lambda: W['w_kv', 0]
    Wq = lambda b: rows_of(W['attn_wq', b])
    Wo = lambda b: rows_of(W['attn_wo', b])
    bsb = jnp.broadcast_to(gmlp_bs[..., None], gmlp_bs.shape + (LANES,))

    def resid(acc, xr, gate):
        return xr + gate * acc, acc

    saved = []
    xs = x0
    kv = None
    for l in range(depth):
        sh1, sc1, gt1, sh2, sc2, gt2 = mods[l]
        ng0, ng1 = row(norm_g_full[l, 0]), row(norm_g_full[l, 1])
        if l + 1 < depth:
            pending = start_gather(l + 1, W[layer_group(l)[0]])
            sh1 = sh1 + pending[3][0:1, 0:1]
        st = {"x": xs}
        h1 = modnorm_fwd(xs, ng0, sc1, sh1, f"norm1_{l}")
        st["h1"] = h1
        if l < n_a:
            t, z = mm(h1, Win(l), outs=[(BF16, None), (BF16, None)], epilogue=lambda acc: (acc, _gelu(acc)),
                      name=f"gmlp_in_{l}")
            vn = gmlp_ln_fwd(z, row(ln_g_full[l]), row(ln_b_full[l]), f"gmlp_ln_{l}")
            p = gmlp_mix_fwd(z, vn, gmlp_ws[l], bsb[l], f"gmlp_mix_{l}")
            x1, y = mm(p, Wout(l), outs=[(F32, None), (BF16, None)], epilogue=resid, extras=[xs, gt1],
                       name=f"gmlp_out_{l}")
            st.update(t=t, z=z, vn=vn, p=p, y=y)
        else:
            if kv is None:
                hk = modnorm_fwd(xs, row(kv_norm_g), kv_scale, kv_shift, "kv_norm")
                kg = row(k_norm_g)
                kp, kk = mm(hk, Wkv()[:2], outs=[(BF16, None), (BF16, None)], extras=[kg],
                            epilogue=lambda acc, g: (acc, _head_norm(acc, g)), name="kv_k")
                vv = mm(hk, Wkv()[2:], outs=[(BF16, None)], name="kv_v")[0]
                zf, logf = mm(hk, w_f_pad, outs=[(F32, None), (F32, None)], extras=[b_f_pad],
                              epilogue=lambda acc, b: (acc + b, jax.nn.log_sigmoid(acc + b)), name="kv_f")
                fc, fk = fcum_fwd(logf, tq, "fcum")
                kv = dict(x=xs, hk=hk, kp=kp, k=kk, v=vv, zf=zf, fc=fc, fk=fk)
            b = l - n_a
            qp, q = mm(h1, Wq(b), outs=[(BF16, None), (BF16, None)], extras=[row(q_norm_g[b])],
                       epilogue=lambda acc, g: (acc, _head_norm(acc, g) * (1.0 / math.sqrt(HEAD_DIM))),
                       name=f"attn_q_{l}")
            o, lse = fox_fwd(q, kv["k"], kv["v"], kv["fk"], tq, f"fox_fwd_{l}")
            x1, y = mm(o, Wo(b), outs=[(F32, None), (BF16, None)], epilogue=resid, extras=[xs, gt1],
                       name=f"attn_o_{l}")
            st.update(qp=qp, q=q, o=o, lse=lse, y=y)
        st["x1"] = x1
        h2 = modnorm_fwd(x1, ng1, sc2, sh2, f"norm2_{l}")
        a_pre, a_sq = mm(h2, W1(l), outs=[(BF16, None), (BF16, None)],
                         epilogue=lambda acc: (acc, jnp.square(jnp.maximum(acc, 0.0))), name=f"mlp_up_{l}")
        xs, y2 = mm(a_sq, W2(l), outs=[(F32, None), (BF16, None)], epilogue=resid, extras=[x1, gt2],
                    name=f"mlp_down_{l}")
        st.update(h2=h2, a=a_pre, a_sq=a_sq, y2=y2)
        saved.append(st)
        if l + 1 < depth:
            finish_gather(l + 1, pending, xs)

    def loss_fn(yb, tb_):
        e = yb - tb_
        return e * (1.0 / D), _colsum(e * e)
    dx, sq = rowmap(loss_fn, [xs, tgt], [], [(D, F32)], reds=[(1, D)], tr=512, name="loss")
    loss = lax.psum(0.5 * jnp.sum(sq) / D, ("x", "y", "c"))

    big_grads = {n: [None] * stacked[n].shape[0] for n in big}
    dmod = [None] * depth
    d_norm_g = [[None, None] for _ in range(depth)]
    small = {}
    d_ws, d_bs, d_lg, d_lb, d_qg = [None] * n_a, [None] * n_a, [None] * n_a, [None] * n_a, [None] * (depth - n_a)
    dk_parts, dv_parts, dfk_parts = [], [], []
    quarter = lambda g: g.reshape(4, g.shape[0] // 4, g.shape[1])
    stacks = {n: None for n in big}

    def start_scatter(l):
        keys = layer_group(l)
        flat = [big_grads[n][i] for n, i in keys]
        got = exchange_halves(flat, f"grad_exchange_halves_{l}")
        sums = [add_halves(f, g, f"grad_chip_sum_{n}_{i}") for f, g, (n, i) in zip(flat, got, keys)]
        return scatter_start(sums, f"grad_scatter_start_{l}")

    def finish_scatter(l, pend, after):
        send_sems, recv_sems, sums, lands, _ = pend
        sums, lands = scatter_wait(sums, lands, send_sems, recv_sems, after, f"grad_scatter_wait_{l}")
        for s, g, (n, i) in zip(sums, lands, layer_group(l)):
            stacks[n] = finish_sum(s, g, stacks[n], stacked[n].shape[0], i, f"grad_finish_{n}_{i}")

    scattering = None
    for l in reversed(range(depth)):
        sh1, sc1, gt1, sh2, sc2, gt2 = mods[l]
        ng0, ng1 = row(norm_g_full[l, 0]), row(norm_g_full[l, 1])
        st = saved[l]
        if scattering is not None:
            gt2 = gt2 + scattering[4][0:1, 0:1]
        dy2, dgt2 = gate_bwd(dx, st["y2"], gt2, f"gate2_bwd_{l}")
        big_grads['mlp_w2'][l] = quarter(mm(st["a_sq"], dy2, ta=True, outs=[(BF16, None)], name=f"mlp_w2_grad_{l}")[0])
        da = mm(dy2, W2(l), tb=True, outs=[(BF16, None)], extras=[st["a"]],
                epilogue=lambda acc, a: (acc * (2.0 * jnp.maximum(a.astype(F32), 0.0)),), name=f"mlp_down_bwd_{l}")[0]
        big_grads['mlp_w1'][l] = mm(st["h2"], da, ta=True, outs=[(BF16, 4)], name=f"mlp_w1_grad_{l}")[0]
        dh2 = mm(da, W1(l), tb=True, outs=[(BF16, None)], name=f"mlp_up_bwd_{l}")[0]
        dx, dsh2, d_norm_g[l][1], dsc2 = modnorm_bwd(st["x1"], dh2, dx, ng1, sc2, f"norm2_bwd_{l}")
        dy, dgt1 = gate_bwd(dx, st["y"], gt1, f"gate1_bwd_{l}")
        if l < n_a:
            big_grads['gmlp_w_out'][l] = quarter(
                mm(st["p"], dy, ta=True, outs=[(BF16, None)], name=f"gmlp_w_out_grad_{l}")[0])
            dp = mm(dy, Wout(l), tb=True, outs=[(BF16, None)], name=f"gmlp_out_bwd_{l}")[0]
            du, dvn, d_ws[l], db = gmlp_mix_bwd(dp, st["z"], st["vn"], gmlp_ws[l], bsb[l], f"gmlp_mix_bwd_{l}")
            d_bs[l] = db[:, :, 0]
            dt, d_lg[l], d_lb[l] = gmlp_act_bwd(st["t"], du, dvn, row(ln_g_full[l]), f"gmlp_act_bwd_{l}")
            big_grads['gmlp_w_in'][l] = mm(st["h1"], dt, ta=True, outs=[(BF16, 4)], name=f"gmlp_w_in_grad_{l}")[0]
            dh1 = mm(dt, Win(l), tb=True, outs=[(BF16, None)], name=f"gmlp_in_bwd_{l}")[0]
        else:
            b = l - n_a
            big_grads['attn_wo'][b] = quarter(
                mm(st["o"], dy, ta=True, outs=[(BF16, None)], name=f"attn_wo_grad_{l}")[0])
            do = mm(dy, Wo(b), tb=True, outs=[(BF16, None)], name=f"attn_o_bwd_{l}")[0]
            dq, rowterm = fox_dq(st["q"], kv["k"], kv["v"], do, st["lse"], kv["fk"], tq, f"fox_dq_{l}")
            dk_l, dv_l, dfk_l = fox_dkv(st["q"], kv["k"], kv["v"], do, rowterm, st["lse"], kv["fk"], tq, f"fox_dkv_{l}")
            dk_parts.append(dk_l)
            dv_parts.append(dv_l)
            dfk_parts.append(jnp.pad(dfk_l[:, :, 0, :].reshape(H, S), ((0, LANES - H), (0, 0))))
            dqp, d_qg[b] = head_norm_bwd(st["qp"], dq, row(q_norm_g[b]), f"q_norm_bwd_{l}")
            big_grads['attn_wq'][b] = quarter(
                mm(st["h1"], dqp, ta=True, outs=[(BF16, None)], name=f"attn_wq_grad_{l}")[0])
            dh1 = mm(dqp, Wq(b), tb=True, outs=[(BF16, None)], name=f"attn_q_bwd_{l}")[0]
        dx, dsh1, d_norm_g[l][0], dsc1 = modnorm_bwd(st["x"], dh1, dx, ng0, sc1, f"norm1_bwd_{l}")
        dmod[l] = jnp.concatenate([dsh1, dsc1, dgt1, dsh2, dsc2, dgt2], axis=1)
        if l == n_a:
            add2 = lambda a, b_: a.astype(F32) + b_.astype(F32)
            dk_sum = rowmap(add2, dk_parts, [], [(D, BF16)], tr=512, name="dk_sum")[0]
            dv_sum = rowmap(add2, dv_parts, [], [(D, BF16)], tr=512, name="dv_sum")[0]
            dkvp, small['k_norm_g'] = head_norm_bwd(kv["kp"], dk_sum, row(k_norm_g), "k_norm_bwd", extra=dv_sum)
            dzf, db_f = fcum_bwd(dfk_parts[0], dfk_parts[1], kv["zf"], H, "fcum_bwd")
            small['b_f'] = db_f[0, :H]
            big_grads['w_kv'][0] = mm(kv["hk"], dkvp, ta=True, outs=[(BF16, 4)], name="w_kv_grad")[0]
            small['w_f'] = mm(kv["hk"], dzf, ta=True, outs=[(F32, None)], name="w_f_grad")[0][:, :H]
            dhk_f = mm(dzf, w_f_pad, tb=True, outs=[(BF16, None)], name="kv_f_bwd")[0]
            dhk = mm(dkvp, Wkv(), tb=True, outs=[(BF16, None)], extras=[dhk_f],
                     epilogue=lambda acc, e: (acc + e.astype(F32),), name="kv_bwd")[0]
            dx, dkv_shift, small['kv_norm_g'], dkv_scale = modnorm_bwd(
                kv["x"], dhk, dx, row(kv_norm_g), kv_scale, "kv_norm_bwd")
            dkvmod = jnp.concatenate([dkv_shift, dkv_scale], axis=1)
        if scattering is not None:
            finish_scatter(l + 1, scattering, dx)
        scattering = start_scatter(l)
    dmod[0] = dmod[0] + scattering[4][0:1, 0:1]
    grad_x = dx[None]

    small['norm_g'] = jnp.stack([jnp.concatenate(p, axis=0) for p in d_norm_g])
    small['gmlp_ln_g'] = jnp.concatenate(d_lg, axis=0)
    small['gmlp_ln_b'] = jnp.concatenate(d_lb, axis=0)
    small['gmlp_ws'] = jnp.stack(d_ws)
    small['gmlp_bs'] = jnp.stack(d_bs)
    small['q_norm_g'] = jnp.concatenate(d_qg, axis=0)
    small['ada_b'] = jnp.concatenate(dmod, axis=0)
    small['kv_ada_b'] = dkvmod
    names = sorted(small)
    shapes = [small[n].shape for n in names]
    g3 = allgather_small(_pack([small[n] for n in names]), "gather_small_grads").reshape(n_dev, -1, PACK_COLS)
    summed = dict(zip(names, _unpack(sum_devices(g3, "sum_small_grads"), shapes)))
    each = dict(zip(names, _unpack(g3, shapes)))
    local_cols = lambda a, n: lax.dynamic_slice_in_dim(a, chip * n, n, axis=a.ndim - 1)
    grads = {
        'ada_b': summed['ada_b'], 'gmlp_ws': summed['gmlp_ws'], 'gmlp_bs': summed['gmlp_bs'],
        'kv_norm_g': summed['kv_norm_g'].reshape(-1), 'kv_ada_b': summed['kv_ada_b'].reshape(-1),
        'k_norm_g': summed['k_norm_g'].reshape(-1), 'b_f': summed['b_f'], 'q_norm_g': summed['q_norm_g'],
        'norm_g': local_cols(summed['norm_g'], norm_g.shape[2]),
        'gmlp_ln_g': local_cols(summed['gmlp_ln_g'], gmlp_ln_g.shape[1]),
        'gmlp_ln_b': local_cols(summed['gmlp_ln_b'], gmlp_ln_b.shape[1]),
        'w_f': lax.dynamic_slice_in_dim(summed['w_f'], chip * w_f.shape[0], w_f.shape[0], axis=0),
    }
    dmod_all = each['ada_b'].reshape(n_dev, depth, N_MOD * D)
    dm_loc = jnp.transpose(local_cols(dmod_all, n_loc), (1, 0, 2))
    dkv_loc = local_cols(each['kv_ada_b'].reshape(n_dev, 2 * D), kv_loc_n)[None]

    delta, new_m, new_v = {}, {}, {}
    sct = jnp.transpose(sc_all)
    grads['ada_w'], delta['ada_w'], new_m['ada_w'], new_v['ada_w'] = adamw_outer(
        ada_w, sct, dm_loc, m_ada_w, v_ada_w, "adamw_ada_w")
    r = adamw_outer(kv_ada_w[None], sct, dkv_loc, m_kv_ada_w[None], v_kv_ada_w[None], "adamw_kv_ada_w")
    grads['kv_ada_w'], delta['kv_ada_w'], new_m['kv_ada_w'], new_v['kv_ada_w'] = [a[0] for a in r]
    for n in WEIGHTS:
        if n in delta or n in big:
            continue
        grads[n] = grads[n].reshape(given[n].shape)
        delta[n], new_m[n], new_v[n] = adamw(given[n], grads[n], given["m_" + n], given["v_" + n], "adamw_" + n)

    finish_scatter(0, scattering, new_v['ada_w'])
    joined = join_halves([stacks[n] for n in big], "grad_join")
    for n, g in zip(big, joined):
        grads[n] = g.reshape(given[n].shape)
        delta[n], new_m[n], new_v[n] = adamw(given[n], grads[n], given["m_" + n], given["v_" + n], "adamw_" + n)
    return (loss, grad_x, *[grads[n] for n in WEIGHTS], *[delta[n] for n in WEIGHTS],
            *[new_m[n] for n in WEIGHTS], *[new_v[n] for n in WEIGHTS])
```

```python
import functools
import math

import jax
import jax.numpy as jnp
from jax import lax
from jax.experimental import pallas as pl
from jax.experimental.pallas import tpu as pltpu

F32 = jnp.float32
BF16 = jnp.bfloat16
EPS = 1e-6
HEAD_DIM = 128
GMLP_BLOCK = 128
CHUNK = 64
LANES = 128
N_MOD = 6
V7X_VMEM_BYTES = 64 * 2**20
VMEM_LIMIT = V7X_VMEM_BYTES - 8 * 2**20
PACK_COLS = 1024
NEG = -1e30
MESH = pl.DeviceIdType.MESH

ADAM_LR = 0.001
ADAM_B1 = 0.9
ADAM_B2 = 0.999
ADAM_EPS = 1e-08
ADAM_WD = 0.01
ADAM_STEP = 10

WEIGHTS = ['ada_w', 'ada_b', 'norm_g', 'mlp_w1', 'mlp_w2', 'gmlp_w_in', 'gmlp_ln_g', 'gmlp_ln_b', 'gmlp_ws',
           'gmlp_bs', 'gmlp_w_out', 'kv_norm_g', 'kv_ada_w', 'kv_ada_b', 'w_kv', 'k_norm_g', 'w_f', 'b_f',
           'attn_wq', 'q_norm_g', 'attn_wo']


def _params(sem=None):
    return pltpu.CompilerParams(dimension_semantics=sem, vmem_limit_bytes=VMEM_LIMIT)


def _sds(shape, dtype):
    return jax.ShapeDtypeStruct(tuple(shape), dtype)


def _ldims(shape):
    return (shape[0], shape[1]) if len(shape) == 2 else (shape[1], shape[0] * shape[2])


def _fit(t, ns):
    n0 = min(ns)
    if n0 <= t and all(n % n0 == 0 for n in ns):
        return n0
    d = (t // LANES) * LANES
    while d > LANES and any(n % d for n in ns):
        d -= LANES
    assert all(n % d == 0 for n in ns), (t, ns)
    return d


def _blk(shape, br, bc):
    if len(shape) == 2:
        return (br, bc), (lambda r, c: (r, c))
    per = shape[2] // bc
    assert shape[2] % bc == 0, (shape, bc)
    return (None, br, bc), (lambda r, c: (c // per, r, c % per))


def mm(a, b, *, name, ta=False, tb=False, outs, epilogue=None, extras=(), tm=1024, tn=1024, tk=2048,
       precision=None):
    ar, ac = _ldims(a.shape)
    br, bc = _ldims(b.shape)
    M, K = (ac, ar) if ta else (ar, ac)
    K2, N = (bc, br) if tb else (br, bc)
    assert K == K2, (name, a.shape, b.shape)
    cons = {"m": [M], "n": [N], "k": [K]}

    def note(shape, dim):
        if len(shape) == 3:
            cons[dim].append(shape[2])

    note(a.shape, "m" if ta else "k")
    note(b.shape, "k" if tb else "n")
    out_shapes = []
    for dt, nb in outs:
        if nb is None:
            out_shapes.append(_sds((M, N), dt))
        else:
            out_shapes.append(_sds((nb, M, N // nb), dt))
            cons["n"].append(N // nb)
    for e in extras:
        note(e.shape, "n")
    tm, tn, tk = _fit(tm, cons["m"]), _fit(tn, cons["n"]), _fit(tk, cons["k"])
    gm, gn, gk = M // tm, N // tn, K // tk

    a_bs, a_ix = _blk(a.shape, tk if ta else tm, tm if ta else tk)
    b_bs, b_ix = _blk(b.shape, tn if tb else tk, tk if tb else tn)
    in_specs = [
        pl.BlockSpec(a_bs, (lambda i, j, k: a_ix(k, i)) if ta else (lambda i, j, k: a_ix(i, k))),
        pl.BlockSpec(b_bs, (lambda i, j, k: b_ix(j, k)) if tb else (lambda i, j, k: b_ix(k, j))),
    ]
    for e in extras:
        if _ldims(e.shape)[0] == 1 and M != 1:
            if e.shape[1] == N:
                in_specs.append(pl.BlockSpec((1, tn), lambda i, j, k: (0, j)))
            else:
                in_specs.append(pl.BlockSpec(e.shape, lambda i, j, k: (0, 0)))
        else:
            e_bs, e_ix = _blk(e.shape, tm, tn)
            in_specs.append(pl.BlockSpec(e_bs, functools.partial(lambda i, j, k, ix: ix(i, j), ix=e_ix)))
    out_specs = []
    for s in out_shapes:
        o_bs, o_ix = _blk(s.shape, tm, tn)
        out_specs.append(pl.BlockSpec(o_bs, functools.partial(lambda i, j, k, ix: ix(i, j), ix=o_ix)))
    n_e, n_o = len(extras), len(outs)
    dims = (((0 if ta else 1,), (1 if tb else 0,)), ((), ()))

    def body(*refs):
        a_ref, b_ref = refs[:2]
        e_refs = refs[2:2 + n_e]
        o_refs = refs[2 + n_e:2 + n_e + n_o]
        x, w = a_ref[...], b_ref[...]
        if precision is None:
            x, w = x.astype(BF16), w.astype(BF16)
        d = lax.dot_general(x, w, dims, preferred_element_type=F32, precision=precision)

        def finish(acc):
            res = epilogue(acc, *[e[...] for e in e_refs]) if epilogue is not None else (acc,)
            for o, r in zip(o_refs, res):
                o[...] = r.astype(o.dtype)

        if gk == 1:
            finish(d)
        else:
            acc_ref = refs[-1]
            k = pl.program_id(2)

            @pl.when(k == 0)
            def _():
                acc_ref[...] = d

            @pl.when(jnp.logical_and(k > 0, k < gk - 1))
            def _():
                acc_ref[...] += d

            @pl.when(k == gk - 1)
            def _():
                finish(acc_ref[...] + d)

    return pl.pallas_call(
        body, name=name, grid=(gm, gn, gk), in_specs=in_specs, out_specs=out_specs, out_shape=out_shapes,
        scratch_shapes=[pltpu.VMEM((tm, tn), F32)] if gk > 1 else [],
        compiler_params=_params(("parallel", "parallel", "arbitrary")),
    )(a, b, *extras)


def rowmap(fn, rows, vecs, outs, reds=(), *, tr, name):
    rows = [r if isinstance(r, tuple) else (r, r.shape[1], 0) for r in rows]
    S = rows[0][0].shape[0]
    tr = min(tr, S)
    assert S % tr == 0, (name, S, tr)
    n_i, n_o = len(rows) + len(vecs), len(outs)
    in_specs = [pl.BlockSpec((tr, w), functools.partial(lambda i, c: (i, c), c=c)) for _, w, c in rows]
    in_specs += [pl.BlockSpec(v.shape, functools.partial(lambda i, n: (0,) * n, n=v.ndim)) for v in vecs]
    out_shape = [_sds((S, f), dt) for f, dt in outs] + [_sds(s, F32) for s in reds]
    out_specs = [pl.BlockSpec((tr, f), lambda i: (i, 0)) for f, _ in outs]
    out_specs += [pl.BlockSpec(s, functools.partial(lambda i, n: (0,) * n, n=len(s))) for s in reds]

    def body(*refs):
        res = fn(*[r[...] for r in refs[:n_i]])
        res = res if isinstance(res, tuple) else (res,)
        for o, r in zip(refs[n_i:n_i + n_o], res[:n_o]):
            o[...] = r.astype(o.dtype)
        if reds:
            d_refs = refs[n_i + n_o:]

            @pl.when(pl.program_id(0) == 0)
            def _():
                for d in d_refs:
                    d[...] = jnp.zeros(d.shape, F32)

            for d, r in zip(d_refs, res[n_o:]):
                d[...] += r

    return pl.pallas_call(
        body, name=name, grid=(S // tr,), in_specs=in_specs, out_specs=out_specs, out_shape=out_shape,
        compiler_params=_params(("arbitrary",) if reds else ("parallel",)),
    )(*[r[0] for r in rows], *vecs)


def _gelu(t):
    return 0.5 * t * (1.0 + lax.erf(t * (1.0 / math.sqrt(2.0))))


def _gelu_grad(t):
    cdf = 0.5 * (1.0 + lax.erf(t * (1.0 / math.sqrt(2.0))))
    return cdf + t * jnp.exp(-0.5 * t * t) * (1.0 / math.sqrt(2.0 * math.pi))


def _colsum(v):
    return jnp.sum(v, axis=0, keepdims=True)


def modnorm_fwd(x, g, scale, shift, name):
    def fn(xb, gb, sc, sh):
        rstd = lax.rsqrt(jnp.mean(xb * xb, axis=-1, keepdims=True) + EPS)
        return ((xb * rstd) * gb) * (1.0 + sc) + sh
    return rowmap(fn, [x], [g, scale, shift], [(x.shape[1], BF16)], tr=512, name=name)[0]


def modnorm_bwd(x, dh, dres, g, scale, name):
    D = x.shape[1]

    def fn(xb, dhb, drb, gb, sc):
        dhb = dhb.astype(F32)
        rstd = lax.rsqrt(jnp.mean(xb * xb, axis=-1, keepdims=True) + EPS)
        xhat = xb * rstd
        a = gb * (1.0 + sc)
        dxhat = dhb * a
        dx = rstd * (dxhat - xhat * jnp.mean(dxhat * xhat, axis=-1, keepdims=True))
        da = _colsum(dhb * xhat)
        return drb + dx, _colsum(dhb), da * (1.0 + sc), da * gb
    return rowmap(fn, [x, dh, dres], [g, scale], [(D, F32)], reds=[(1, D)] * 3, tr=256, name=name)


def gate_bwd(dx, y, gate, name):
    D = dx.shape[1]

    def fn(dxb, yb, gb):
        return gb * dxb, _colsum(dxb * yb.astype(F32))
    return rowmap(fn, [dx, y], [gate], [(D, BF16)], reds=[(1, D)], tr=512, name=name)


def _head_norm(x, g):
    parts = []
    for h in range(x.shape[1] // HEAD_DIM):
        xh = x[:, h * HEAD_DIM:(h + 1) * HEAD_DIM]
        rstd = lax.rsqrt(jnp.mean(xh * xh, axis=-1, keepdims=True) + EPS)
        parts.append(xh * rstd * g)
    return jnp.concatenate(parts, axis=1)


def head_norm_bwd(xp, dy, g, name, extra=None):
    D = xp.shape[1]

    def fn(*blocks):
        xb, dyb = blocks[0].astype(F32), blocks[1].astype(F32)
        gb = blocks[-1]
        parts, dg = [], jnp.zeros((1, HEAD_DIM), F32)
        for h in range(D // HEAD_DIM):
            xh = xb[:, h * HEAD_DIM:(h + 1) * HEAD_DIM]
            dyh = dyb[:, h * HEAD_DIM:(h + 1) * HEAD_DIM]
            rstd = lax.rsqrt(jnp.mean(xh * xh, axis=-1, keepdims=True) + EPS)
            xhat = xh * rstd
            dg = dg + _colsum(dyh * xhat)
            dxhat = dyh * gb
            parts.append(rstd * (dxhat - xhat * jnp.mean(dxhat * xhat, axis=-1, keepdims=True)))
        if extra is not None:
            parts.append(blocks[2].astype(F32))
        return jnp.concatenate(parts, axis=1), dg
    rows = [xp, dy] + ([extra] if extra is not None else [])
    width = D + (extra.shape[1] if extra is not None else 0)
    return rowmap(fn, rows, [g], [(width, BF16)], reds=[(1, HEAD_DIM)], tr=512, name=name)


def gmlp_ln_fwd(z, ln_g, ln_b, name):
    half = z.shape[1] // 2

    def fn(vb, gb, bb):
        vb = vb.astype(F32)
        mu = jnp.mean(vb, axis=-1, keepdims=True)
        var = jnp.mean(jnp.square(vb - mu), axis=-1, keepdims=True)
        return ((vb - mu) * lax.rsqrt(var + EPS)) * gb + bb
    return rowmap(fn, [(z, half, 1)], [ln_g, ln_b], [(half, BF16)], tr=256, name=name)[0]


def _mix_mask():
    r = lax.broadcasted_iota(jnp.int32, (GMLP_BLOCK, GMLP_BLOCK), 0) // CHUNK
    c = lax.broadcasted_iota(jnp.int32, (GMLP_BLOCK, GMLP_BLOCK), 1) // CHUNK
    return c <= r


def gmlp_mix_fwd(z, vn, ws, bsb, name):
    S, half = vn.shape
    G = ws.shape[0]
    gd = half // G
    tb = min(512, S // 2)

    def body(u_ref, v_ref, w_ref, b_ref, p_ref):
        w = jnp.where(_mix_mask(), w_ref[...], 0.0).astype(BF16)
        bcol = b_ref[:, 0:1]
        for r in range(tb // GMLP_BLOCK):
            rs = slice(r * GMLP_BLOCK, (r + 1) * GMLP_BLOCK)
            sv = jnp.dot(w, v_ref[rs, :], preferred_element_type=F32) + bcol
            p_ref[rs, :] = (u_ref[rs, :].astype(F32) * sv).astype(p_ref.dtype)

    return pl.pallas_call(
        body, name=name, grid=(S // tb, G),
        in_specs=[pl.BlockSpec((tb, gd), lambda n, g: (n, g)), pl.BlockSpec((tb, gd), lambda n, g: (n, g)),
                  pl.BlockSpec((None, GMLP_BLOCK, GMLP_BLOCK), lambda n, g: (g, 0, 0)),
                  pl.BlockSpec((None, GMLP_BLOCK, LANES), lambda n, g: (g, 0, 0))],
        out_specs=pl.BlockSpec((tb, gd), lambda n, g: (n, g)),
        out_shape=_sds((S, half), BF16),
        compiler_params=_params(("parallel", "parallel")),
    )(z, vn, ws, bsb)


def gmlp_mix_bwd(dp, z, vn, ws, bsb, name):
    S, half = vn.shape
    G = ws.shape[0]
    gd = half // G
    tb = min(512, S // 2)

    def body(dp_ref, u_ref, v_ref, w_ref, b_ref, du_ref, dv_ref, dw_ref, db_ref):
        n = pl.program_id(1)
        mask = _mix_mask()
        w = jnp.where(mask, w_ref[...], 0.0).astype(BF16)
        bcol = b_ref[:, 0:1]
        dw = jnp.zeros((GMLP_BLOCK, GMLP_BLOCK), F32)
        db = jnp.zeros((GMLP_BLOCK, 1), F32)
        for r in range(tb // GMLP_BLOCK):
            rs = slice(r * GMLP_BLOCK, (r + 1) * GMLP_BLOCK)
            vb = v_ref[rs, :]
            dpb = dp_ref[rs, :].astype(F32)
            sv = jnp.dot(w, vb, preferred_element_type=F32) + bcol
            du_ref[rs, :] = (dpb * sv).astype(du_ref.dtype)
            dsv = dpb * u_ref[rs, :].astype(F32)
            dsv16 = dsv.astype(BF16)
            dv_ref[rs, :] = lax.dot_general(w, dsv16, (((0,), (0,)), ((), ())),
                                            preferred_element_type=F32).astype(dv_ref.dtype)
            dw = dw + lax.dot_general(dsv16, vb, (((1,), (1,)), ((), ())), preferred_element_type=F32)
            db = db + jnp.sum(dsv, axis=1, keepdims=True)
        dw = jnp.where(mask, dw, 0.0)
        db = jnp.broadcast_to(db, (GMLP_BLOCK, LANES))

        @pl.when(n == 0)
        def _():
            dw_ref[...] = dw
            db_ref[...] = db

        @pl.when(n > 0)
        def _():
            dw_ref[...] += dw
            db_ref[...] += db

    blk = pl.BlockSpec((tb, gd), lambda g, n: (n, g))
    return pl.pallas_call(
        body, name=name, grid=(G, S // tb),
        in_specs=[blk, blk, blk, pl.BlockSpec((None, GMLP_BLOCK, GMLP_BLOCK), lambda g, n: (g, 0, 0)),
                  pl.BlockSpec((None, GMLP_BLOCK, LANES), lambda g, n: (g, 0, 0))],
        out_specs=[blk, blk, pl.BlockSpec((None, GMLP_BLOCK, GMLP_BLOCK), lambda g, n: (g, 0, 0)),
                   pl.BlockSpec((None, GMLP_BLOCK, LANES), lambda g, n: (g, 0, 0))],
        out_shape=[_sds((S, half), BF16), _sds((S, half), BF16), _sds((G, GMLP_BLOCK, GMLP_BLOCK), F32),
                   _sds((G, GMLP_BLOCK, LANES), F32)],
        compiler_params=_params(("parallel", "arbitrary")),
    )(dp, z, vn, ws, bsb)


def gmlp_act_bwd(t, du, dvn, ln_g, name):
    half = du.shape[1]

    def fn(tb_, dub, dvb, gb):
        tb_ = tb_.astype(F32)
        tu, tv = tb_[:, :half], tb_[:, half:]
        dtu = dub.astype(F32) * _gelu_grad(tu)
        v = _gelu(tv)
        mu = jnp.mean(v, axis=-1, keepdims=True)
        vc = v - mu
        rstd = lax.rsqrt(jnp.mean(vc * vc, axis=-1, keepdims=True) + EPS)
        vhat = vc * rstd
        dvb = dvb.astype(F32)
        dvhat = dvb * gb
        dv = rstd * (dvhat - jnp.mean(dvhat, axis=-1, keepdims=True)
                     - vhat * jnp.mean(dvhat * vhat, axis=-1, keepdims=True))
        dtv = dv * _gelu_grad(tv)
        return jnp.concatenate([dtu, dtv], axis=1), _colsum(dvb * vhat), _colsum(dvb)
    return rowmap(fn, [t, du, dvn], [ln_g], [(2 * half, BF16)], reds=[(1, half)] * 2, tr=128, name=name)


def fcum_fwd(logf, tb, name):
    S = logf.shape[0]
    nb = S // tb

    def body(x_ref, fc_ref, fk_ref):
        tri = (lax.broadcasted_iota(jnp.int32, (LANES, LANES), 0)
               >= lax.broadcasted_iota(jnp.int32, (LANES, LANES), 1)).astype(F32)

        def blk(b, carry):
            off = pl.multiple_of(b * LANES, LANES)
            cs = jnp.dot(tri, x_ref[pl.ds(off, LANES), :], preferred_element_type=F32,
                         precision=lax.Precision.HIGHEST) + carry
            fc_ref[pl.ds(off, LANES), :] = cs
            return cs[LANES - 1:LANES, :]

        lax.fori_loop(0, S // LANES, blk, jnp.zeros((1, LANES), F32))
        for b in range(nb):
            fk_ref[b] = fc_ref[b * tb:(b + 1) * tb, :].T

    return pl.pallas_call(
        body, name=name, out_shape=[_sds((S, LANES), F32), _sds((nb, LANES, tb), F32)],
        compiler_params=_params(),
    )(logf)


def fcum_bwd(dfk_a, dfk_b, zf, n_heads, name):
    S = zf.shape[0]

    def body(da_ref, db_ref, zf_ref, dz_ref, dsum_ref, d_ref):
        d_ref[...] = (da_ref[...] + db_ref[...]).T
        triu = (lax.broadcasted_iota(jnp.int32, (LANES, LANES), 0)
                <= lax.broadcasted_iota(jnp.int32, (LANES, LANES), 1)).astype(F32)
        nblk = S // LANES
        live = lax.broadcasted_iota(jnp.int32, (LANES, LANES), 1) < n_heads

        def blk(r, carry):
            carry_row, tot = carry
            off = pl.multiple_of((nblk - 1 - r) * LANES, LANES)
            d_blk = d_ref[pl.ds(off, LANES), :]
            cs = jnp.dot(triu, d_blk, preferred_element_type=F32, precision=lax.Precision.HIGHEST) + carry_row
            dz = jnp.where(live, cs * jax.nn.sigmoid(-zf_ref[pl.ds(off, LANES), :]), 0.0)
            dz_ref[pl.ds(off, LANES), :] = dz.astype(dz_ref.dtype)
            return carry_row + _colsum(d_blk), tot + _colsum(dz)

        _, tot = lax.fori_loop(0, nblk, blk, (jnp.zeros((1, LANES), F32), jnp.zeros((1, LANES), F32)))
        dsum_ref[...] = tot

    return pl.pallas_call(
        body, name=name, out_shape=[_sds((S, LANES), BF16), _sds((1, LANES), F32)],
        scratch_shapes=[pltpu.VMEM((S, LANES), F32)], compiler_params=_params(),
    )(dfk_a, dfk_b, zf)


def _causal(tq):
    return (lax.broadcasted_iota(jnp.int32, (tq, tq), 1) <= lax.broadcasted_iota(jnp.int32, (tq, tq), 0))


_NT = (((1,), (1,)), ((), ()))
_TN = (((0,), (0,)), ((), ()))


def fox_fwd(q, k, v, fk, tq, name):
    S, D = q.shape
    H, nq = D // HEAD_DIM, S // tq

    def body(q_ref, k_ref, v_ref, fk_ref, o_ref, lse_ref):
        h, i = pl.program_id(0), pl.program_id(1)
        qb = q_ref[...]
        hs = h % 8

        def step(j, carry, masked):
            m, l, acc = carry
            off = pl.multiple_of(j * tq, tq)
            kb, vb = k_ref[pl.ds(off, tq), :], v_ref[pl.ds(off, tq), :]
            s = lax.dot_general(qb, kb, _NT, preferred_element_type=F32) - fk_ref[j, pl.ds(hs, 1), :]
            if masked:
                s = jnp.where(_causal(tq), s, NEG)
            m_new = jnp.maximum(m, jnp.max(s, axis=1, keepdims=True))
            alpha = jnp.exp(m - m_new)
            p = jnp.exp(s - m_new)
            l = alpha * l + jnp.sum(p, axis=1, keepdims=True)
            acc = alpha * acc + jnp.dot(p.astype(BF16), vb, preferred_element_type=F32)
            return m_new, l, acc

        init = (jnp.full((tq, 1), NEG, F32), jnp.zeros((tq, 1), F32), jnp.zeros((tq, HEAD_DIM), F32))
        carry = lax.fori_loop(0, i, lambda j, c: step(j, c, False), init)
        m, l, acc = step(i, carry, True)
        o_ref[...] = (acc / l).astype(o_ref.dtype)
        lse_ref[...] = jnp.broadcast_to(m + jnp.log(l), (tq, LANES))

    return pl.pallas_call(
        body, name=name, grid=(H, nq),
        in_specs=[pl.BlockSpec((tq, HEAD_DIM), lambda h, i: (i, h)),
                  pl.BlockSpec((S, HEAD_DIM), lambda h, i: (0, h)),
                  pl.BlockSpec((S, HEAD_DIM), lambda h, i: (0, h)),
                  pl.BlockSpec((nq, 8, tq), lambda h, i: (0, h // 8, 0))],
        out_specs=[pl.BlockSpec((tq, HEAD_DIM), lambda h, i: (i, h)),
                   pl.BlockSpec((None, tq, LANES), lambda h, i: (h, i, 0))],
        out_shape=[_sds((S, D), BF16), _sds((H, S, LANES), F32)],
        compiler_params=_params(("parallel", "arbitrary")),
    )(q, k, v, fk)


def fox_dq(q, k, v, do, lse, fk, tq, name):
    S, D = q.shape
    H, nq = D // HEAD_DIM, S // tq
    scale = 1.0 / math.sqrt(HEAD_DIM)

    def body(q_ref, k_ref, v_ref, do_ref, lse_ref, fk_ref, dq_ref, row_ref):
        h, i = pl.program_id(0), pl.program_id(1)
        qb, dob = q_ref[...], do_ref[...]
        lsec = lse_ref[:, 0:1]
        hs = h % 8

        def p_dp(j, masked):
            off = pl.multiple_of(j * tq, tq)
            kb, vb = k_ref[pl.ds(off, tq), :], v_ref[pl.ds(off, tq), :]
            s = lax.dot_general(qb, kb, _NT, preferred_element_type=F32) - fk_ref[j, pl.ds(hs, 1), :]
            if masked:
                s = jnp.where(_causal(tq), s, NEG)
            return jnp.exp(s - lsec), lax.dot_general(dob, vb, _NT, preferred_element_type=F32), kb

        def sums(j, carry, masked):
            p, dp, _ = p_dp(j, masked)
            return carry[0] + jnp.sum(p * dp, axis=1, keepdims=True), carry[1] + jnp.sum(p, axis=1, keepdims=True)

        zero = jnp.zeros((tq, 1), F32)
        carry = lax.fori_loop(0, i, lambda j, c: sums(j, c, False), (zero, zero))
        num, den = sums(i, carry, True)
        rowterm = num / den

        def step(j, acc, masked):
            p, dp, kb = p_dp(j, masked)
            ds = p * (dp - rowterm)
            return acc + jnp.dot(ds.astype(BF16), kb, preferred_element_type=F32)

        acc = lax.fori_loop(0, i, lambda j, c: step(j, c, False), jnp.zeros((tq, HEAD_DIM), F32))
        acc = step(i, acc, True)
        dq_ref[...] = (acc * scale).astype(dq_ref.dtype)
        row_ref[...] = jnp.broadcast_to(rowterm, (tq, LANES))

    tile = pl.BlockSpec((tq, HEAD_DIM), lambda h, i: (i, h))
    full = pl.BlockSpec((S, HEAD_DIM), lambda h, i: (0, h))
    stat = pl.BlockSpec((None, tq, LANES), lambda h, i: (h, i, 0))
    return pl.pallas_call(
        body, name=name, grid=(H, nq),
        in_specs=[tile, full, full, tile, stat, pl.BlockSpec((nq, 8, tq), lambda h, i: (0, h // 8, 0))],
        out_specs=[tile, stat], out_shape=[_sds((S, D), BF16), _sds((H, S, LANES), F32)],
        compiler_params=_params(("parallel", "arbitrary")),
    )(q, k, v, do, lse, fk)


def fox_dkv(q, k, v, do, rowterm, lse, fk, tq, name):
    S, D = q.shape
    H, nq = D // HEAD_DIM, S // tq

    def body(q_ref, k_ref, v_ref, do_ref, row_ref, lse_ref, fk_ref, dk_ref, dv_ref, dfk_ref):
        h, j = pl.program_id(0), pl.program_id(1)
        kb, vb = k_ref[...], v_ref[...]
        fkr = fk_ref[pl.ds(h % 8, 1), :]

        def step(i, carry, masked):
            dk, dv, dfk = carry
            off = pl.multiple_of(i * tq, tq)
            qb, dob = q_ref[pl.ds(off, tq), :], do_ref[pl.ds(off, tq), :]
            delta = row_ref[pl.ds(off, tq), 0:1]
            lsec = lse_ref[pl.ds(off, tq), 0:1]
            s = lax.dot_general(qb, kb, _NT, preferred_element_type=F32) - fkr
            if masked:
                s = jnp.where(_causal(tq), s, NEG)
            p = jnp.exp(s - lsec)
            dv = dv + lax.dot_general(p.astype(BF16), dob, _TN, preferred_element_type=F32)
            dp = lax.dot_general(dob, vb, _NT, preferred_element_type=F32)
            ds = p * (dp - delta)
            dk = dk + lax.dot_general(ds.astype(BF16), qb, _TN, preferred_element_type=F32)
            return dk, dv, dfk - _colsum(ds)

        init = (jnp.zeros((tq, HEAD_DIM), F32), jnp.zeros((tq, HEAD_DIM), F32), jnp.zeros((1, tq), F32))
        carry = step(j, init, True)
        dk, dv, dfk = lax.fori_loop(j + 1, nq, lambda i, c: step(i, c, False), carry)
        dk_ref[...] = dk.astype(dk_ref.dtype)
        dv_ref[...] = dv.astype(dv_ref.dtype)
        dfk_ref[...] = jnp.broadcast_to(dfk, (8, tq))

    tile = pl.BlockSpec((tq, HEAD_DIM), lambda h, j: (j, h))
    full = pl.BlockSpec((S, HEAD_DIM), lambda h, j: (0, h))
    stat = pl.BlockSpec((None, S, LANES), lambda h, j: (h, 0, 0))
    return pl.pallas_call(
        body, name=name, grid=(H, nq),
        in_specs=[full, tile, tile, full, stat, stat, pl.BlockSpec((None, 8, tq), lambda h, j: (j, h // 8, 0))],
        out_specs=[tile, tile, pl.BlockSpec((None, None, 8, tq), lambda h, j: (h, j, 0, 0))],
        out_shape=[_sds((S, D), BF16), _sds((S, D), BF16), _sds((H, nq, 8, tq), F32)],
        compiler_params=_params(("parallel", "arbitrary")),
    )(q, k, v, do, rowterm, lse, fk)


def _adamw_math(w, g, m, v):
    m = ADAM_B1 * m + (1.0 - ADAM_B1) * g
    v = ADAM_B2 * v + (1.0 - ADAM_B2) * jnp.square(g)
    m_hat = m / (1.0 - ADAM_B1 ** ADAM_STEP)
    v_hat = v / (1.0 - ADAM_B2 ** ADAM_STEP)
    delta = -ADAM_LR * (m_hat / (jnp.sqrt(v_hat) + ADAM_EPS) + ADAM_WD * w)
    return delta, m, v


def adamw(w, g, m, v, name):
    shape = w.shape
    cols = shape[-1]
    two_d = lambda a: a.reshape(-1, cols)
    rows = max(1, w.size // cols)
    tr = rows if rows * cols * 4 <= 2**21 else max(8, (2**21 // (cols * 4)) // 8 * 8)
    while rows % tr:
        tr -= 8
    res = rowmap(_adamw_math, [two_d(w), two_d(g), two_d(m), two_d(v)], [], [(cols, F32)] * 3, tr=tr, name=name)
    return [r.reshape(shape) for r in res]


def adamw_outer(w, sct, dm, m, v, name):
    L, R, C = w.shape
    B = sct.shape[1]
    tr = min(R, 256)

    def body(w_ref, s_ref, d_ref, m_ref, v_ref, g_out, dl_out, m_out, v_out):
        g = jnp.dot(s_ref[...], d_ref[...], preferred_element_type=F32, precision=lax.Precision.HIGHEST)
        delta, mn, vn = _adamw_math(w_ref[...], g, m_ref[...], v_ref[...])
        g_out[...] = g
        dl_out[...] = delta
        m_out[...] = mn
        v_out[...] = vn

    big = pl.BlockSpec((None, tr, C), lambda l, i: (l, i, 0))
    return pl.pallas_call(
        body, name=name, grid=(L, R // tr),
        in_specs=[big, pl.BlockSpec((tr, B), lambda l, i: (i, 0)), pl.BlockSpec((None, B, C), lambda l, i: (l, 0, 0)),
                  big, big],
        out_specs=[big] * 4, out_shape=[_sds((L, R, C), F32)] * 4,
        compiler_params=_params(("parallel", "parallel")),
    )(w, sct, dm, m, v)


def _place():
    x, y, c = lax.axis_index("x"), lax.axis_index("y"), lax.axis_index("c")
    return x, y, c


def _other_chips(x, y):
    return [(1 - x, y), (x, 1 - y), (1 - x, 1 - y)]


def allgather_small(block, name):
    m_per, n = block.shape

    def body(x_ref, out_ref, send_sems, recv_sems, local_sem):
        x, y, c = _place()
        me, sibling = (x, y, c), (x, y, 1 - c)
        chips = _other_chips(x, y)

        def rows(px, py, pc):
            return out_ref.at[pl.ds((4 * px + 2 * py + pc) * m_per, m_per), :]

        def copy(k, block_of, to, src=None):
            return pltpu.make_async_remote_copy(
                src_ref=rows(*block_of) if src is None else src, dst_ref=rows(*block_of),
                send_sem=send_sems.at[k], recv_sem=recv_sems.at[k], device_id=to, device_id_type=MESH)

        mine = pltpu.make_async_copy(x_ref, rows(*me), local_sem)
        mine.start()
        first = [copy(0, me, sibling, src=x_ref)]
        first += [copy(1 + j, me, (*chip, c), src=x_ref) for j, chip in enumerate(chips)]
        for cp in first:
            cp.start()
        passed = [copy(4 + j, (*chip, c), sibling) for j, chip in enumerate(chips)]
        for j, chip in enumerate(chips):
            copy(1 + j, (*chip, c), me).wait_recv()
            passed[j].start()
        copy(0, sibling, me).wait_recv()
        for j, chip in enumerate(chips):
            copy(4 + j, (*chip, 1 - c), me).wait_recv()
        for cp in first + passed:
            cp.wait_send()
        mine.wait()

    return pl.pallas_call(
        body, name=name, out_shape=_sds((8 * m_per, n), block.dtype),
        in_specs=[pl.BlockSpec(memory_space=pltpu.VMEM)], out_specs=pl.BlockSpec(memory_space=pltpu.VMEM),
        scratch_shapes=[pltpu.SemaphoreType.DMA((7,)), pltpu.SemaphoreType.DMA((7,)), pltpu.SemaphoreType.DMA],
        compiler_params=_params(),
    )(block)


def _half(ref, which):
    n = ref.shape[-2] // 2
    idx = (slice(None),) * (len(ref.shape) - 2) + (pl.ds(which * n, n), slice(None))
    return ref.at[idx]


_ANY = pl.BlockSpec(memory_space=pl.ANY)


def exchange_halves(full, name):
    T = len(full)

    def body(*refs):
        ins, outs = refs[:T], refs[T:2 * T]
        send_sems, recv_sems = refs[2 * T:]
        x, y, c = _place()
        cps = [pltpu.make_async_remote_copy(
            src_ref=_half(ins[t], 1 - c), dst_ref=outs[t], send_sem=send_sems.at[t], recv_sem=recv_sems.at[t],
            device_id=(x, y, 1 - c), device_id_type=MESH) for t in range(T)]
        for cp in cps:
            cp.start()
        for cp in cps:
            cp.wait()

    return pl.pallas_call(
        body, name=name,
        out_shape=[_sds((4, f.shape[1] // 2, f.shape[2]), f.dtype) for f in full],
        in_specs=[_ANY] * T, out_specs=[_ANY] * T,
        scratch_shapes=[pltpu.SemaphoreType.DMA((T,)), pltpu.SemaphoreType.DMA((T,))],
        compiler_params=_params(),
    )(*full)


_HBM = pl.BlockSpec(memory_space=pltpu.HBM)
_SEM = pl.BlockSpec(memory_space=pltpu.SEMAPHORE)
_EFFECT = pltpu.SideEffectType.DATAFLOW_SIDE_EFFECTING


def _in_hbm(a):
    return pltpu.with_memory_space_constraint(a, pltpu.HBM)


def _split_params():
    return pltpu.CompilerParams(has_side_effects=_EFFECT, vmem_limit_bytes=VMEM_LIMIT)


def _gather_copy(bufs, t, j, chip_of_data, to, send_sems, recv_sems, c):
    return pltpu.make_async_remote_copy(
        src_ref=_half(bufs[t].at[_my_chip()], c), dst_ref=_half(bufs[t].at[chip_of_data], c),
        send_sem=send_sems.at[3 * t + j], recv_sem=recv_sems.at[3 * t + j], device_id=to, device_id_type=MESH)


def gather_start(bufs, after, name):
    T = len(bufs)

    def body(*refs):
        ins, send_sems, recv_sems, token = refs[:T], refs[T + 1], refs[T + 2], refs[-1]
        x, y, c = _place()
        for j, chip in enumerate(_other_chips(x, y)):
            for t in range(T):
                _gather_copy(ins, t, j, _my_chip(), (*chip, c), send_sems, recv_sems, c).start()
        token[...] = jnp.zeros(token.shape, token.dtype)

    sem = pltpu.SemaphoreType.DMA((3 * T,))
    res = pl.pallas_call(
        body, name=name,
        out_shape=(sem, sem, *[pltpu.HBM(b.shape, b.dtype) for b in bufs], _sds((8, LANES), F32)),
        in_specs=[_HBM] * T + [_ANY], out_specs=(_SEM, _SEM, *[_HBM] * T, pl.BlockSpec(memory_space=pltpu.VMEM)),
        input_output_aliases={t: 2 + t for t in range(T)}, compiler_params=_split_params(),
    )(*[_in_hbm(b) for b in bufs], after)
    return res[0], res[1], list(res[2:2 + T]), res[-1]


def gather_wait(bufs, send_sems, recv_sems, after, name):
    T = len(bufs)

    def body(*refs):
        ins, ssem, rsem = refs[:T], refs[T], refs[T + 1]
        x, y, c = _place()
        for j, (cx, cy) in enumerate(_other_chips(x, y)):
            for t in range(T):
                cp = _gather_copy(ins, t, j, 2 * cx + cy, (x, y, c), ssem, rsem, c)
                cp.wait_send()
                cp.wait_recv()

    return pl.pallas_call(
        body, name=name, out_shape=[pltpu.HBM(b.shape, b.dtype) for b in bufs],
        in_specs=[_HBM] * T + [_SEM, _SEM, _ANY], out_specs=[_HBM] * T,
        input_output_aliases={t: t for t in range(T)}, compiler_params=_split_params(),
    )(*bufs, send_sems, recv_sems, after)


def gather_pass(bufs, name):
    T = len(bufs)

    def body(*refs):
        ins, outs = refs[:T], refs[T:2 * T]
        send_sems, recv_sems = refs[2 * T:]
        x, y, c = _place()
        chips = _other_chips(x, y)

        def d2d(t, j, chip_of_data, which):
            return pltpu.make_async_remote_copy(
                src_ref=_half(ins[t].at[chip_of_data], which), dst_ref=_half(outs[t].at[chip_of_data], which),
                send_sem=send_sems.at[t, j], recv_sem=recv_sems.at[t, j], device_id=(x, y, 1 - c),
                device_id_type=MESH)

        passed = [d2d(t, j, 2 * cx + cy, c) for j, (cx, cy) in enumerate(chips) for t in range(T)]
        for cp in passed:
            cp.start()
        for j, (cx, cy) in enumerate(chips):
            for t in range(T):
                d2d(t, j, 2 * cx + cy, 1 - c).wait_recv()
        for cp in passed:
            cp.wait_send()

    sem = lambda: pltpu.SemaphoreType.DMA((T, 3))
    return pl.pallas_call(
        body, name=name, out_shape=[_sds(b.shape, b.dtype) for b in bufs],
        in_specs=[_ANY] * T, out_specs=[_ANY] * T, input_output_aliases={t: t for t in range(T)},
        scratch_shapes=[sem(), sem()], compiler_params=_params(),
    )(*bufs)


def _scatter_copy(sums, lands, t, j, chip_xy, c, send_sems, recv_sems):
    cx, cy = chip_xy
    return pltpu.make_async_remote_copy(
        src_ref=sums[t].at[2 * cx + cy], dst_ref=lands[t].at[j], send_sem=send_sems.at[3 * t + j],
        recv_sem=recv_sems.at[3 * t + j], device_id=(cx, cy, c), device_id_type=MESH)


def scatter_start(sums, after, name):
    T = len(sums)
    lands = [lax.empty((3,) + s.shape[1:], s.dtype) for s in sums]

    def body(*refs):
        s_in, l_in = refs[:T], refs[T:2 * T]
        send_sems, recv_sems, token = refs[2 * T + 1], refs[2 * T + 2], refs[-1]
        x, y, c = _place()
        for j, chip in enumerate(_other_chips(x, y)):
            for t in range(T):
                _scatter_copy(s_in, l_in, t, j, chip, c, send_sems, recv_sems).start()
        token[...] = jnp.zeros(token.shape, token.dtype)

    sem = pltpu.SemaphoreType.DMA((3 * T,))
    both = list(sums) + lands
    res = pl.pallas_call(
        body, name=name,
        out_shape=(sem, sem, *[pltpu.HBM(b.shape, b.dtype) for b in both], _sds((8, LANES), F32)),
        in_specs=[_HBM] * (2 * T) + [_ANY],
        out_specs=(_SEM, _SEM, *[_HBM] * (2 * T), pl.BlockSpec(memory_space=pltpu.VMEM)),
        input_output_aliases={t: 2 + t for t in range(2 * T)}, compiler_params=_split_params(),
    )(*[_in_hbm(b) for b in both], after)
    return res[0], res[1], list(res[2:2 + T]), list(res[2 + T:2 + 2 * T]), res[-1]


def scatter_wait(sums, lands, send_sems, recv_sems, after, name):
    T = len(sums)

    def body(*refs):
        s_in, l_in, ssem, rsem = refs[:T], refs[T:2 * T], refs[2 * T], refs[2 * T + 1]
        x, y, c = _place()
        for j, chip in enumerate(_other_chips(x, y)):
            for t in range(T):
                cp = _scatter_copy(s_in, l_in, t, j, chip, c, ssem, rsem)
                cp.wait_send()
                cp.wait_recv()

    both = list(sums) + list(lands)
    res = pl.pallas_call(
        body, name=name, out_shape=[pltpu.HBM(b.shape, b.dtype) for b in both],
        in_specs=[_HBM] * (2 * T) + [_SEM, _SEM, _ANY], out_specs=[_HBM] * (2 * T),
        input_output_aliases={t: t for t in range(2 * T)}, compiler_params=_split_params(),
    )(*both, send_sems, recv_sems, after)
    return list(res[:T]), list(res[T:])


def join_halves(bufs, name):
    G = len(bufs)
    layers = [(g, l) for g in range(G) for l in range(bufs[g].shape[0])]
    T = len(layers)

    def body(*refs):
        ins, outs = refs[:G], refs[G:2 * G]
        send_sems, recv_sems = refs[2 * G:]
        x, y, c = _place()
        cps = [pltpu.make_async_remote_copy(
            src_ref=_half(ins[g].at[l], c), dst_ref=_half(outs[g].at[l], c), send_sem=send_sems.at[t],
            recv_sem=recv_sems.at[t], device_id=(x, y, 1 - c), device_id_type=MESH) for t, (g, l) in enumerate(layers)]
        for cp in cps:
            cp.start()
        for cp in cps:
            cp.wait()

    return pl.pallas_call(
        body, name=name, out_shape=[_sds(b.shape, b.dtype) for b in bufs],
        in_specs=[_ANY] * G, out_specs=[_ANY] * G, input_output_aliases={g: g for g in range(G)},
        scratch_shapes=[pltpu.SemaphoreType.DMA((T,)), pltpu.SemaphoreType.DMA((T,))],
        compiler_params=_params(),
    )(*bufs)


def _my_chip():
    return 2 * lax.axis_index("x") + lax.axis_index("y")


def add_halves(full, got, name):
    nb, R, C = full.shape
    rh = R // 2
    tr = _fit(512, [rh])
    per = rh // tr

    def body(a_ref, b_ref, o_ref):
        o_ref[...] = (a_ref[...].astype(F32) + b_ref[...].astype(F32)).astype(o_ref.dtype)

    return pl.pallas_call(
        body, name=name, grid=(nb, per),
        in_specs=[pl.BlockSpec((None, tr, C), lambda b, i: (b, lax.axis_index("c") * per + i, 0)),
                  pl.BlockSpec((None, tr, C), lambda b, i: (b, i, 0))],
        out_specs=pl.BlockSpec((None, tr, C), lambda b, i: (b, i, 0)),
        out_shape=_sds((nb, rh, C), BF16),
        compiler_params=_params(("parallel", "parallel")),
    )(full, got)


def finish_sum(sums, got, stacked, n_layers, l, name):
    nb, rh, C = sums.shape
    tr = _fit(512, [rh])
    per = rh // tr

    def body(s_ref, g_ref, *rest):
        o_ref = rest[-1]
        acc = s_ref[...].astype(F32)
        for j in range(3):
            acc = acc + g_ref[j].astype(F32)
        o_ref[...] = acc

    in_specs = [pl.BlockSpec((None, tr, C), lambda i: (_my_chip(), i, 0)),
                pl.BlockSpec((3, tr, C), lambda i: (0, i, 0))]
    args = [sums, got]
    aliases = {}
    if stacked is not None:
        in_specs.append(_ANY)
        args.append(stacked)
        aliases = {2: 0}
    return pl.pallas_call(
        body, name=name, grid=(per,), in_specs=in_specs,
        out_specs=pl.BlockSpec((None, tr, C), lambda i: (l, lax.axis_index("c") * per + i, 0)),
        out_shape=_sds((n_layers, 2 * rh, C), F32), input_output_aliases=aliases,
        compiler_params=_params(("arbitrary",)),
    )(*args)


def sum_devices(gathered, name):
    n_dev, M, N = gathered.shape

    def body(g_ref, o_ref):
        acc = g_ref[0]
        for d in range(1, n_dev):
            acc = acc + g_ref[d]
        o_ref[...] = acc

    tr = 8
    return pl.pallas_call(
        body, name=name, grid=(M // tr,),
        in_specs=[pl.BlockSpec((n_dev, tr, N), lambda i: (0, i, 0))], out_specs=pl.BlockSpec((tr, N), lambda i: (i, 0)),
        out_shape=_sds((M, N), F32), compiler_params=_params(("parallel",)),
    )(gathered)


def _pack(arrays):
    flat = jnp.concatenate([a.reshape(-1).astype(F32) for a in arrays])
    unit = 8 * PACK_COLS
    pad = (-flat.shape[0]) % unit
    return jnp.pad(flat, (0, pad)).reshape(-1, PACK_COLS)


def _unpack(packed, shapes):
    flat = packed.reshape(packed.shape[:-2] + (-1,))
    out, off = [], 0
    for s in shapes:
        n = math.prod(s)
        out.append(flat[..., off:off + n].reshape(packed.shape[:-2] + tuple(s)))
        off += n
    return out


def to_bf16(w, l, name):
    _, R, C = w.shape
    tr = _fit(512, [R])

    def body(w_ref, o_ref):
        o_ref[...] = w_ref[...].astype(BF16)

    return pl.pallas_call(
        body, name=name, grid=(R // tr,),
        in_specs=[pl.BlockSpec((None, tr, C), lambda i: (l, i, 0))],
        out_specs=pl.BlockSpec((None, tr, C), lambda i: (_my_chip(), i, 0)),
        out_shape=_sds((4, R, C), BF16),
        compiler_params=_params(("parallel",)),
    )(w)


def kernel(x, c, ada_w, ada_b, norm_g, mlp_w1, mlp_w2, gmlp_w_in, gmlp_ln_g, gmlp_ln_b, gmlp_ws, gmlp_bs, gmlp_w_out, kv_norm_g, kv_ada_w, kv_ada_b, w_kv, k_norm_g, w_f, b_f, attn_wq, q_norm_g, attn_wo, loss_target, m_ada_w, m_ada_b, m_norm_g, m_mlp_w1, m_mlp_w2, m_gmlp_w_in, m_gmlp_ln_g, m_gmlp_ln_b, m_gmlp_ws, m_gmlp_bs, m_gmlp_w_out, m_kv_norm_g, m_kv_ada_w, m_kv_ada_b, m_w_kv, m_k_norm_g, m_w_f, m_b_f, m_attn_wq, m_q_norm_g, m_attn_wo, v_ada_w, v_ada_b, v_norm_g, v_mlp_w1, v_mlp_w2, v_gmlp_w_in, v_gmlp_ln_g, v_gmlp_ln_b, v_gmlp_ws, v_gmlp_bs, v_gmlp_w_out, v_kv_norm_g, v_kv_ada_w, v_kv_ada_b, v_w_kv, v_k_norm_g, v_w_f, v_b_f, v_attn_wq, v_q_norm_g, v_attn_wo):
    given = dict(locals())
    S, D = x.shape[1], x.shape[2]
    depth = ada_w.shape[0]
    n_a = gmlp_w_in.shape[0]
    H = D // HEAD_DIM
    G = gmlp_ws.shape[1]
    half = gmlp_w_out.shape[1] * 4
    n_dev = 8
    tq = min(512, S // 4)
    ax, ay, ac = _place()
    chip = 2 * ax + ay
    me = 4 * ax + 2 * ay + ac
    row = lambda v: v.reshape(1, -1)
    x0 = x[0]
    tgt = loss_target[0]

    small_in = [c, w_f, norm_g, gmlp_ln_g, gmlp_ln_b]
    g1 = allgather_small(_pack(small_in), "gather_small_params").reshape(n_dev, -1, PACK_COLS)
    c_all, wf_all, ng_all, lg_all, lb_all = _unpack(g1, [a.shape for a in small_in])
    c_all = c_all[:, 0, :]
    per_chip = lambda a: [a[2 * j] for j in range(4)]
    w_f_full = jnp.concatenate(per_chip(wf_all), axis=0)
    norm_g_full = jnp.concatenate(per_chip(ng_all), axis=-1)
    ln_g_full = jnp.concatenate(per_chip(lg_all), axis=-1)
    ln_b_full = jnp.concatenate(per_chip(lb_all), axis=-1)
    w_f_pad = jnp.pad(w_f_full, ((0, 0), (0, LANES - H))).astype(BF16)
    b_f_pad = jnp.pad(b_f, (0, LANES - H)).reshape(1, LANES)

    sc_all = rowmap(lambda cb: cb * jax.nn.sigmoid(cb), [c_all], [], [(D, F32)], tr=8, name="silu")[0]
    n_loc = ada_w.shape[2]
    ada_b_loc = lax.dynamic_slice_in_dim(ada_b, chip * n_loc, n_loc, axis=1).reshape(1, -1)
    add_bias = lambda acc, b: (acc + b,)
    mod_loc = mm(sc_all, ada_w, outs=[(F32, None)], epilogue=add_bias, extras=[ada_b_loc], name="ada_mod")[0]
    kv_loc_n = kv_ada_w.shape[1]
    kv_b_loc = lax.dynamic_slice_in_dim(kv_ada_b, chip * kv_loc_n, kv_loc_n).reshape(1, -1)
    kvmod_loc = mm(sc_all, kv_ada_w, outs=[(F32, None)], epilogue=add_bias, extras=[kv_b_loc], name="kv_ada_mod")[0]
    g2 = allgather_small(_pack([mod_loc, kvmod_loc]), "gather_mod").reshape(n_dev, -1, PACK_COLS)
    mod_all, kvmod_all = _unpack(g2, [mod_loc.shape, kvmod_loc.shape])
    mod_me = jnp.concatenate(
        [lax.dynamic_index_in_dim(m, me, 0, keepdims=False).reshape(depth, n_loc) for m in per_chip(mod_all)], axis=1)
    kvmod_me = jnp.concatenate([lax.dynamic_index_in_dim(m, me, 0, keepdims=False) for m in per_chip(kvmod_all)])
    mods = [[row(v) for v in jnp.split(mod_me[l], N_MOD)] for l in range(depth)]
    kv_shift, kv_scale = [row(v) for v in jnp.split(kvmod_me, 2)]

    big = ['mlp_w1', 'mlp_w2', 'gmlp_w_in', 'gmlp_w_out', 'w_kv', 'attn_wq', 'attn_wo']
    stacked = {n: (given[n] if given[n].ndim == 3 else given[n][None]) for n in big}
    owner = [(n, l) for n in big for l in range(stacked[n].shape[0])]
    bufs = {o: to_bf16(stacked[o[0]], o[1], f"cast_{o[0]}_{o[1]}") for o in owner}
    W = {}

    def group(g):
        l = g // 2
        if g % 2:
            return [('mlp_w1', l), ('mlp_w2', l)]
        if l < n_a:
            return [('gmlp_w_in', l), ('gmlp_w_out', l)]
        b = l - n_a
        return ([('w_kv', 0)] if b == 0 else []) + [('attn_wq', b), ('attn_wo', b)]

    def start_gather(g, after):
        return gather_start([bufs[k] for k in group(g)], after, f"gather_start_{g}")

    def finish_gather(g, pending, after):
        send_sems, recv_sems, thru, _ = pending
        done = gather_wait(thru, send_sems, recv_sems, after, f"gather_wait_{g}")
        W.update(zip(group(g), gather_pass(done, f"gather_pass_{g}")))

    pending = start_gather(0, c)
    finish_gather(0, pending, pending[3])
    rows_of = lambda a: a.reshape(-1, a.shape[-1])
    W1 = lambda l: W['mlp_w1', l]
    W2 = lambda l: rows_of(W['mlp_w2', l])
    Win = lambda a: W['gmlp_w_in', a]
    Wout = lambda a: rows_of(W['gmlp_w_out', a])
    Wkv = lambda: W['w_kv', 0]
    Wq = lambda b: rows_of(W['attn_wq', b])
    Wo = lambda b: rows_of(W['attn_wo', b])
    bsb = jnp.broadcast_to(gmlp_bs[..., None], gmlp_bs.shape + (LANES,))

    def resid(acc, xr, gate):
        return xr + gate * acc, acc

    saved = []
    xs = x0
    kv = None
    for l in range(depth):
        sh1, sc1, gt1, sh2, sc2, gt2 = mods[l]
        ng0, ng1 = row(norm_g_full[l, 0]), row(norm_g_full[l, 1])
        pending = start_gather(2 * l + 1, W[group(2 * l)[0]])
        sh1 = sh1 + pending[3][0:1, 0:1]
        st = {"x": xs}
        h1 = modnorm_fwd(xs, ng0, sc1, sh1, f"norm1_{l}")
        st["h1"] = h1
        if l < n_a:
            t, z = mm(h1, Win(l), outs=[(BF16, None), (BF16, None)], epilogue=lambda acc: (acc, _gelu(acc)),
                      name=f"gmlp_in_{l}")
            vn = gmlp_ln_fwd(z, row(ln_g_full[l]), row(ln_b_full[l]), f"gmlp_ln_{l}")
            p = gmlp_mix_fwd(z, vn, gmlp_ws[l], bsb[l], f"gmlp_mix_{l}")
            x1, y = mm(p, Wout(l), outs=[(F32, None), (BF16, None)], epilogue=resid, extras=[xs, gt1],
                       name=f"gmlp_out_{l}")
            st.update(t=t, z=z, vn=vn, p=p, y=y)
        else:
            if kv is None:
                hk = modnorm_fwd(xs, row(kv_norm_g), kv_scale, kv_shift, "kv_norm")
                kg = row(k_norm_g)
                kp, kk = mm(hk, Wkv()[:2], outs=[(BF16, None), (BF16, None)], extras=[kg],
                            epilogue=lambda acc, g: (acc, _head_norm(acc, g)), name="kv_k")
                vv = mm(hk, Wkv()[2:], outs=[(BF16, None)], name="kv_v")[0]
                zf, logf = mm(hk, w_f_pad, outs=[(F32, None), (F32, None)], extras=[b_f_pad],
                              epilogue=lambda acc, b: (acc + b, jax.nn.log_sigmoid(acc + b)), name="kv_f")
                fc, fk = fcum_fwd(logf, tq, "fcum")
                kv = dict(x=xs, hk=hk, kp=kp, k=kk, v=vv, zf=zf, fc=fc, fk=fk)
            b = l - n_a
            qp, q = mm(h1, Wq(b), outs=[(BF16, None), (BF16, None)], extras=[row(q_norm_g[b])],
                       epilogue=lambda acc, g: (acc, _head_norm(acc, g) * (1.0 / math.sqrt(HEAD_DIM))),
                       name=f"attn_q_{l}")
            o, lse = fox_fwd(q, kv["k"], kv["v"], kv["fk"], tq, f"fox_fwd_{l}")
            x1, y = mm(o, Wo(b), outs=[(F32, None), (BF16, None)], epilogue=resid, extras=[xs, gt1],
                       name=f"attn_o_{l}")
            st.update(qp=qp, q=q, o=o, lse=lse, y=y)
        st["x1"] = x1
        finish_gather(2 * l + 1, pending, x1)
        if l + 1 < depth:
            pending = start_gather(2 * l + 2, W[group(2 * l + 1)[0]])
            sh2 = sh2 + pending[3][0:1, 0:1]
        h2 = modnorm_fwd(x1, ng1, sc2, sh2, f"norm2_{l}")
        a_pre, a_sq = mm(h2, W1(l), outs=[(BF16, None), (BF16, None)],
                         epilogue=lambda acc: (acc, jnp.square(jnp.maximum(acc, 0.0))), name=f"mlp_up_{l}")
        xs, y2 = mm(a_sq, W2(l), outs=[(F32, None), (BF16, None)], epilogue=resid, extras=[x1, gt2],
                    name=f"mlp_down_{l}")
        st.update(h2=h2, a=a_pre, a_sq=a_sq, y2=y2)
        saved.append(st)
        if l + 1 < depth:
            finish_gather(2 * l + 2, pending, xs)

    def loss_fn(yb, tb_):
        e = yb - tb_
        return e * (1.0 / D), _colsum(e * e)
    dx, sq = rowmap(loss_fn, [xs, tgt], [], [(D, F32)], reds=[(1, D)], tr=512, name="loss")
    loss = lax.psum(0.5 * jnp.sum(sq) / D, ("x", "y", "c"))

    big_grads = {n: [None] * stacked[n].shape[0] for n in big}
    dmod = [None] * depth
    d_norm_g = [[None, None] for _ in range(depth)]
    small = {}
    d_ws, d_bs, d_lg, d_lb, d_qg = [None] * n_a, [None] * n_a, [None] * n_a, [None] * n_a, [None] * (depth - n_a)
    dk_parts, dv_parts, dfk_parts = [], [], []
    quarter = lambda g: g.reshape(4, g.shape[0] // 4, g.shape[1])
    stacks = {n: None for n in big}

    def start_scatter(g, after):
        flat = [big_grads[n][i] for n, i in group(g)]
        got = exchange_halves(flat, f"grad_exchange_halves_{g}")
        sums = [add_halves(f, h, f"grad_chip_sum_{n}_{i}") for f, h, (n, i) in zip(flat, got, group(g))]
        return scatter_start(sums, after, f"grad_scatter_start_{g}")

    def finish_scatter(g, pend, after):
        send_sems, recv_sems, sums, lands, _ = pend
        sums, lands = scatter_wait(sums, lands, send_sems, recv_sems, after, f"grad_scatter_wait_{g}")
        for s, h, (n, i) in zip(sums, lands, group(g)):
            stacks[n] = finish_sum(s, h, stacks[n], stacked[n].shape[0], i, f"grad_finish_{n}_{i}")

    scattering = None
    for l in reversed(range(depth)):
        sh1, sc1, gt1, sh2, sc2, gt2 = mods[l]
        ng0, ng1 = row(norm_g_full[l, 0]), row(norm_g_full[l, 1])
        st = saved[l]
        if scattering is not None:
            gt2 = gt2 + scattering[4][0:1, 0:1]
        dy2, dgt2 = gate_bwd(dx, st["y2"], gt2, f"gate2_bwd_{l}")
        big_grads['mlp_w2'][l] = quarter(mm(st["a_sq"], dy2, ta=True, outs=[(BF16, None)], name=f"mlp_w2_grad_{l}")[0])
        da = mm(dy2, W2(l), tb=True, outs=[(BF16, None)], extras=[st["a"]],
                epilogue=lambda acc, a: (acc * (2.0 * jnp.maximum(a.astype(F32), 0.0)),), name=f"mlp_down_bwd_{l}")[0]
        big_grads['mlp_w1'][l] = mm(st["h2"], da, ta=True, outs=[(BF16, 4)], name=f"mlp_w1_grad_{l}")[0]
        dh2 = mm(da, W1(l), tb=True, outs=[(BF16, None)], name=f"mlp_up_bwd_{l}")[0]
        if scattering is not None:
            finish_scatter(2 * l + 2, scattering, dh2)
        scattering = start_scatter(2 * l + 1, dh2)
        sc2 = sc2 + scattering[4][0:1, 0:1]
        dx, dsh2, d_norm_g[l][1], dsc2 = modnorm_bwd(st["x1"], dh2, dx, ng1, sc2, f"norm2_bwd_{l}")
        dy, dgt1 = gate_bwd(dx, st["y"], gt1, f"gate1_bwd_{l}")
        if l < n_a:
            big_grads['gmlp_w_out'][l] = quarter(
                mm(st["p"], dy, ta=True, outs=[(BF16, None)], name=f"gmlp_w_out_grad_{l}")[0])
            dp = mm(dy, Wout(l), tb=True, outs=[(BF16, None)], name=f"gmlp_out_bwd_{l}")[0]
            du, dvn, d_ws[l], db = gmlp_mix_bwd(dp, st["z"], st["vn"], gmlp_ws[l], bsb[l], f"gmlp_mix_bwd_{l}")
            d_bs[l] = db[:, :, 0]
            dt, d_lg[l], d_lb[l] = gmlp_act_bwd(st["t"], du, dvn, row(ln_g_full[l]), f"gmlp_act_bwd_{l}")
            big_grads['gmlp_w_in'][l] = mm(st["h1"], dt, ta=True, outs=[(BF16, 4)], name=f"gmlp_w_in_grad_{l}")[0]
            dh1 = mm(dt, Win(l), tb=True, outs=[(BF16, None)], name=f"gmlp_in_bwd_{l}")[0]
        else:
            b = l - n_a
            big_grads['attn_wo'][b] = quarter(
                mm(st["o"], dy, ta=True, outs=[(BF16, None)], name=f"attn_wo_grad_{l}")[0])
            do = mm(dy, Wo(b), tb=True, outs=[(BF16, None)], name=f"attn_o_bwd_{l}")[0]
            dq, rowterm = fox_dq(st["q"], kv["k"], kv["v"], do, st["lse"], kv["fk"], tq, f"fox_dq_{l}")
            dk_l, dv_l, dfk_l = fox_dkv(st["q"], kv["k"], kv["v"], do, rowterm, st["lse"], kv["fk"], tq, f"fox_dkv_{l}")
            dk_parts.append(dk_l)
            dv_parts.append(dv_l)
            dfk_parts.append(jnp.pad(dfk_l[:, :, 0, :].reshape(H, S), ((0, LANES - H), (0, 0))))
            dqp, d_qg[b] = head_norm_bwd(st["qp"], dq, row(q_norm_g[b]), f"q_norm_bwd_{l}")
            big_grads['attn_wq'][b] = quarter(
                mm(st["h1"], dqp, ta=True, outs=[(BF16, None)], name=f"attn_wq_grad_{l}")[0])
            dh1 = mm(dqp, Wq(b), tb=True, outs=[(BF16, None)], name=f"attn_q_bwd_{l}")[0]
        dx, dsh1, d_norm_g[l][0], dsc1 = modnorm_bwd(st["x"], dh1, dx, ng0, sc1, f"norm1_bwd_{l}")
        dmod[l] = jnp.concatenate([dsh1, dsc1, dgt1, dsh2, dsc2, dgt2], axis=1)
        if l == n_a:
            add2 = lambda a, b_: a.astype(F32) + b_.astype(F32)
            dk_sum = rowmap(add2, dk_parts, [], [(D, BF16)], tr=512, name="dk_sum")[0]
            dv_sum = rowmap(add2, dv_parts, [], [(D, BF16)], tr=512, name="dv_sum")[0]
            dkvp, small['k_norm_g'] = head_norm_bwd(kv["kp"], dk_sum, row(k_norm_g), "k_norm_bwd", extra=dv_sum)
            dzf, db_f = fcum_bwd(dfk_parts[0], dfk_parts[1], kv["zf"], H, "fcum_bwd")
            small['b_f'] = db_f[0, :H]
            big_grads['w_kv'][0] = mm(kv["hk"], dkvp, ta=True, outs=[(BF16, 4)], name="w_kv_grad")[0]
            small['w_f'] = mm(kv["hk"], dzf, ta=True, outs=[(F32, None)], name="w_f_grad")[0][:, :H]
            dhk_f = mm(dzf, w_f_pad, tb=True, outs=[(BF16, None)], name="kv_f_bwd")[0]
            dhk = mm(dkvp, Wkv(), tb=True, outs=[(BF16, None)], extras=[dhk_f],
                     epilogue=lambda acc, e: (acc + e.astype(F32),), name="kv_bwd")[0]
            dx, dkv_shift, small['kv_norm_g'], dkv_scale = modnorm_bwd(
                kv["x"], dhk, dx, row(kv_norm_g), kv_scale, "kv_norm_bwd")
            dkvmod = jnp.concatenate([dkv_shift, dkv_scale], axis=1)
        finish_scatter(2 * l + 1, scattering, dx)
        scattering = start_scatter(2 * l, dx) if l > 0 else None
    grad_x = dx[None]

    small['norm_g'] = jnp.stack([jnp.concatenate(p, axis=0) for p in d_norm_g])
    small['gmlp_ln_g'] = jnp.concatenate(d_lg, axis=0)
    small['gmlp_ln_b'] = jnp.concatenate(d_lb, axis=0)
    small['gmlp_ws'] = jnp.stack(d_ws)
    small['gmlp_bs'] = jnp.stack(d_bs)
    small['q_norm_g'] = jnp.concatenate(d_qg, axis=0)
    small['ada_b'] = jnp.concatenate(dmod, axis=0)
    small['kv_ada_b'] = dkvmod
    names = sorted(small)
    shapes = [small[n].shape for n in names]
    g3 = allgather_small(_pack([small[n] for n in names]), "gather_small_grads").reshape(n_dev, -1, PACK_COLS)
    summed = dict(zip(names, _unpack(sum_devices(g3, "sum_small_grads"), shapes)))
    each = dict(zip(names, _unpack(g3, shapes)))
    local_cols = lambda a, n: lax.dynamic_slice_in_dim(a, chip * n, n, axis=a.ndim - 1)
    grads = {
        'ada_b': summed['ada_b'], 'gmlp_ws': summed['gmlp_ws'], 'gmlp_bs': summed['gmlp_bs'],
        'kv_norm_g': summed['kv_norm_g'].reshape(-1), 'kv_ada_b': summed['kv_ada_b'].reshape(-1),
        'k_norm_g': summed['k_norm_g'].reshape(-1), 'b_f': summed['b_f'], 'q_norm_g': summed['q_norm_g'],
        'norm_g': local_cols(summed['norm_g'], norm_g.shape[2]),
        'gmlp_ln_g': local_cols(summed['gmlp_ln_g'], gmlp_ln_g.shape[1]),
        'gmlp_ln_b': local_cols(summed['gmlp_ln_b'], gmlp_ln_b.shape[1]),
        'w_f': lax.dynamic_slice_in_dim(summed['w_f'], chip * w_f.shape[0], w_f.shape[0], axis=0),
    }
    dmod_all = each['ada_b'].reshape(n_dev, depth, N_MOD * D)
    dm_loc = jnp.transpose(local_cols(dmod_all, n_loc), (1, 0, 2))
    dkv_loc = local_cols(each['kv_ada_b'].reshape(n_dev, 2 * D), kv_loc_n)[None]
    scattering = start_scatter(0, g3)
    dm_loc = dm_loc + scattering[4][0:1, 0:1]

    delta, new_m, new_v = {}, {}, {}
    sct = jnp.transpose(sc_all)
    grads['ada_w'], delta['ada_w'], new_m['ada_w'], new_v['ada_w'] = adamw_outer(
        ada_w, sct, dm_loc, m_ada_w, v_ada_w, "adamw_ada_w")
    r = adamw_outer(kv_ada_w[None], sct, dkv_loc, m_kv_ada_w[None], v_kv_ada_w[None], "adamw_kv_ada_w")
    grads['kv_ada_w'], delta['kv_ada_w'], new_m['kv_ada_w'], new_v['kv_ada_w'] = [a[0] for a in r]
    for n in WEIGHTS:
        if n in delta or n in big:
            continue
        grads[n] = grads[n].reshape(given[n].shape)
        delta[n], new_m[n], new_v[n] = adamw(given[n], grads[n], given["m_" + n], given["v_" + n], "adamw_" + n)

    done = sum(new_v[n][(0,) * new_v[n].ndim] for n in WEIGHTS if n not in big).reshape(1, 1)
    finish_scatter(0, scattering, done)
    joined = join_halves([stacks[n] for n in big], "grad_join")
    for n, g in zip(big, joined):
        grads[n] = g.reshape(given[n].shape)
        delta[n], new_m[n], new_v[n] = adamw(given[n], grads[n], given["m_" + n], given["v_" + n], "adamw_" + n)
    return (loss, grad_x, *[grads[n] for n in WEIGHTS], *[delta[n] for n in WEIGHTS],
            *[new_m[n] for n in WEIGHTS], *[new_v[n] for n in WEIGHTS])
```

```python
import functools
import math

import jax
import jax.numpy as jnp
from jax import lax
from jax.experimental import pallas as pl
from jax.experimental.pallas import tpu as pltpu

F32 = jnp.float32
BF16 = jnp.bfloat16
EPS = 1e-6
HEAD_DIM = 128
GMLP_BLOCK = 128
CHUNK = 64
LANES = 128
N_MOD = 6
V7X_VMEM_BYTES = 64 * 2**20
VMEM_LIMIT = V7X_VMEM_BYTES - 8 * 2**20
PACK_COLS = 1024
NEG = -1e30
MESH = pl.DeviceIdType.MESH

ADAM_LR = 0.001
ADAM_B1 = 0.9
ADAM_B2 = 0.999
ADAM_EPS = 1e-08
ADAM_WD = 0.01
ADAM_STEP = 10

WEIGHTS = ['ada_w', 'ada_b', 'norm_g', 'mlp_w1', 'mlp_w2', 'gmlp_w_in', 'gmlp_ln_g', 'gmlp_ln_b', 'gmlp_ws',
           'gmlp_bs', 'gmlp_w_out', 'kv_norm_g', 'kv_ada_w', 'kv_ada_b', 'w_kv', 'k_norm_g', 'w_f', 'b_f',
           'attn_wq', 'q_norm_g', 'attn_wo']


def _params(sem=None):
    return pltpu.CompilerParams(dimension_semantics=sem, vmem_limit_bytes=VMEM_LIMIT)


def _sds(shape, dtype):
    return jax.ShapeDtypeStruct(tuple(shape), dtype)


def _ldims(shape):
    return (shape[0], shape[1]) if len(shape) == 2 else (shape[1], shape[0] * shape[2])


def _fit(t, ns):
    n0 = min(ns)
    if n0 <= t and all(n % n0 == 0 for n in ns):
        return n0
    d = (t // LANES) * LANES
    while d > LANES and any(n % d for n in ns):
        d -= LANES
    assert all(n % d == 0 for n in ns), (t, ns)
    return d


def _blk(shape, br, bc):
    if len(shape) == 2:
        return (br, bc), (lambda r, c: (r, c))
    per = shape[2] // bc
    assert shape[2] % bc == 0, (shape, bc)
    return (None, br, bc), (lambda r, c: (c // per, r, c % per))


def mm(a, b, *, name, ta=False, tb=False, outs, epilogue=None, extras=(), tm=1024, tn=1024, tk=2048,
       precision=None):
    ar, ac = _ldims(a.shape)
    br, bc = _ldims(b.shape)
    M, K = (ac, ar) if ta else (ar, ac)
    K2, N = (bc, br) if tb else (br, bc)
    assert K == K2, (name, a.shape, b.shape)
    cons = {"m": [M], "n": [N], "k": [K]}

    def note(shape, dim):
        if len(shape) == 3:
            cons[dim].append(shape[2])

    note(a.shape, "m" if ta else "k")
    note(b.shape, "k" if tb else "n")
    out_shapes = []
    for dt, nb in outs:
        if nb is None:
            out_shapes.append(_sds((M, N), dt))
        else:
            out_shapes.append(_sds((nb, M, N // nb), dt))
            cons["n"].append(N // nb)
    for e in extras:
        note(e.shape, "n")
    tm, tn, tk = _fit(tm, cons["m"]), _fit(tn, cons["n"]), _fit(tk, cons["k"])
    gm, gn, gk = M // tm, N // tn, K // tk

    a_bs, a_ix = _blk(a.shape, tk if ta else tm, tm if ta else tk)
    b_bs, b_ix = _blk(b.shape, tn if tb else tk, tk if tb else tn)
    in_specs = [
        pl.BlockSpec(a_bs, (lambda i, j, k: a_ix(k, i)) if ta else (lambda i, j, k: a_ix(i, k))),
        pl.BlockSpec(b_bs, (lambda i, j, k: b_ix(j, k)) if tb else (lambda i, j, k: b_ix(k, j))),
    ]
    for e in extras:
        if _ldims(e.shape)[0] == 1 and M != 1:
            if e.shape[1] == N:
                in_specs.append(pl.BlockSpec((1, tn), lambda i, j, k: (0, j)))
            else:
                in_specs.append(pl.BlockSpec(e.shape, lambda i, j, k: (0, 0)))
        else:
            e_bs, e_ix = _blk(e.shape, tm, tn)
            in_specs.append(pl.BlockSpec(e_bs, functools.partial(lambda i, j, k, ix: ix(i, j), ix=e_ix)))
    out_specs = []
    for s in out_shapes:
        o_bs, o_ix = _blk(s.shape, tm, tn)
        out_specs.append(pl.BlockSpec(o_bs, functools.partial(lambda i, j, k, ix: ix(i, j), ix=o_ix)))
    n_e, n_o = len(extras), len(outs)
    dims = (((0 if ta else 1,), (1 if tb else 0,)), ((), ()))

    def body(*refs):
        a_ref, b_ref = refs[:2]
        e_refs = refs[2:2 + n_e]
        o_refs = refs[2 + n_e:2 + n_e + n_o]
        x, w = a_ref[...], b_ref[...]
        if precision is None:
            x, w = x.astype(BF16), w.astype(BF16)
        d = lax.dot_general(x, w, dims, preferred_element_type=F32, precision=precision)

        def finish(acc):
            res = epilogue(acc, *[e[...] for e in e_refs]) if epilogue is not None else (acc,)
            for o, r in zip(o_refs, res):
                o[...] = r.astype(o.dtype)

        if gk == 1:
            finish(d)
        else:
            acc_ref = refs[-1]
            k = pl.program_id(2)

            @pl.when(k == 0)
            def _():
                acc_ref[...] = d

            @pl.when(jnp.logical_and(k > 0, k < gk - 1))
            def _():
                acc_ref[...] += d

            @pl.when(k == gk - 1)
            def _():
                finish(acc_ref[...] + d)

    return pl.pallas_call(
        body, name=name, grid=(gm, gn, gk), in_specs=in_specs, out_specs=out_specs, out_shape=out_shapes,
        scratch_shapes=[pltpu.VMEM((tm, tn), F32)] if gk > 1 else [],
        compiler_params=_params(("parallel", "parallel", "arbitrary")),
    )(a, b, *extras)


def rowmap(fn, rows, vecs, outs, reds=(), *, tr, name):
    rows = [r if isinstance(r, tuple) else (r, r.shape[1], 0) for r in rows]
    S = rows[0][0].shape[0]
    tr = min(tr, S)
    assert S % tr == 0, (name, S, tr)
    n_i, n_o = len(rows) + len(vecs), len(outs)
    in_specs = [pl.BlockSpec((tr, w), functools.partial(lambda i, c: (i, c), c=c)) for _, w, c in rows]
    in_specs += [pl.BlockSpec(v.shape, functools.partial(lambda i, n: (0,) * n, n=v.ndim)) for v in vecs]
    out_shape = [_sds((S, f), dt) for f, dt in outs] + [_sds(s, F32) for s in reds]
    out_specs = [pl.BlockSpec((tr, f), lambda i: (i, 0)) for f, _ in outs]
    out_specs += [pl.BlockSpec(s, functools.partial(lambda i, n: (0,) * n, n=len(s))) for s in reds]

    def body(*refs):
        res = fn(*[r[...] for r in refs[:n_i]])
        res = res if isinstance(res, tuple) else (res,)
        for o, r in zip(refs[n_i:n_i + n_o], res[:n_o]):
            o[...] = r.astype(o.dtype)
        if reds:
            d_refs = refs[n_i + n_o:]

            @pl.when(pl.program_id(0) == 0)
            def _():
                for d in d_refs:
                    d[...] = jnp.zeros(d.shape, F32)

            for d, r in zip(d_refs, res[n_o:]):
                d[...] += r

    return pl.pallas_call(
        body, name=name, grid=(S // tr,), in_specs=in_specs, out_specs=out_specs, out_shape=out_shape,
        compiler_params=_params(("arbitrary",) if reds else ("parallel",)),
    )(*[r[0] for r in rows], *vecs)


def _gelu(t):
    return 0.5 * t * (1.0 + lax.erf(t * (1.0 / math.sqrt(2.0))))


def _gelu_and_grad(t):
    cdf = 0.5 * (1.0 + lax.erf(t * (1.0 / math.sqrt(2.0))))
    return t * cdf, cdf + t * jnp.exp(-0.5 * t * t) * (1.0 / math.sqrt(2.0 * math.pi))


def _colsum(v):
    return jnp.sum(v, axis=0, keepdims=True)


def modnorm_fwd(x, g, scale, shift, name):
    def fn(xb, gb, sc, sh):
        rstd = lax.rsqrt(jnp.mean(xb * xb, axis=-1, keepdims=True) + EPS)
        return ((xb * rstd) * gb) * (1.0 + sc) + sh
    return rowmap(fn, [x], [g, scale, shift], [(x.shape[1], BF16)], tr=512, name=name)[0]


def modnorm_bwd(x, dh, dres, g, scale, name):
    D = x.shape[1]

    def fn(xb, dhb, drb, gb, sc):
        dhb = dhb.astype(F32)
        rstd = lax.rsqrt(jnp.mean(xb * xb, axis=-1, keepdims=True) + EPS)
        xhat = xb * rstd
        a = gb * (1.0 + sc)
        dxhat = dhb * a
        dx = rstd * (dxhat - xhat * jnp.mean(dxhat * xhat, axis=-1, keepdims=True))
        da = _colsum(dhb * xhat)
        return drb + dx, _colsum(dhb), da * (1.0 + sc), da * gb
    return rowmap(fn, [x, dh, dres], [g, scale], [(D, F32)], reds=[(1, D)] * 3, tr=256, name=name)


def gate_bwd(dx, y, gate, name):
    D = dx.shape[1]

    def fn(dxb, yb, gb):
        return gb * dxb, _colsum(dxb * yb.astype(F32))
    return rowmap(fn, [dx, y], [gate], [(D, BF16)], reds=[(1, D)], tr=512, name=name)


def _head_norm(x, g):
    parts = []
    for h in range(x.shape[1] // HEAD_DIM):
        xh = x[:, h * HEAD_DIM:(h + 1) * HEAD_DIM]
        rstd = lax.rsqrt(jnp.mean(xh * xh, axis=-1, keepdims=True) + EPS)
        parts.append(xh * rstd * g)
    return jnp.concatenate(parts, axis=1)


def head_norm_bwd(xp, dy, g, name, extra=None):
    D = xp.shape[1]

    def fn(*blocks):
        xb, dyb = blocks[0].astype(F32), blocks[1].astype(F32)
        gb = blocks[-1]
        parts, dg = [], jnp.zeros((1, HEAD_DIM), F32)
        for h in range(D // HEAD_DIM):
            xh = xb[:, h * HEAD_DIM:(h + 1) * HEAD_DIM]
            dyh = dyb[:, h * HEAD_DIM:(h + 1) * HEAD_DIM]
            rstd = lax.rsqrt(jnp.mean(xh * xh, axis=-1, keepdims=True) + EPS)
            xhat = xh * rstd
            dg = dg + _colsum(dyh * xhat)
            dxhat = dyh * gb
            parts.append(rstd * (dxhat - xhat * jnp.mean(dxhat * xhat, axis=-1, keepdims=True)))
        if extra is not None:
            parts.append(blocks[2].astype(F32))
        return jnp.concatenate(parts, axis=1), dg
    rows = [xp, dy] + ([extra] if extra is not None else [])
    width = D + (extra.shape[1] if extra is not None else 0)
    return rowmap(fn, rows, [g], [(width, BF16)], reds=[(1, HEAD_DIM)], tr=512, name=name)


def gmlp_ln_fwd(z, ln_g, ln_b, name):
    half = z.shape[1] // 2

    def fn(vb, gb, bb):
        vb = vb.astype(F32)
        mu = jnp.mean(vb, axis=-1, keepdims=True)
        var = jnp.mean(jnp.square(vb - mu), axis=-1, keepdims=True)
        return ((vb - mu) * lax.rsqrt(var + EPS)) * gb + bb
    return rowmap(fn, [(z, half, 1)], [ln_g, ln_b], [(half, BF16)], tr=256, name=name)[0]


def _mix_mask():
    r = lax.broadcasted_iota(jnp.int32, (GMLP_BLOCK, GMLP_BLOCK), 0) // CHUNK
    c = lax.broadcasted_iota(jnp.int32, (GMLP_BLOCK, GMLP_BLOCK), 1) // CHUNK
    return c <= r


def gmlp_mix_fwd(z, vn, ws, bsb, name):
    S, half = vn.shape
    G = ws.shape[0]
    gd = half // G
    tb = min(512, S // 2)

    def body(u_ref, v_ref, w_ref, b_ref, p_ref):
        w = jnp.where(_mix_mask(), w_ref[...], 0.0).astype(BF16)
        bcol = b_ref[:, 0:1]
        for r in range(tb // GMLP_BLOCK):
            rs = slice(r * GMLP_BLOCK, (r + 1) * GMLP_BLOCK)
            sv = jnp.dot(w, v_ref[rs, :], preferred_element_type=F32) + bcol
            p_ref[rs, :] = (u_ref[rs, :].astype(F32) * sv).astype(p_ref.dtype)

    return pl.pallas_call(
        body, name=name, grid=(S // tb, G),
        in_specs=[pl.BlockSpec((tb, gd), lambda n, g: (n, g)), pl.BlockSpec((tb, gd), lambda n, g: (n, g)),
                  pl.BlockSpec((None, GMLP_BLOCK, GMLP_BLOCK), lambda n, g: (g, 0, 0)),
                  pl.BlockSpec((None, GMLP_BLOCK, LANES), lambda n, g: (g, 0, 0))],
        out_specs=pl.BlockSpec((tb, gd), lambda n, g: (n, g)),
        out_shape=_sds((S, half), BF16),
        compiler_params=_params(("parallel", "parallel")),
    )(z, vn, ws, bsb)


def gmlp_mix_bwd(dp, z, vn, ws, bsb, name):
    S, half = vn.shape
    G = ws.shape[0]
    gd = half // G
    tb = min(512, S // 2)

    def body(dp_ref, u_ref, v_ref, w_ref, b_ref, du_ref, dv_ref, dw_ref, db_ref):
        n = pl.program_id(1)
        mask = _mix_mask()
        w = jnp.where(mask, w_ref[...], 0.0).astype(BF16)
        bcol = b_ref[:, 0:1]
        dw = jnp.zeros((GMLP_BLOCK, GMLP_BLOCK), F32)
        db = jnp.zeros((GMLP_BLOCK, 1), F32)
        for r in range(tb // GMLP_BLOCK):
            rs = slice(r * GMLP_BLOCK, (r + 1) * GMLP_BLOCK)
            vb = v_ref[rs, :]
            dpb = dp_ref[rs, :].astype(F32)
            sv = jnp.dot(w, vb, preferred_element_type=F32) + bcol
            du_ref[rs, :] = (dpb * sv).astype(du_ref.dtype)
            dsv = dpb * u_ref[rs, :].astype(F32)
            dsv16 = dsv.astype(BF16)
            dv_ref[rs, :] = lax.dot_general(w, dsv16, (((0,), (0,)), ((), ())),
                                            preferred_element_type=F32).astype(dv_ref.dtype)
            dw = dw + lax.dot_general(dsv16, vb, (((1,), (1,)), ((), ())), preferred_element_type=F32)
            db = db + jnp.sum(dsv, axis=1, keepdims=True)
        dw = jnp.where(mask, dw, 0.0)
        db = jnp.broadcast_to(db, (GMLP_BLOCK, LANES))

        @pl.when(n == 0)
        def _():
            dw_ref[...] = dw
            db_ref[...] = db

        @pl.when(n > 0)
        def _():
            dw_ref[...] += dw
            db_ref[...] += db

    blk = pl.BlockSpec((tb, gd), lambda g, n: (n, g))
    return pl.pallas_call(
        body, name=name, grid=(G, S // tb),
        in_specs=[blk, blk, blk, pl.BlockSpec((None, GMLP_BLOCK, GMLP_BLOCK), lambda g, n: (g, 0, 0)),
                  pl.BlockSpec((None, GMLP_BLOCK, LANES), lambda g, n: (g, 0, 0))],
        out_specs=[blk, blk, pl.BlockSpec((None, GMLP_BLOCK, GMLP_BLOCK), lambda g, n: (g, 0, 0)),
                   pl.BlockSpec((None, GMLP_BLOCK, LANES), lambda g, n: (g, 0, 0))],
        out_shape=[_sds((S, half), BF16), _sds((S, half), BF16), _sds((G, GMLP_BLOCK, GMLP_BLOCK), F32),
                   _sds((G, GMLP_BLOCK, LANES), F32)],
        compiler_params=_params(("parallel", "arbitrary")),
    )(dp, z, vn, ws, bsb)


def gmlp_act_bwd(t, du, dvn, ln_g, name):
    half = du.shape[1]

    def fn(tb_, dub, dvb, gb):
        tb_ = tb_.astype(F32)
        tu, tv = tb_[:, :half], tb_[:, half:]
        dtu = dub.astype(F32) * _gelu_and_grad(tu)[1]
        v, dv_dt = _gelu_and_grad(tv)
        mu = jnp.mean(v, axis=-1, keepdims=True)
        vc = v - mu
        rstd = lax.rsqrt(jnp.mean(vc * vc, axis=-1, keepdims=True) + EPS)
        vhat = vc * rstd
        dvb = dvb.astype(F32)
        dvhat = dvb * gb
        dv = rstd * (dvhat - jnp.mean(dvhat, axis=-1, keepdims=True)
                     - vhat * jnp.mean(dvhat * vhat, axis=-1, keepdims=True))
        dtv = dv * dv_dt
        return jnp.concatenate([dtu, dtv], axis=1), _colsum(dvb * vhat), _colsum(dvb)
    return rowmap(fn, [t, du, dvn], [ln_g], [(2 * half, BF16)], reds=[(1, half)] * 2, tr=128, name=name)


def fcum_fwd(logf, tb, name):
    S = logf.shape[0]
    nb = S // tb

    def body(x_ref, fc_ref, fk_ref):
        tri = (lax.broadcasted_iota(jnp.int32, (LANES, LANES), 0)
               >= lax.broadcasted_iota(jnp.int32, (LANES, LANES), 1)).astype(F32)

        def blk(b, carry):
            off = pl.multiple_of(b * LANES, LANES)
            cs = jnp.dot(tri, x_ref[pl.ds(off, LANES), :], preferred_element_type=F32,
                         precision=lax.Precision.HIGHEST) + carry
            fc_ref[pl.ds(off, LANES), :] = cs
            return cs[LANES - 1:LANES, :]

        lax.fori_loop(0, S // LANES, blk, jnp.zeros((1, LANES), F32))
        for b in range(nb):
            fk_ref[b] = fc_ref[b * tb:(b + 1) * tb, :].T

    return pl.pallas_call(
        body, name=name, out_shape=[_sds((S, LANES), F32), _sds((nb, LANES, tb), F32)],
        compiler_params=_params(),
    )(logf)


def fcum_bwd(dfk_a, dfk_b, zf, n_heads, name):
    S = zf.shape[0]

    def body(da_ref, db_ref, zf_ref, dz_ref, dsum_ref, d_ref):
        d_ref[...] = (da_ref[...] + db_ref[...]).T
        triu = (lax.broadcasted_iota(jnp.int32, (LANES, LANES), 0)
                <= lax.broadcasted_iota(jnp.int32, (LANES, LANES), 1)).astype(F32)
        nblk = S // LANES
        live = lax.broadcasted_iota(jnp.int32, (LANES, LANES), 1) < n_heads

        def blk(r, carry):
            carry_row, tot = carry
            off = pl.multiple_of((nblk - 1 - r) * LANES, LANES)
            d_blk = d_ref[pl.ds(off, LANES), :]
            cs = jnp.dot(triu, d_blk, preferred_element_type=F32, precision=lax.Precision.HIGHEST) + carry_row
            dz = jnp.where(live, cs * jax.nn.sigmoid(-zf_ref[pl.ds(off, LANES), :]), 0.0)
            dz_ref[pl.ds(off, LANES), :] = dz.astype(dz_ref.dtype)
            return carry_row + _colsum(d_blk), tot + _colsum(dz)

        _, tot = lax.fori_loop(0, nblk, blk, (jnp.zeros((1, LANES), F32), jnp.zeros((1, LANES), F32)))
        dsum_ref[...] = tot

    return pl.pallas_call(
        body, name=name, out_shape=[_sds((S, LANES), BF16), _sds((1, LANES), F32)],
        scratch_shapes=[pltpu.VMEM((S, LANES), F32)], compiler_params=_params(),
    )(dfk_a, dfk_b, zf)


def _causal(tq):
    return (lax.broadcasted_iota(jnp.int32, (tq, tq), 1) <= lax.broadcasted_iota(jnp.int32, (tq, tq), 0))


_NT = (((1,), (1,)), ((), ()))
_TN = (((0,), (0,)), ((), ()))


def fox_fwd(q, k, v, fk, tq, name):
    S, D = q.shape
    H, nq = D // HEAD_DIM, S // tq

    def body(q_ref, k_ref, v_ref, fk_ref, o_ref, lse_ref):
        h, i = pl.program_id(0), pl.program_id(1)
        qb = q_ref[...]
        hs = h % 8

        def step(j, carry, masked):
            m, l, acc = carry
            off = pl.multiple_of(j * tq, tq)
            kb, vb = k_ref[pl.ds(off, tq), :], v_ref[pl.ds(off, tq), :]
            s = lax.dot_general(qb, kb, _NT, preferred_element_type=F32) - fk_ref[j, pl.ds(hs, 1), :]
            if masked:
                s = jnp.where(_causal(tq), s, NEG)
            m_new = jnp.maximum(m, jnp.max(s, axis=1, keepdims=True))
            alpha = jnp.exp(m - m_new)
            p = jnp.exp(s - m_new)
            l = alpha * l + jnp.sum(p, axis=1, keepdims=True)
            acc = alpha * acc + jnp.dot(p.astype(BF16), vb, preferred_element_type=F32)
            return m_new, l, acc

        init = (jnp.full((tq, 1), NEG, F32), jnp.zeros((tq, 1), F32), jnp.zeros((tq, HEAD_DIM), F32))
        carry = lax.fori_loop(0, i, lambda j, c: step(j, c, False), init)
        m, l, acc = step(i, carry, True)
        o_ref[...] = (acc / l).astype(o_ref.dtype)
        lse_ref[...] = jnp.broadcast_to(m + jnp.log(l), (tq, LANES))

    return pl.pallas_call(
        body, name=name, grid=(H, nq),
        in_specs=[pl.BlockSpec((tq, HEAD_DIM), lambda h, i: (i, h)),
                  pl.BlockSpec((S, HEAD_DIM), lambda h, i: (0, h)),
                  pl.BlockSpec((S, HEAD_DIM), lambda h, i: (0, h)),
                  pl.BlockSpec((nq, 8, tq), lambda h, i: (0, h // 8, 0))],
        out_specs=[pl.BlockSpec((tq, HEAD_DIM), lambda h, i: (i, h)),
                   pl.BlockSpec((None, tq, LANES), lambda h, i: (h, i, 0))],
        out_shape=[_sds((S, D), BF16), _sds((H, S, LANES), F32)],
        compiler_params=_params(("parallel", "arbitrary")),
    )(q, k, v, fk)


def fox_dq(q, k, v, do, lse, fk, tq, name):
    S, D = q.shape
    H, nq = D // HEAD_DIM, S // tq
    scale = 1.0 / math.sqrt(HEAD_DIM)

    def body(q_ref, k_ref, v_ref, do_ref, lse_ref, fk_ref, dq_ref, row_ref):
        h, i = pl.program_id(0), pl.program_id(1)
        qb, dob = q_ref[...], do_ref[...]
        lsec = lse_ref[:, 0:1]
        hs = h % 8

        def p_dp(j, masked):
            off = pl.multiple_of(j * tq, tq)
            kb, vb = k_ref[pl.ds(off, tq), :], v_ref[pl.ds(off, tq), :]
            s = lax.dot_general(qb, kb, _NT, preferred_element_type=F32) - fk_ref[j, pl.ds(hs, 1), :]
            if masked:
                s = jnp.where(_causal(tq), s, NEG)
            return jnp.exp(s - lsec), lax.dot_general(dob, vb, _NT, preferred_element_type=F32), kb

        def sums(j, carry, masked):
            p, dp, _ = p_dp(j, masked)
            return carry[0] + jnp.sum(p * dp, axis=1, keepdims=True), carry[1] + jnp.sum(p, axis=1, keepdims=True)

        zero = jnp.zeros((tq, 1), F32)
        carry = lax.fori_loop(0, i, lambda j, c: sums(j, c, False), (zero, zero))
        num, den = sums(i, carry, True)
        rowterm = num / den

        def step(j, acc, masked):
            p, dp, kb = p_dp(j, masked)
            ds = p * (dp - rowterm)
            return acc + jnp.dot(ds.astype(BF16), kb, preferred_element_type=F32)

        acc = lax.fori_loop(0, i, lambda j, c: step(j, c, False), jnp.zeros((tq, HEAD_DIM), F32))
        acc = step(i, acc, True)
        dq_ref[...] = (acc * scale).astype(dq_ref.dtype)
        row_ref[...] = jnp.broadcast_to(rowterm, (tq, LANES))

    tile = pl.BlockSpec((tq, HEAD_DIM), lambda h, i: (i, h))
    full = pl.BlockSpec((S, HEAD_DIM), lambda h, i: (0, h))
    stat = pl.BlockSpec((None, tq, LANES), lambda h, i: (h, i, 0))
    return pl.pallas_call(
        body, name=name, grid=(H, nq),
        in_specs=[tile, full, full, tile, stat, pl.BlockSpec((nq, 8, tq), lambda h, i: (0, h // 8, 0))],
        out_specs=[tile, stat], out_shape=[_sds((S, D), BF16), _sds((H, S, LANES), F32)],
        compiler_params=_params(("parallel", "arbitrary")),
    )(q, k, v, do, lse, fk)


def fox_dkv(q, k, v, do, rowterm, lse, fk, tq, name):
    S, D = q.shape
    H, nq = D // HEAD_DIM, S // tq

    def body(q_ref, k_ref, v_ref, do_ref, row_ref, lse_ref, fk_ref, dk_ref, dv_ref, dfk_ref):
        h, j = pl.program_id(0), pl.program_id(1)
        kb, vb = k_ref[...], v_ref[...]
        fkr = fk_ref[pl.ds(h % 8, 1), :]

        def step(i, carry, masked):
            dk, dv, dfk = carry
            off = pl.multiple_of(i * tq, tq)
            qb, dob = q_ref[pl.ds(off, tq), :], do_ref[pl.ds(off, tq), :]
            delta = row_ref[pl.ds(off, tq), 0:1]
            lsec = lse_ref[pl.ds(off, tq), 0:1]
            s = lax.dot_general(qb, kb, _NT, preferred_element_type=F32) - fkr
            if masked:
                s = jnp.where(_causal(tq), s, NEG)
            p = jnp.exp(s - lsec)
            dv = dv + lax.dot_general(p.astype(BF16), dob, _TN, preferred_element_type=F32)
            dp = lax.dot_general(dob, vb, _NT, preferred_element_type=F32)
            ds = p * (dp - delta)
            dk = dk + lax.dot_general(ds.astype(BF16), qb, _TN, preferred_element_type=F32)
            return dk, dv, dfk - _colsum(ds)

        init = (jnp.zeros((tq, HEAD_DIM), F32), jnp.zeros((tq, HEAD_DIM), F32), jnp.zeros((1, tq), F32))
        carry = step(j, init, True)
        dk, dv, dfk = lax.fori_loop(j + 1, nq, lambda i, c: step(i, c, False), carry)
        dk_ref[...] = dk.astype(dk_ref.dtype)
        dv_ref[...] = dv.astype(dv_ref.dtype)
        dfk_ref[...] = jnp.broadcast_to(dfk, (8, tq))

    tile = pl.BlockSpec((tq, HEAD_DIM), lambda h, j: (j, h))
    full = pl.BlockSpec((S, HEAD_DIM), lambda h, j: (0, h))
    stat = pl.BlockSpec((None, S, LANES), lambda h, j: (h, 0, 0))
    return pl.pallas_call(
        body, name=name, grid=(H, nq),
        in_specs=[full, tile, tile, full, stat, stat, pl.BlockSpec((None, 8, tq), lambda h, j: (j, h // 8, 0))],
        out_specs=[tile, tile, pl.BlockSpec((None, None, 8, tq), lambda h, j: (h, j, 0, 0))],
        out_shape=[_sds((S, D), BF16), _sds((S, D), BF16), _sds((H, nq, 8, tq), F32)],
        compiler_params=_params(("parallel", "arbitrary")),
    )(q, k, v, do, rowterm, lse, fk)


def _adamw_math(w, g, m, v):
    m = ADAM_B1 * m + (1.0 - ADAM_B1) * g
    v = ADAM_B2 * v + (1.0 - ADAM_B2) * jnp.square(g)
    m_hat = m / (1.0 - ADAM_B1 ** ADAM_STEP)
    v_hat = v / (1.0 - ADAM_B2 ** ADAM_STEP)
    delta = -ADAM_LR * (m_hat / (jnp.sqrt(v_hat) + ADAM_EPS) + ADAM_WD * w)
    return delta, m, v


def adamw(w, g, m, v, name, with_grad=False):
    shape = w.shape
    cols = shape[-1]
    two_d = lambda a: a.reshape(-1, cols)
    rows = max(1, w.size // cols)
    tr = rows if rows * cols * 4 <= 2**21 else max(8, (2**21 // (cols * 4)) // 8 * 8)
    while rows % tr:
        tr -= 8
    fn = (lambda wb, gb, mb, vb: (*_adamw_math(wb, gb, mb, vb), gb)) if with_grad else _adamw_math
    res = rowmap(fn, [two_d(w), two_d(g), two_d(m), two_d(v)], [], [(cols, F32)] * (3 + with_grad), tr=tr, name=name)
    return [r.reshape(shape) for r in res]


def adamw_outer(w, sct, dm, m, v, name):
    L, R, C = w.shape
    B = sct.shape[1]
    tr = min(R, 256)

    def body(w_ref, s_ref, d_ref, m_ref, v_ref, g_out, dl_out, m_out, v_out):
        g = jnp.dot(s_ref[...], d_ref[...], preferred_element_type=F32, precision=lax.Precision.HIGHEST)
        delta, mn, vn = _adamw_math(w_ref[...], g, m_ref[...], v_ref[...])
        g_out[...] = g
        dl_out[...] = delta
        m_out[...] = mn
        v_out[...] = vn

    big = pl.BlockSpec((None, tr, C), lambda l, i: (l, i, 0))
    return pl.pallas_call(
        body, name=name, grid=(L, R // tr),
        in_specs=[big, pl.BlockSpec((tr, B), lambda l, i: (i, 0)), pl.BlockSpec((None, B, C), lambda l, i: (l, 0, 0)),
                  big, big],
        out_specs=[big] * 4, out_shape=[_sds((L, R, C), F32)] * 4,
        compiler_params=_params(("parallel", "parallel")),
    )(w, sct, dm, m, v)


def _place():
    x, y, c = lax.axis_index("x"), lax.axis_index("y"), lax.axis_index("c")
    return x, y, c


def _other_chips(x, y):
    return [(1 - x, y), (x, 1 - y), (1 - x, 1 - y)]


def allgather_small(block, name):
    m_per, n = block.shape

    def body(x_ref, out_ref, send_sems, recv_sems, local_sem):
        x, y, c = _place()
        me, sibling = (x, y, c), (x, y, 1 - c)
        chips = _other_chips(x, y)

        def rows(px, py, pc):
            return out_ref.at[pl.ds((4 * px + 2 * py + pc) * m_per, m_per), :]

        def copy(k, block_of, to, src=None):
            return pltpu.make_async_remote_copy(
                src_ref=rows(*block_of) if src is None else src, dst_ref=rows(*block_of),
                send_sem=send_sems.at[k], recv_sem=recv_sems.at[k], device_id=to, device_id_type=MESH)

        mine = pltpu.make_async_copy(x_ref, rows(*me), local_sem)
        mine.start()
        first = [copy(0, me, sibling, src=x_ref)]
        first += [copy(1 + j, me, (*chip, c), src=x_ref) for j, chip in enumerate(chips)]
        for cp in first:
            cp.start()
        passed = [copy(4 + j, (*chip, c), sibling) for j, chip in enumerate(chips)]
        for j, chip in enumerate(chips):
            copy(1 + j, (*chip, c), me).wait_recv()
            passed[j].start()
        copy(0, sibling, me).wait_recv()
        for j, chip in enumerate(chips):
            copy(4 + j, (*chip, 1 - c), me).wait_recv()
        for cp in first + passed:
            cp.wait_send()
        mine.wait()

    return pl.pallas_call(
        body, name=name, out_shape=_sds((8 * m_per, n), block.dtype),
        in_specs=[pl.BlockSpec(memory_space=pltpu.VMEM)], out_specs=pl.BlockSpec(memory_space=pltpu.VMEM),
        scratch_shapes=[pltpu.SemaphoreType.DMA((7,)), pltpu.SemaphoreType.DMA((7,)), pltpu.SemaphoreType.DMA],
        compiler_params=_params(),
    )(block)


def _half(ref, which):
    n = ref.shape[-2] // 2
    idx = (slice(None),) * (len(ref.shape) - 2) + (pl.ds(which * n, n), slice(None))
    return ref.at[idx]


_ANY = pl.BlockSpec(memory_space=pl.ANY)


def exchange_halves(full, name):
    T = len(full)

    def body(*refs):
        ins, outs = refs[:T], refs[T:2 * T]
        send_sems, recv_sems = refs[2 * T:]
        x, y, c = _place()
        cps = [pltpu.make_async_remote_copy(
            src_ref=_half(ins[t], 1 - c), dst_ref=outs[t], send_sem=send_sems.at[t], recv_sem=recv_sems.at[t],
            device_id=(x, y, 1 - c), device_id_type=MESH) for t in range(T)]
        for cp in cps:
            cp.start()
        for cp in cps:
            cp.wait()

    return pl.pallas_call(
        body, name=name,
        out_shape=[_sds((4, f.shape[1] // 2, f.shape[2]), f.dtype) for f in full],
        in_specs=[_ANY] * T, out_specs=[_ANY] * T,
        scratch_shapes=[pltpu.SemaphoreType.DMA((T,)), pltpu.SemaphoreType.DMA((T,))],
        compiler_params=_params(),
    )(*full)


_HBM = pl.BlockSpec(memory_space=pltpu.HBM)
_SEM = pl.BlockSpec(memory_space=pltpu.SEMAPHORE)
_EFFECT = pltpu.SideEffectType.DATAFLOW_SIDE_EFFECTING


def _in_hbm(a):
    return pltpu.with_memory_space_constraint(a, pltpu.HBM)


def _split_params():
    return pltpu.CompilerParams(has_side_effects=_EFFECT, vmem_limit_bytes=VMEM_LIMIT)


def _gather_copy(bufs, t, j, chip_of_data, to, send_sems, recv_sems, c):
    return pltpu.make_async_remote_copy(
        src_ref=_half(bufs[t].at[_my_chip()], c), dst_ref=_half(bufs[t].at[chip_of_data], c),
        send_sem=send_sems.at[3 * t + j], recv_sem=recv_sems.at[3 * t + j], device_id=to, device_id_type=MESH)


def gather_start(bufs, after, name):
    T = len(bufs)

    def body(*refs):
        ins, send_sems, recv_sems, token = refs[:T], refs[T + 1], refs[T + 2], refs[-1]
        x, y, c = _place()
        for j, chip in enumerate(_other_chips(x, y)):
            for t in range(T):
                _gather_copy(ins, t, j, _my_chip(), (*chip, c), send_sems, recv_sems, c).start()
        token[...] = jnp.zeros(token.shape, token.dtype)

    sem = pltpu.SemaphoreType.DMA((3 * T,))
    res = pl.pallas_call(
        body, name=name,
        out_shape=(sem, sem, *[pltpu.HBM(b.shape, b.dtype) for b in bufs], _sds((8, LANES), F32)),
        in_specs=[_HBM] * T + [_ANY], out_specs=(_SEM, _SEM, *[_HBM] * T, pl.BlockSpec(memory_space=pltpu.VMEM)),
        input_output_aliases={t: 2 + t for t in range(T)}, compiler_params=_split_params(),
    )(*[_in_hbm(b) for b in bufs], after)
    return res[0], res[1], list(res[2:2 + T]), res[-1]


def gather_wait(bufs, send_sems, recv_sems, after, name):
    T = len(bufs)

    def body(*refs):
        ins, ssem, rsem = refs[:T], refs[T], refs[T + 1]
        x, y, c = _place()
        for j, (cx, cy) in enumerate(_other_chips(x, y)):
            for t in range(T):
                cp = _gather_copy(ins, t, j, 2 * cx + cy, (x, y, c), ssem, rsem, c)
                cp.wait_send()
                cp.wait_recv()

    return pl.pallas_call(
        body, name=name, out_shape=[pltpu.HBM(b.shape, b.dtype) for b in bufs],
        in_specs=[_HBM] * T + [_SEM, _SEM, _ANY], out_specs=[_HBM] * T,
        input_output_aliases={t: t for t in range(T)}, compiler_params=_split_params(),
    )(*bufs, send_sems, recv_sems, after)


def gather_pass(bufs, name):
    T = len(bufs)

    def body(*refs):
        ins, outs = refs[:T], refs[T:2 * T]
        send_sems, recv_sems = refs[2 * T:]
        x, y, c = _place()
        chips = _other_chips(x, y)

        def d2d(t, j, chip_of_data, which):
            return pltpu.make_async_remote_copy(
                src_ref=_half(ins[t].at[chip_of_data], which), dst_ref=_half(outs[t].at[chip_of_data], which),
                send_sem=send_sems.at[t, j], recv_sem=recv_sems.at[t, j], device_id=(x, y, 1 - c),
                device_id_type=MESH)

        passed = [d2d(t, j, 2 * cx + cy, c) for j, (cx, cy) in enumerate(chips) for t in range(T)]
        for cp in passed:
            cp.start()
        for j, (cx, cy) in enumerate(chips):
            for t in range(T):
                d2d(t, j, 2 * cx + cy, 1 - c).wait_recv()
        for cp in passed:
            cp.wait_send()

    sem = lambda: pltpu.SemaphoreType.DMA((T, 3))
    return pl.pallas_call(
        body, name=name, out_shape=[_sds(b.shape, b.dtype) for b in bufs],
        in_specs=[_ANY] * T, out_specs=[_ANY] * T, input_output_aliases={t: t for t in range(T)},
        scratch_shapes=[sem(), sem()], compiler_params=_params(),
    )(*bufs)


def _scatter_copy(sums, lands, t, j, chip_xy, c, send_sems, recv_sems):
    cx, cy = chip_xy
    return pltpu.make_async_remote_copy(
        src_ref=sums[t].at[2 * cx + cy], dst_ref=lands[t].at[j], send_sem=send_sems.at[3 * t + j],
        recv_sem=recv_sems.at[3 * t + j], device_id=(cx, cy, c), device_id_type=MESH)


def scatter_start(sums, after, name):
    T = len(sums)
    lands = [lax.empty((3,) + s.shape[1:], s.dtype) for s in sums]

    def body(*refs):
        s_in, l_in = refs[:T], refs[T:2 * T]
        send_sems, recv_sems, token = refs[2 * T + 1], refs[2 * T + 2], refs[-1]
        x, y, c = _place()
        for j, chip in enumerate(_other_chips(x, y)):
            for t in range(T):
                _scatter_copy(s_in, l_in, t, j, chip, c, send_sems, recv_sems).start()
        token[...] = jnp.zeros(token.shape, token.dtype)

    sem = pltpu.SemaphoreType.DMA((3 * T,))
    both = list(sums) + lands
    res = pl.pallas_call(
        body, name=name,
        out_shape=(sem, sem, *[pltpu.HBM(b.shape, b.dtype) for b in both], _sds((8, LANES), F32)),
        in_specs=[_HBM] * (2 * T) + [_ANY],
        out_specs=(_SEM, _SEM, *[_HBM] * (2 * T), pl.BlockSpec(memory_space=pltpu.VMEM)),
        input_output_aliases={t: 2 + t for t in range(2 * T)}, compiler_params=_split_params(),
    )(*[_in_hbm(b) for b in both], after)
    return res[0], res[1], list(res[2:2 + T]), list(res[2 + T:2 + 2 * T]), res[-1]


def scatter_wait(sums, lands, send_sems, recv_sems, after, name):
    T = len(sums)

    def body(*refs):
        s_in, l_in, ssem, rsem = refs[:T], refs[T:2 * T], refs[2 * T], refs[2 * T + 1]
        x, y, c = _place()
        for j, chip in enumerate(_other_chips(x, y)):
            for t in range(T):
                cp = _scatter_copy(s_in, l_in, t, j, chip, c, ssem, rsem)
                cp.wait_send()
                cp.wait_recv()

    both = list(sums) + list(lands)
    res = pl.pallas_call(
        body, name=name, out_shape=[pltpu.HBM(b.shape, b.dtype) for b in both],
        in_specs=[_HBM] * (2 * T) + [_SEM, _SEM, _ANY], out_specs=[_HBM] * (2 * T),
        input_output_aliases={t: t for t in range(2 * T)}, compiler_params=_split_params(),
    )(*both, send_sems, recv_sems, after)
    return list(res[:T]), list(res[T:])


def join_halves(bufs, name):
    G = len(bufs)
    layers = [(g, l) for g in range(G) for l in range(bufs[g].shape[0])]
    T = len(layers)

    def body(*refs):
        ins, outs = refs[:G], refs[G:2 * G]
        send_sems, recv_sems = refs[2 * G:]
        x, y, c = _place()
        cps = [pltpu.make_async_remote_copy(
            src_ref=_half(ins[g].at[l], c), dst_ref=_half(outs[g].at[l], c), send_sem=send_sems.at[t],
            recv_sem=recv_sems.at[t], device_id=(x, y, 1 - c), device_id_type=MESH) for t, (g, l) in enumerate(layers)]
        for cp in cps:
            cp.start()
        for cp in cps:
            cp.wait()

    return pl.pallas_call(
        body, name=name, out_shape=[_sds(b.shape, b.dtype) for b in bufs],
        in_specs=[_ANY] * G, out_specs=[_ANY] * G, input_output_aliases={g: g for g in range(G)},
        scratch_shapes=[pltpu.SemaphoreType.DMA((T,)), pltpu.SemaphoreType.DMA((T,))],
        compiler_params=_params(),
    )(*bufs)


def _my_chip():
    return 2 * lax.axis_index("x") + lax.axis_index("y")


def add_halves(full, got, name):
    nb, R, C = full.shape
    rh = R // 2
    tr = _fit(512, [rh])
    per = rh // tr

    def body(a_ref, b_ref, o_ref):
        o_ref[...] = (a_ref[...].astype(F32) + b_ref[...].astype(F32)).astype(o_ref.dtype)

    return pl.pallas_call(
        body, name=name, grid=(nb, per),
        in_specs=[pl.BlockSpec((None, tr, C), lambda b, i: (b, lax.axis_index("c") * per + i, 0)),
                  pl.BlockSpec((None, tr, C), lambda b, i: (b, i, 0))],
        out_specs=pl.BlockSpec((None, tr, C), lambda b, i: (b, i, 0)),
        out_shape=_sds((nb, rh, C), BF16),
        compiler_params=_params(("parallel", "parallel")),
    )(full, got)


def finish_sum(sums, got, stacked, n_layers, l, name):
    nb, rh, C = sums.shape
    tr = _fit(512, [rh])
    per = rh // tr

    def body(s_ref, g_ref, *rest):
        o_ref = rest[-1]
        acc = s_ref[...].astype(F32)
        for j in range(3):
            acc = acc + g_ref[j].astype(F32)
        o_ref[...] = acc

    in_specs = [pl.BlockSpec((None, tr, C), lambda i: (_my_chip(), i, 0)),
                pl.BlockSpec((3, tr, C), lambda i: (0, i, 0))]
    args = [sums, got]
    aliases = {}
    if stacked is not None:
        in_specs.append(_ANY)
        args.append(stacked)
        aliases = {2: 0}
    return pl.pallas_call(
        body, name=name, grid=(per,), in_specs=in_specs,
        out_specs=pl.BlockSpec((None, tr, C), lambda i: (l, lax.axis_index("c") * per + i, 0)),
        out_shape=_sds((n_layers, 2 * rh, C), F32), input_output_aliases=aliases,
        compiler_params=_params(("arbitrary",)),
    )(*args)


def sum_devices(gathered, name):
    n_dev, M, N = gathered.shape

    def body(g_ref, o_ref):
        acc = g_ref[0]
        for d in range(1, n_dev):
            acc = acc + g_ref[d]
        o_ref[...] = acc

    tr = 8
    return pl.pallas_call(
        body, name=name, grid=(M // tr,),
        in_specs=[pl.BlockSpec((n_dev, tr, N), lambda i: (0, i, 0))], out_specs=pl.BlockSpec((tr, N), lambda i: (i, 0)),
        out_shape=_sds((M, N), F32), compiler_params=_params(("parallel",)),
    )(gathered)


def _pack(arrays):
    flat = jnp.concatenate([a.reshape(-1).astype(F32) for a in arrays])
    unit = 8 * PACK_COLS
    pad = (-flat.shape[0]) % unit
    return jnp.pad(flat, (0, pad)).reshape(-1, PACK_COLS)


def _unpack(packed, shapes):
    flat = packed.reshape(packed.shape[:-2] + (-1,))
    out, off = [], 0
    for s in shapes:
        n = math.prod(s)
        out.append(flat[..., off:off + n].reshape(packed.shape[:-2] + tuple(s)))
        off += n
    return out


def to_bf16(w, l, after, name):
    _, R, C = w.shape
    tr = _fit(512, [R])

    def body(w_ref, after_ref, o_ref):
        o_ref[...] = w_ref[...].astype(BF16)

    return pl.pallas_call(
        body, name=name, grid=(R // tr,),
        in_specs=[pl.BlockSpec((None, tr, C), lambda i: (l, i, 0)), _ANY],
        out_specs=pl.BlockSpec((None, tr, C), lambda i: (_my_chip(), i, 0)),
        out_shape=_sds((4, R, C), BF16),
        compiler_params=_params(("parallel",)),
    )(w, after)


def kernel(x, c, ada_w, ada_b, norm_g, mlp_w1, mlp_w2, gmlp_w_in, gmlp_ln_g, gmlp_ln_b, gmlp_ws, gmlp_bs, gmlp_w_out, kv_norm_g, kv_ada_w, kv_ada_b, w_kv, k_norm_g, w_f, b_f, attn_wq, q_norm_g, attn_wo, loss_target, m_ada_w, m_ada_b, m_norm_g, m_mlp_w1, m_mlp_w2, m_gmlp_w_in, m_gmlp_ln_g, m_gmlp_ln_b, m_gmlp_ws, m_gmlp_bs, m_gmlp_w_out, m_kv_norm_g, m_kv_ada_w, m_kv_ada_b, m_w_kv, m_k_norm_g, m_w_f, m_b_f, m_attn_wq, m_q_norm_g, m_attn_wo, v_ada_w, v_ada_b, v_norm_g, v_mlp_w1, v_mlp_w2, v_gmlp_w_in, v_gmlp_ln_g, v_gmlp_ln_b, v_gmlp_ws, v_gmlp_bs, v_gmlp_w_out, v_kv_norm_g, v_kv_ada_w, v_kv_ada_b, v_w_kv, v_k_norm_g, v_w_f, v_b_f, v_attn_wq, v_q_norm_g, v_attn_wo):
    given = dict(locals())
    S, D = x.shape[1], x.shape[2]
    depth = ada_w.shape[0]
    n_a = gmlp_w_in.shape[0]
    H = D // HEAD_DIM
    G = gmlp_ws.shape[1]
    half = gmlp_w_out.shape[1] * 4
    n_dev = 8
    tq = min(512, S // 4)
    ax, ay, ac = _place()
    chip = 2 * ax + ay
    me = 4 * ax + 2 * ay + ac
    row = lambda v: v.reshape(1, -1)
    x0 = x[0]
    tgt = loss_target[0]

    small_in = [c, w_f, norm_g, gmlp_ln_g, gmlp_ln_b]
    g1 = allgather_small(_pack(small_in), "gather_small_params").reshape(n_dev, -1, PACK_COLS)
    c_all, wf_all, ng_all, lg_all, lb_all = _unpack(g1, [a.shape for a in small_in])
    c_all = c_all[:, 0, :]
    per_chip = lambda a: [a[2 * j] for j in range(4)]
    w_f_full = jnp.concatenate(per_chip(wf_all), axis=0)
    norm_g_full = jnp.concatenate(per_chip(ng_all), axis=-1)
    ln_g_full = jnp.concatenate(per_chip(lg_all), axis=-1)
    ln_b_full = jnp.concatenate(per_chip(lb_all), axis=-1)
    w_f_pad = jnp.pad(w_f_full, ((0, 0), (0, LANES - H))).astype(BF16)
    b_f_pad = jnp.pad(b_f, (0, LANES - H)).reshape(1, LANES)

    sc_all = rowmap(lambda cb: cb * jax.nn.sigmoid(cb), [c_all], [], [(D, F32)], tr=8, name="silu")[0]
    n_loc = ada_w.shape[2]
    ada_b_loc = lax.dynamic_slice_in_dim(ada_b, chip * n_loc, n_loc, axis=1).reshape(1, -1)
    add_bias = lambda acc, b: (acc + b,)
    mod_loc = mm(sc_all, ada_w, outs=[(F32, None)], epilogue=add_bias, extras=[ada_b_loc], name="ada_mod")[0]
    kv_loc_n = kv_ada_w.shape[1]
    kv_b_loc = lax.dynamic_slice_in_dim(kv_ada_b, chip * kv_loc_n, kv_loc_n).reshape(1, -1)
    kvmod_loc = mm(sc_all, kv_ada_w, outs=[(F32, None)], epilogue=add_bias, extras=[kv_b_loc], name="kv_ada_mod")[0]
    g2 = allgather_small(_pack([mod_loc, kvmod_loc]), "gather_mod").reshape(n_dev, -1, PACK_COLS)
    mod_all, kvmod_all = _unpack(g2, [mod_loc.shape, kvmod_loc.shape])
    mod_me = jnp.concatenate(
        [lax.dynamic_index_in_dim(m, me, 0, keepdims=False).reshape(depth, n_loc) for m in per_chip(mod_all)], axis=1)
    kvmod_me = jnp.concatenate([lax.dynamic_index_in_dim(m, me, 0, keepdims=False) for m in per_chip(kvmod_all)])
    mods = [[row(v) for v in jnp.split(mod_me[l], N_MOD)] for l in range(depth)]
    kv_shift, kv_scale = [row(v) for v in jnp.split(kvmod_me, 2)]

    big = ['mlp_w1', 'mlp_w2', 'gmlp_w_in', 'gmlp_w_out', 'w_kv', 'attn_wq', 'attn_wo']
    stacked = {n: (given[n] if given[n].ndim == 3 else given[n][None]) for n in big}
    owner = [(n, l) for n in big for l in range(stacked[n].shape[0])]
    W = {}

    def group(g):
        l = g // 2
        if g % 2:
            return [('mlp_w1', l), ('mlp_w2', l)]
        if l < n_a:
            return [('gmlp_w_in', l), ('gmlp_w_out', l)]
        b = l - n_a
        return ([('w_kv', 0)] if b == 0 else []) + [('attn_wq', b), ('attn_wo', b)]

    def start_gather(g, after):
        return gather_start([bufs[k] for k in group(g)], after, f"gather_start_{g}")

    def finish_gather(g, pending, after):
        send_sems, recv_sems, thru, _ = pending
        done = gather_wait(thru, send_sems, recv_sems, after, f"gather_wait_{g}")
        W.update(zip(group(g), gather_pass(done, f"gather_pass_{g}")))

    cast = lambda o, after: to_bf16(stacked[o[0]], o[1], after, f"cast_{o[0]}_{o[1]}")
    bufs = {o: cast(o, c) for o in group(0)}
    pending = start_gather(0, c)
    bufs.update({o: cast(o, pending[3]) for o in owner if o not in bufs})
    finish_gather(0, pending, bufs[group(1)[0]])
    rows_of = lambda a: a.reshape(-1, a.shape[-1])
    W1 = lambda l: W['mlp_w1', l]
    W2 = lambda l: rows_of(W['mlp_w2', l])
    Win = lambda a: W['gmlp_w_in', a]
    Wout = lambda a: rows_of(W['gmlp_w_out', a])
    Wkv = lambda: W['w_kv', 0]
    Wq = lambda b: rows_of(W['attn_wq', b])
    Wo = lambda b: rows_of(W['attn_wo', b])
    bsb = jnp.broadcast_to(gmlp_bs[..., None], gmlp_bs.shape + (LANES,))

    def resid(acc, xr, gate):
        return xr + gate * acc, acc

    saved = []
    xs = x0
    kv = None
    for l in range(depth):
        sh1, sc1, gt1, sh2, sc2, gt2 = mods[l]
        ng0, ng1 = row(norm_g_full[l, 0]), row(norm_g_full[l, 1])
        pending = start_gather(2 * l + 1, W[group(2 * l)[0]])
        sh1 = sh1 + pending[3][0:1, 0:1]
        st = {"x": xs}
        h1 = modnorm_fwd(xs, ng0, sc1, sh1, f"norm1_{l}")
        st["h1"] = h1
        if l < n_a:
            t, z = mm(h1, Win(l), outs=[(BF16, None), (BF16, None)], epilogue=lambda acc: (acc, _gelu(acc)),
                      name=f"gmlp_in_{l}")
            vn = gmlp_ln_fwd(z, row(ln_g_full[l]), row(ln_b_full[l]), f"gmlp_ln_{l}")
            p = gmlp_mix_fwd(z, vn, gmlp_ws[l], bsb[l], f"gmlp_mix_{l}")
            x1, y = mm(p, Wout(l), outs=[(F32, None), (BF16, None)], epilogue=resid, extras=[xs, gt1],
                       name=f"gmlp_out_{l}")
            st.update(t=t, z=z, vn=vn, p=p, y=y)
        else:
            if kv is None:
                hk = modnorm_fwd(xs, row(kv_norm_g), kv_scale, kv_shift, "kv_norm")
                kg = row(k_norm_g)
                kp, kk = mm(hk, Wkv()[:2], outs=[(BF16, None), (BF16, None)], extras=[kg],
                            epilogue=lambda acc, g: (acc, _head_norm(acc, g)), name="kv_k")
                vv = mm(hk, Wkv()[2:], outs=[(BF16, None)], name="kv_v")[0]
                zf, logf = mm(hk, w_f_pad, outs=[(F32, None), (F32, None)], extras=[b_f_pad],
                              epilogue=lambda acc, b: (acc + b, jax.nn.log_sigmoid(acc + b)), name="kv_f")
                fc, fk = fcum_fwd(logf, tq, "fcum")
                kv = dict(x=xs, hk=hk, kp=kp, k=kk, v=vv, zf=zf, fc=fc, fk=fk)
            b = l - n_a
            qp, q = mm(h1, Wq(b), outs=[(BF16, None), (BF16, None)], extras=[row(q_norm_g[b])],
                       epilogue=lambda acc, g: (acc, _head_norm(acc, g) * (1.0 / math.sqrt(HEAD_DIM))),
                       name=f"attn_q_{l}")
            o, lse = fox_fwd(q, kv["k"], kv["v"], kv["fk"], tq, f"fox_fwd_{l}")
            x1, y = mm(o, Wo(b), outs=[(F32, None), (BF16, None)], epilogue=resid, extras=[xs, gt1],
                       name=f"attn_o_{l}")
            st.update(qp=qp, q=q, o=o, lse=lse, y=y)
        st["x1"] = x1
        finish_gather(2 * l + 1, pending, x1)
        if l + 1 < depth:
            pending = start_gather(2 * l + 2, W[group(2 * l + 1)[0]])
            sh2 = sh2 + pending[3][0:1, 0:1]
        h2 = modnorm_fwd(x1, ng1, sc2, sh2, f"norm2_{l}")
        a_pre, a_sq = mm(h2, W1(l), outs=[(BF16, None), (BF16, None)],
                         epilogue=lambda acc: (acc, jnp.square(jnp.maximum(acc, 0.0))), name=f"mlp_up_{l}")
        xs, y2 = mm(a_sq, W2(l), outs=[(F32, None), (BF16, None)], epilogue=resid, extras=[x1, gt2],
                    name=f"mlp_down_{l}")
        st.update(h2=h2, a=a_pre, a_sq=a_sq, y2=y2)
        saved.append(st)
        if l + 1 < depth:
            finish_gather(2 * l + 2, pending, xs)

    def loss_fn(yb, tb_):
        e = yb - tb_
        return e * (1.0 / D), _colsum(e * e)
    dx, sq = rowmap(loss_fn, [xs, tgt], [], [(D, F32)], reds=[(1, D)], tr=512, name="loss")
    loss = lax.psum(0.5 * jnp.sum(sq) / D, ("x", "y", "c"))

    big_grads = {n: [None] * stacked[n].shape[0] for n in big}
    dmod = [None] * depth
    d_norm_g = [[None, None] for _ in range(depth)]
    small = {}
    d_ws, d_bs, d_lg, d_lb, d_qg = [None] * n_a, [None] * n_a, [None] * n_a, [None] * n_a, [None] * (depth - n_a)
    dk_parts, dv_parts, dfk_parts = [], [], []
    quarter = lambda g: g.reshape(4, g.shape[0] // 4, g.shape[1])
    stacks = {n: None for n in big}

    def start_scatter(g, after):
        flat = [big_grads[n][i] for n, i in group(g)]
        got = exchange_halves(flat, f"grad_exchange_halves_{g}")
        sums = [add_halves(f, h, f"grad_chip_sum_{n}_{i}") for f, h, (n, i) in zip(flat, got, group(g))]
        return scatter_start(sums, after, f"grad_scatter_start_{g}")

    def finish_scatter(g, pend, after):
        send_sems, recv_sems, sums, lands, _ = pend
        sums, lands = scatter_wait(sums, lands, send_sems, recv_sems, after, f"grad_scatter_wait_{g}")
        for s, h, (n, i) in zip(sums, lands, group(g)):
            stacks[n] = finish_sum(s, h, stacks[n], stacked[n].shape[0], i, f"grad_finish_{n}_{i}")

    scattering = None
    for l in reversed(range(depth)):
        sh1, sc1, gt1, sh2, sc2, gt2 = mods[l]
        ng0, ng1 = row(norm_g_full[l, 0]), row(norm_g_full[l, 1])
        st = saved[l]
        if scattering is not None:
            gt2 = gt2 + scattering[4][0:1, 0:1]
        dy2, dgt2 = gate_bwd(dx, st["y2"], gt2, f"gate2_bwd_{l}")
        big_grads['mlp_w2'][l] = quarter(mm(st["a_sq"], dy2, ta=True, outs=[(BF16, None)], name=f"mlp_w2_grad_{l}")[0])
        da = mm(dy2, W2(l), tb=True, outs=[(BF16, None)], extras=[st["a"]],
                epilogue=lambda acc, a: (acc * (2.0 * jnp.maximum(a.astype(F32), 0.0)),), name=f"mlp_down_bwd_{l}")[0]
        big_grads['mlp_w1'][l] = mm(st["h2"], da, ta=True, outs=[(BF16, 4)], name=f"mlp_w1_grad_{l}")[0]
        dh2 = mm(da, W1(l), tb=True, outs=[(BF16, None)], name=f"mlp_up_bwd_{l}")[0]
        if scattering is not None:
            finish_scatter(2 * l + 2, scattering, dh2)
        scattering = start_scatter(2 * l + 1, dh2)
        sc2 = sc2 + scattering[4][0:1, 0:1]
        dx, dsh2, d_norm_g[l][1], dsc2 = modnorm_bwd(st["x1"], dh2, dx, ng1, sc2, f"norm2_bwd_{l}")
        dy, dgt1 = gate_bwd(dx, st["y"], gt1, f"gate1_bwd_{l}")
        if l < n_a:
            big_grads['gmlp_w_out'][l] = quarter(
                mm(st["p"], dy, ta=True, outs=[(BF16, None)], name=f"gmlp_w_out_grad_{l}")[0])
            dp = mm(dy, Wout(l), tb=True, outs=[(BF16, None)], name=f"gmlp_out_bwd_{l}")[0]
            du, dvn, d_ws[l], db = gmlp_mix_bwd(dp, st["z"], st["vn"], gmlp_ws[l], bsb[l], f"gmlp_mix_bwd_{l}")
            d_bs[l] = db[:, :, 0]
            dt, d_lg[l], d_lb[l] = gmlp_act_bwd(st["t"], du, dvn, row(ln_g_full[l]), f"gmlp_act_bwd_{l}")
            big_grads['gmlp_w_in'][l] = mm(st["h1"], dt, ta=True, outs=[(BF16, 4)], name=f"gmlp_w_in_grad_{l}")[0]
            dh1 = mm(dt, Win(l), tb=True, outs=[(BF16, None)], name=f"gmlp_in_bwd_{l}")[0]
        else:
            b = l - n_a
            big_grads['attn_wo'][b] = quarter(
                mm(st["o"], dy, ta=True, outs=[(BF16, None)], name=f"attn_wo_grad_{l}")[0])
            do = mm(dy, Wo(b), tb=True, outs=[(BF16, None)], name=f"attn_o_bwd_{l}")[0]
            dq, rowterm = fox_dq(st["q"], kv["k"], kv["v"], do, st["lse"], kv["fk"], tq, f"fox_dq_{l}")
            dk_l, dv_l, dfk_l = fox_dkv(st["q"], kv["k"], kv["v"], do, rowterm, st["lse"], kv["fk"], tq, f"fox_dkv_{l}")
            dk_parts.append(dk_l)
            dv_parts.append(dv_l)
            dfk_parts.append(jnp.pad(dfk_l[:, :, 0, :].reshape(H, S), ((0, LANES - H), (0, 0))))
            dqp, d_qg[b] = head_norm_bwd(st["qp"], dq, row(q_norm_g[b]), f"q_norm_bwd_{l}")
            big_grads['attn_wq'][b] = quarter(
                mm(st["h1"], dqp, ta=True, outs=[(BF16, None)], name=f"attn_wq_grad_{l}")[0])
            dh1 = mm(dqp, Wq(b), tb=True, outs=[(BF16, None)], name=f"attn_q_bwd_{l}")[0]
        dx, dsh1, d_norm_g[l][0], dsc1 = modnorm_bwd(st["x"], dh1, dx, ng0, sc1, f"norm1_bwd_{l}")
        dmod[l] = jnp.concatenate([dsh1, dsc1, dgt1, dsh2, dsc2, dgt2], axis=1)
        if l == n_a:
            add2 = lambda a, b_: a.astype(F32) + b_.astype(F32)
            dk_sum = rowmap(add2, dk_parts, [], [(D, BF16)], tr=512, name="dk_sum")[0]
            dv_sum = rowmap(add2, dv_parts, [], [(D, BF16)], tr=512, name="dv_sum")[0]
            dkvp, small['k_norm_g'] = head_norm_bwd(kv["kp"], dk_sum, row(k_norm_g), "k_norm_bwd", extra=dv_sum)
            dzf, db_f = fcum_bwd(dfk_parts[0], dfk_parts[1], kv["zf"], H, "fcum_bwd")
            small['b_f'] = db_f[0, :H]
            big_grads['w_kv'][0] = mm(kv["hk"], dkvp, ta=True, outs=[(BF16, 4)], name="w_kv_grad")[0]
            small['w_f'] = mm(kv["hk"], dzf, ta=True, outs=[(F32, None)], name="w_f_grad")[0][:, :H]
            dhk_f = mm(dzf, w_f_pad, tb=True, outs=[(BF16, None)], name="kv_f_bwd")[0]
            dhk = mm(dkvp, Wkv(), tb=True, outs=[(BF16, None)], extras=[dhk_f],
                     epilogue=lambda acc, e: (acc + e.astype(F32),), name="kv_bwd")[0]
            dx, dkv_shift, small['kv_norm_g'], dkv_scale = modnorm_bwd(
                kv["x"], dhk, dx, row(kv_norm_g), kv_scale, "kv_norm_bwd")
            dkvmod = jnp.concatenate([dkv_shift, dkv_scale], axis=1)
        finish_scatter(2 * l + 1, scattering, dx)
        scattering = start_scatter(2 * l, dx) if l > 0 else None
    grad_x = dx[None]

    small['norm_g'] = jnp.stack([jnp.concatenate(p, axis=0) for p in d_norm_g])
    small['gmlp_ln_g'] = jnp.concatenate(d_lg, axis=0)
    small['gmlp_ln_b'] = jnp.concatenate(d_lb, axis=0)
    small['gmlp_ws'] = jnp.stack(d_ws)
    small['gmlp_bs'] = jnp.stack(d_bs)
    small['q_norm_g'] = jnp.concatenate(d_qg, axis=0)
    small['ada_b'] = jnp.concatenate(dmod, axis=0)
    small['kv_ada_b'] = dkvmod
    names = sorted(small)
    shapes = [small[n].shape for n in names]
    g3 = allgather_small(_pack([small[n] for n in names]), "gather_small_grads").reshape(n_dev, -1, PACK_COLS)
    summed = dict(zip(names, _unpack(sum_devices(g3, "sum_small_grads"), shapes)))
    each = dict(zip(names, _unpack(g3, shapes)))
    local_cols = lambda a, n: lax.dynamic_slice_in_dim(a, chip * n, n, axis=a.ndim - 1)
    grads = {
        'ada_b': summed['ada_b'], 'gmlp_ws': summed['gmlp_ws'], 'gmlp_bs': summed['gmlp_bs'],
        'kv_norm_g': summed['kv_norm_g'].reshape(-1), 'kv_ada_b': summed['kv_ada_b'].reshape(-1),
        'k_norm_g': summed['k_norm_g'].reshape(-1), 'b_f': summed['b_f'], 'q_norm_g': summed['q_norm_g'],
        'norm_g': local_cols(summed['norm_g'], norm_g.shape[2]),
        'gmlp_ln_g': local_cols(summed['gmlp_ln_g'], gmlp_ln_g.shape[1]),
        'gmlp_ln_b': local_cols(summed['gmlp_ln_b'], gmlp_ln_b.shape[1]),
        'w_f': lax.dynamic_slice_in_dim(summed['w_f'], chip * w_f.shape[0], w_f.shape[0], axis=0),
    }
    dmod_all = each['ada_b'].reshape(n_dev, depth, N_MOD * D)
    dm_loc = jnp.transpose(local_cols(dmod_all, n_loc), (1, 0, 2))
    dkv_loc = local_cols(each['kv_ada_b'].reshape(n_dev, 2 * D), kv_loc_n)[None]
    scattering = start_scatter(0, g3)
    dm_loc = dm_loc + scattering[4][0:1, 0:1]

    delta, new_m, new_v = {}, {}, {}
    sct = jnp.transpose(sc_all)
    grads['ada_w'], delta['ada_w'], new_m['ada_w'], new_v['ada_w'] = adamw_outer(
        ada_w, sct, dm_loc, m_ada_w, v_ada_w, "adamw_ada_w")
    r = adamw_outer(kv_ada_w[None], sct, dkv_loc, m_kv_ada_w[None], v_kv_ada_w[None], "adamw_kv_ada_w")
    grads['kv_ada_w'], delta['kv_ada_w'], new_m['kv_ada_w'], new_v['kv_ada_w'] = [a[0] for a in r]
    for n in WEIGHTS:
        if n in delta or n in big:
            continue
        grads[n] = grads[n].reshape(given[n].shape)
        delta[n], new_m[n], new_v[n] = adamw(given[n], grads[n], given["m_" + n], given["v_" + n], "adamw_" + n)

    done = sum(new_v[n][(0,) * new_v[n].ndim] for n in WEIGHTS if n not in big).reshape(1, 1)
    finish_scatter(0, scattering, done)
    joined = join_halves([stacks[n] for n in big], "grad_join")
    for n, g in zip(big, joined):
        delta[n], new_m[n], new_v[n], grads[n] = adamw(
            given[n], g.reshape(given[n].shape), given["m_" + n], given["v_" + n], "adamw_" + n, with_grad=True)
    return (loss, grad_x, *[grads[n] for n in WEIGHTS], *[delta[n] for n in WEIGHTS],
            *[new_m[n] for n in WEIGHTS], *[new_v[n] for n in WEIGHTS])
```

```python
import functools
import math

import jax
import jax.numpy as jnp
from jax import lax
from jax.experimental import pallas as pl
from jax.experimental.pallas import tpu as pltpu

F32 = jnp.float32
BF16 = jnp.bfloat16
EPS = 1e-6
HEAD_DIM = 128
GMLP_BLOCK = 128
CHUNK = 64
LANES = 128
N_MOD = 6
V7X_VMEM_BYTES = 64 * 2**20
VMEM_LIMIT = V7X_VMEM_BYTES - 8 * 2**20
PACK_COLS = 1024
NEG = -1e30
MESH = pl.DeviceIdType.MESH

ADAM_LR = 0.001
ADAM_B1 = 0.9
ADAM_B2 = 0.999
ADAM_EPS = 1e-08
ADAM_WD = 0.01
ADAM_STEP = 10

WEIGHTS = ['ada_w', 'ada_b', 'norm_g', 'mlp_w1', 'mlp_w2', 'gmlp_w_in', 'gmlp_ln_g', 'gmlp_ln_b', 'gmlp_ws',
           'gmlp_bs', 'gmlp_w_out', 'kv_norm_g', 'kv_ada_w', 'kv_ada_b', 'w_kv', 'k_norm_g', 'w_f', 'b_f',
           'attn_wq', 'q_norm_g', 'attn_wo']


def _params(sem=None):
    return pltpu.CompilerParams(dimension_semantics=sem, vmem_limit_bytes=VMEM_LIMIT)


def _sds(shape, dtype):
    return jax.ShapeDtypeStruct(tuple(shape), dtype)


def _ldims(shape):
    return (shape[0], shape[1]) if len(shape) == 2 else (shape[1], shape[0] * shape[2])


def _fit(t, ns):
    n0 = min(ns)
    if n0 <= t and all(n % n0 == 0 for n in ns):
        return n0
    d = (t // LANES) * LANES
    while d > LANES and any(n % d for n in ns):
        d -= LANES
    assert all(n % d == 0 for n in ns), (t, ns)
    return d


def _blk(shape, br, bc):
    if len(shape) == 2:
        return (br, bc), (lambda r, c: (r, c))
    per = shape[2] // bc
    assert shape[2] % bc == 0, (shape, bc)
    return (None, br, bc), (lambda r, c: (c // per, r, c % per))


def mm(a, b, *, name, ta=False, tb=False, outs, epilogue=None, extras=(), tm=1024, tn=1024, tk=2048,
       precision=None):
    ar, ac = _ldims(a.shape)
    br, bc = _ldims(b.shape)
    M, K = (ac, ar) if ta else (ar, ac)
    K2, N = (bc, br) if tb else (br, bc)
    assert K == K2, (name, a.shape, b.shape)
    cons = {"m": [M], "n": [N], "k": [K]}

    def note(shape, dim):
        if len(shape) == 3:
            cons[dim].append(shape[2])

    note(a.shape, "m" if ta else "k")
    note(b.shape, "k" if tb else "n")
    out_shapes = []
    for dt, nb in outs:
        if nb is None:
            out_shapes.append(_sds((M, N), dt))
        else:
            out_shapes.append(_sds((nb, M, N // nb), dt))
            cons["n"].append(N // nb)
    for e in extras:
        note(e.shape, "n")
    tm, tn, tk = _fit(tm, cons["m"]), _fit(tn, cons["n"]), _fit(tk, cons["k"])
    gm, gn, gk = M // tm, N // tn, K // tk

    a_bs, a_ix = _blk(a.shape, tk if ta else tm, tm if ta else tk)
    b_bs, b_ix = _blk(b.shape, tn if tb else tk, tk if tb else tn)
    in_specs = [
        pl.BlockSpec(a_bs, (lambda i, j, k: a_ix(k, i)) if ta else (lambda i, j, k: a_ix(i, k))),
        pl.BlockSpec(b_bs, (lambda i, j, k: b_ix(j, k)) if tb else (lambda i, j, k: b_ix(k, j))),
    ]
    for e in extras:
        if _ldims(e.shape)[0] == 1 and M != 1:
            if e.shape[1] == N:
                in_specs.append(pl.BlockSpec((1, tn), lambda i, j, k: (0, j)))
            else:
                in_specs.append(pl.BlockSpec(e.shape, lambda i, j, k: (0, 0)))
        else:
            e_bs, e_ix = _blk(e.shape, tm, tn)
            in_specs.append(pl.BlockSpec(e_bs, functools.partial(lambda i, j, k, ix: ix(i, j), ix=e_ix)))
    out_specs = []
    for s in out_shapes:
        o_bs, o_ix = _blk(s.shape, tm, tn)
        out_specs.append(pl.BlockSpec(o_bs, functools.partial(lambda i, j, k, ix: ix(i, j), ix=o_ix)))
    n_e, n_o = len(extras), len(outs)
    dims = (((0 if ta else 1,), (1 if tb else 0,)), ((), ()))

    def body(*refs):
        a_ref, b_ref = refs[:2]
        e_refs = refs[2:2 + n_e]
        o_refs = refs[2 + n_e:2 + n_e + n_o]
        x, w = a_ref[...], b_ref[...]
        if precision is None:
            x, w = x.astype(BF16), w.astype(BF16)
        d = lax.dot_general(x, w, dims, preferred_element_type=F32, precision=precision)

        def finish(acc):
            res = epilogue(acc, *[e[...] for e in e_refs]) if epilogue is not None else (acc,)
            for o, r in zip(o_refs, res):
                o[...] = r.astype(o.dtype)

        if gk == 1:
            finish(d)
        else:
            acc_ref = refs[-1]
            k = pl.program_id(2)

            @pl.when(k == 0)
            def _():
                acc_ref[...] = d

            @pl.when(jnp.logical_and(k > 0, k < gk - 1))
            def _():
                acc_ref[...] += d

            @pl.when(k == gk - 1)
            def _():
                finish(acc_ref[...] + d)

    return pl.pallas_call(
        body, name=name, grid=(gm, gn, gk), in_specs=in_specs, out_specs=out_specs, out_shape=out_shapes,
        scratch_shapes=[pltpu.VMEM((tm, tn), F32)] if gk > 1 else [],
        compiler_params=_params(("parallel", "parallel", "arbitrary")),
    )(a, b, *extras)


def rowmap(fn, rows, vecs, outs, reds=(), *, tr, name):
    rows = [r if isinstance(r, tuple) else (r, r.shape[1], 0) for r in rows]
    S = rows[0][0].shape[0]
    tr = min(tr, S)
    assert S % tr == 0, (name, S, tr)
    n_i, n_o = len(rows) + len(vecs), len(outs)
    in_specs = [pl.BlockSpec((tr, w), functools.partial(lambda i, c: (i, c), c=c)) for _, w, c in rows]
    in_specs += [pl.BlockSpec(v.shape, functools.partial(lambda i, n: (0,) * n, n=v.ndim)) for v in vecs]
    out_shape = [_sds((S, f), dt) for f, dt in outs] + [_sds(s, F32) for s in reds]
    out_specs = [pl.BlockSpec((tr, f), lambda i: (i, 0)) for f, _ in outs]
    out_specs += [pl.BlockSpec(s, functools.partial(lambda i, n: (0,) * n, n=len(s))) for s in reds]

    def body(*refs):
        res = fn(*[r[...] for r in refs[:n_i]])
        res = res if isinstance(res, tuple) else (res,)
        for o, r in zip(refs[n_i:n_i + n_o], res[:n_o]):
            o[...] = r.astype(o.dtype)
        if reds:
            d_refs = refs[n_i + n_o:]

            @pl.when(pl.program_id(0) == 0)
            def _():
                for d in d_refs:
                    d[...] = jnp.zeros(d.shape, F32)

            for d, r in zip(d_refs, res[n_o:]):
                d[...] += r

    return pl.pallas_call(
        body, name=name, grid=(S // tr,), in_specs=in_specs, out_specs=out_specs, out_shape=out_shape,
        compiler_params=_params(("arbitrary",) if reds else ("parallel",)),
    )(*[r[0] for r in rows], *vecs)


def _gelu(t):
    return 0.5 * t * (1.0 + lax.erf(t * (1.0 / math.sqrt(2.0))))


def _gelu_and_grad(t):
    cdf = 0.5 * (1.0 + lax.erf(t * (1.0 / math.sqrt(2.0))))
    return t * cdf, cdf + t * jnp.exp(-0.5 * t * t) * (1.0 / math.sqrt(2.0 * math.pi))


def _colsum(v):
    return jnp.sum(v, axis=0, keepdims=True)


def modnorm_fwd(x, g, scale, shift, name):
    def fn(xb, gb, sc, sh):
        rstd = lax.rsqrt(jnp.mean(xb * xb, axis=-1, keepdims=True) + EPS)
        return ((xb * rstd) * gb) * (1.0 + sc) + sh
    return rowmap(fn, [x], [g, scale, shift], [(x.shape[1], BF16)], tr=512, name=name)[0]


def modnorm_bwd(x, dh, dres, g, scale, name):
    D = x.shape[1]

    def fn(xb, dhb, drb, gb, sc):
        dhb = dhb.astype(F32)
        rstd = lax.rsqrt(jnp.mean(xb * xb, axis=-1, keepdims=True) + EPS)
        xhat = xb * rstd
        a = gb * (1.0 + sc)
        dxhat = dhb * a
        dx = rstd * (dxhat - xhat * jnp.mean(dxhat * xhat, axis=-1, keepdims=True))
        da = _colsum(dhb * xhat)
        return drb + dx, _colsum(dhb), da * (1.0 + sc), da * gb
    return rowmap(fn, [x, dh, dres], [g, scale], [(D, F32)], reds=[(1, D)] * 3, tr=256, name=name)


def gate_bwd(dx, y, gate, name):
    D = dx.shape[1]

    def fn(dxb, yb, gb):
        return gb * dxb, _colsum(dxb * yb.astype(F32))
    return rowmap(fn, [dx, y], [gate], [(D, BF16)], reds=[(1, D)], tr=512, name=name)


def _head_norm(x, g):
    parts = []
    for h in range(x.shape[1] // HEAD_DIM):
        xh = x[:, h * HEAD_DIM:(h + 1) * HEAD_DIM]
        rstd = lax.rsqrt(jnp.mean(xh * xh, axis=-1, keepdims=True) + EPS)
        parts.append(xh * rstd * g)
    return jnp.concatenate(parts, axis=1)


def head_norm_bwd(xp, dy, g, name, extra=None):
    D = xp.shape[1]

    def fn(*blocks):
        xb, dyb = blocks[0].astype(F32), blocks[1].astype(F32)
        gb = blocks[-1]
        parts, dg = [], jnp.zeros((1, HEAD_DIM), F32)
        for h in range(D // HEAD_DIM):
            xh = xb[:, h * HEAD_DIM:(h + 1) * HEAD_DIM]
            dyh = dyb[:, h * HEAD_DIM:(h + 1) * HEAD_DIM]
            rstd = lax.rsqrt(jnp.mean(xh * xh, axis=-1, keepdims=True) + EPS)
            xhat = xh * rstd
            dg = dg + _colsum(dyh * xhat)
            dxhat = dyh * gb
            parts.append(rstd * (dxhat - xhat * jnp.mean(dxhat * xhat, axis=-1, keepdims=True)))
        if extra is not None:
            parts.append(blocks[2].astype(F32))
        return jnp.concatenate(parts, axis=1), dg
    rows = [xp, dy] + ([extra] if extra is not None else [])
    width = D + (extra.shape[1] if extra is not None else 0)
    return rowmap(fn, rows, [g], [(width, BF16)], reds=[(1, HEAD_DIM)], tr=512, name=name)


def gmlp_ln_fwd(z, ln_g, ln_b, name):
    half = z.shape[1] // 2

    def fn(vb, gb, bb):
        vb = vb.astype(F32)
        mu = jnp.mean(vb, axis=-1, keepdims=True)
        var = jnp.mean(jnp.square(vb - mu), axis=-1, keepdims=True)
        return ((vb - mu) * lax.rsqrt(var + EPS)) * gb + bb
    return rowmap(fn, [(z, half, 1)], [ln_g, ln_b], [(half, BF16)], tr=256, name=name)[0]


def _mix_mask():
    r = lax.broadcasted_iota(jnp.int32, (GMLP_BLOCK, GMLP_BLOCK), 0) // CHUNK
    c = lax.broadcasted_iota(jnp.int32, (GMLP_BLOCK, GMLP_BLOCK), 1) // CHUNK
    return c <= r


def gmlp_mix_fwd(z, vn, ws, bsb, name):
    S, half = vn.shape
    G = ws.shape[0]
    gd = half // G
    tb = min(512, S // 2)

    def body(u_ref, v_ref, w_ref, b_ref, p_ref):
        w = jnp.where(_mix_mask(), w_ref[...], 0.0).astype(BF16)
        bcol = b_ref[:, 0:1]
        for r in range(tb // GMLP_BLOCK):
            rs = slice(r * GMLP_BLOCK, (r + 1) * GMLP_BLOCK)
            sv = jnp.dot(w, v_ref[rs, :], preferred_element_type=F32) + bcol
            p_ref[rs, :] = (u_ref[rs, :].astype(F32) * sv).astype(p_ref.dtype)

    return pl.pallas_call(
        body, name=name, grid=(S // tb, G),
        in_specs=[pl.BlockSpec((tb, gd), lambda n, g: (n, g)), pl.BlockSpec((tb, gd), lambda n, g: (n, g)),
                  pl.BlockSpec((None, GMLP_BLOCK, GMLP_BLOCK), lambda n, g: (g, 0, 0)),
                  pl.BlockSpec((None, GMLP_BLOCK, LANES), lambda n, g: (g, 0, 0))],
        out_specs=pl.BlockSpec((tb, gd), lambda n, g: (n, g)),
        out_shape=_sds((S, half), BF16),
        compiler_params=_params(("parallel", "parallel")),
    )(z, vn, ws, bsb)


def gmlp_mix_bwd(dp, z, vn, ws, bsb, name):
    S, half = vn.shape
    G = ws.shape[0]
    gd = half // G
    tb = min(512, S // 2)

    def body(dp_ref, u_ref, v_ref, w_ref, b_ref, du_ref, dv_ref, dw_ref, db_ref):
        n = pl.program_id(1)
        mask = _mix_mask()
        w = jnp.where(mask, w_ref[...], 0.0).astype(BF16)
        bcol = b_ref[:, 0:1]
        dw = jnp.zeros((GMLP_BLOCK, GMLP_BLOCK), F32)
        db = jnp.zeros((GMLP_BLOCK, 1), F32)
        for r in range(tb // GMLP_BLOCK):
            rs = slice(r * GMLP_BLOCK, (r + 1) * GMLP_BLOCK)
            vb = v_ref[rs, :]
            dpb = dp_ref[rs, :].astype(F32)
            sv = jnp.dot(w, vb, preferred_element_type=F32) + bcol
            du_ref[rs, :] = (dpb * sv).astype(du_ref.dtype)
            dsv = dpb * u_ref[rs, :].astype(F32)
            dsv16 = dsv.astype(BF16)
            dv_ref[rs, :] = lax.dot_general(w, dsv16, (((0,), (0,)), ((), ())),
                                            preferred_element_type=F32).astype(dv_ref.dtype)
            dw = dw + lax.dot_general(dsv16, vb, (((1,), (1,)), ((), ())), preferred_element_type=F32)
            db = db + jnp.sum(dsv, axis=1, keepdims=True)
        dw = jnp.where(mask, dw, 0.0)
        db = jnp.broadcast_to(db, (GMLP_BLOCK, LANES))

        @pl.when(n == 0)
        def _():
            dw_ref[...] = dw
            db_ref[...] = db

        @pl.when(n > 0)
        def _():
            dw_ref[...] += dw
            db_ref[...] += db

    blk = pl.BlockSpec((tb, gd), lambda g, n: (n, g))
    return pl.pallas_call(
        body, name=name, grid=(G, S // tb),
        in_specs=[blk, blk, blk, pl.BlockSpec((None, GMLP_BLOCK, GMLP_BLOCK), lambda g, n: (g, 0, 0)),
                  pl.BlockSpec((None, GMLP_BLOCK, LANES), lambda g, n: (g, 0, 0))],
        out_specs=[blk, blk, pl.BlockSpec((None, GMLP_BLOCK, GMLP_BLOCK), lambda g, n: (g, 0, 0)),
                   pl.BlockSpec((None, GMLP_BLOCK, LANES), lambda g, n: (g, 0, 0))],
        out_shape=[_sds((S, half), BF16), _sds((S, half), BF16), _sds((G, GMLP_BLOCK, GMLP_BLOCK), F32),
                   _sds((G, GMLP_BLOCK, LANES), F32)],
        compiler_params=_params(("parallel", "arbitrary")),
    )(dp, z, vn, ws, bsb)


def gmlp_act_bwd(t, du, dvn, ln_g, name):
    half = du.shape[1]

    def fn(tb_, dub, dvb, gb):
        tb_ = tb_.astype(F32)
        tu, tv = tb_[:, :half], tb_[:, half:]
        dtu = dub.astype(F32) * _gelu_and_grad(tu)[1]
        v, dv_dt = _gelu_and_grad(tv)
        mu = jnp.mean(v, axis=-1, keepdims=True)
        vc = v - mu
        rstd = lax.rsqrt(jnp.mean(vc * vc, axis=-1, keepdims=True) + EPS)
        vhat = vc * rstd
        dvb = dvb.astype(F32)
        dvhat = dvb * gb
        dv = rstd * (dvhat - jnp.mean(dvhat, axis=-1, keepdims=True)
                     - vhat * jnp.mean(dvhat * vhat, axis=-1, keepdims=True))
        dtv = dv * dv_dt
        return jnp.concatenate([dtu, dtv], axis=1), _colsum(dvb * vhat), _colsum(dvb)
    return rowmap(fn, [t, du, dvn], [ln_g], [(2 * half, BF16)], reds=[(1, half)] * 2, tr=128, name=name)


def fcum_fwd(logf, tb, name):
    S = logf.shape[0]
    nb = S // tb

    def body(x_ref, fc_ref, fk_ref):
        tri = (lax.broadcasted_iota(jnp.int32, (LANES, LANES), 0)
               >= lax.broadcasted_iota(jnp.int32, (LANES, LANES), 1)).astype(F32)

        def blk(b, carry):
            off = pl.multiple_of(b * LANES, LANES)
            cs = jnp.dot(tri, x_ref[pl.ds(off, LANES), :], preferred_element_type=F32,
                         precision=lax.Precision.HIGHEST) + carry
            fc_ref[pl.ds(off, LANES), :] = cs
            return cs[LANES - 1:LANES, :]

        lax.fori_loop(0, S // LANES, blk, jnp.zeros((1, LANES), F32))
        for b in range(nb):
            fk_ref[b] = fc_ref[b * tb:(b + 1) * tb, :].T

    return pl.pallas_call(
        body, name=name, out_shape=[_sds((S, LANES), F32), _sds((nb, LANES, tb), F32)],
        compiler_params=_params(),
    )(logf)


def fcum_bwd(dfk_a, dfk_b, zf, n_heads, name):
    S = zf.shape[0]

    def body(da_ref, db_ref, zf_ref, dz_ref, dsum_ref, d_ref):
        d_ref[...] = (da_ref[...] + db_ref[...]).T
        triu = (lax.broadcasted_iota(jnp.int32, (LANES, LANES), 0)
                <= lax.broadcasted_iota(jnp.int32, (LANES, LANES), 1)).astype(F32)
        nblk = S // LANES
        live = lax.broadcasted_iota(jnp.int32, (LANES, LANES), 1) < n_heads

        def blk(r, carry):
            carry_row, tot = carry
            off = pl.multiple_of((nblk - 1 - r) * LANES, LANES)
            d_blk = d_ref[pl.ds(off, LANES), :]
            cs = jnp.dot(triu, d_blk, preferred_element_type=F32, precision=lax.Precision.HIGHEST) + carry_row
            dz = jnp.where(live, cs * jax.nn.sigmoid(-zf_ref[pl.ds(off, LANES), :]), 0.0)
            dz_ref[pl.ds(off, LANES), :] = dz.astype(dz_ref.dtype)
            return carry_row + _colsum(d_blk), tot + _colsum(dz)

        _, tot = lax.fori_loop(0, nblk, blk, (jnp.zeros((1, LANES), F32), jnp.zeros((1, LANES), F32)))
        dsum_ref[...] = tot

    return pl.pallas_call(
        body, name=name, out_shape=[_sds((S, LANES), BF16), _sds((1, LANES), F32)],
        scratch_shapes=[pltpu.VMEM((S, LANES), F32)], compiler_params=_params(),
    )(dfk_a, dfk_b, zf)


def _causal(tq):
    return (lax.broadcasted_iota(jnp.int32, (tq, tq), 1) <= lax.broadcasted_iota(jnp.int32, (tq, tq), 0))


_NT = (((1,), (1,)), ((), ()))
_TN = (((0,), (0,)), ((), ()))


def fox_fwd(q, k, v, fk, tq, name):
    S, D = q.shape
    H, nq = D // HEAD_DIM, S // tq

    def body(q_ref, k_ref, v_ref, fk_ref, o_ref, lse_ref):
        h, i = pl.program_id(0), pl.program_id(1)
        qb = q_ref[...]
        hs = h % 8

        def step(j, carry, masked):
            m, l, acc = carry
            off = pl.multiple_of(j * tq, tq)
            kb, vb = k_ref[pl.ds(off, tq), :], v_ref[pl.ds(off, tq), :]
            s = lax.dot_general(qb, kb, _NT, preferred_element_type=F32) - fk_ref[j, pl.ds(hs, 1), :]
            if masked:
                s = jnp.where(_causal(tq), s, NEG)
            m_new = jnp.maximum(m, jnp.max(s, axis=1, keepdims=True))
            alpha = jnp.exp(m - m_new)
            p = jnp.exp(s - m_new)
            l = alpha * l + jnp.sum(p, axis=1, keepdims=True)
            acc = alpha * acc + jnp.dot(p.astype(BF16), vb, preferred_element_type=F32)
            return m_new, l, acc

        init = (jnp.full((tq, 1), NEG, F32), jnp.zeros((tq, 1), F32), jnp.zeros((tq, HEAD_DIM), F32))
        carry = lax.fori_loop(0, i, lambda j, c: step(j, c, False), init)
        m, l, acc = step(i, carry, True)
        o_ref[...] = (acc / l).astype(o_ref.dtype)
        lse_ref[...] = jnp.broadcast_to(m + jnp.log(l), (tq, LANES))

    return pl.pallas_call(
        body, name=name, grid=(H, nq),
        in_specs=[pl.BlockSpec((tq, HEAD_DIM), lambda h, i: (i, h)),
                  pl.BlockSpec((S, HEAD_DIM), lambda h, i: (0, h)),
                  pl.BlockSpec((S, HEAD_DIM), lambda h, i: (0, h)),
                  pl.BlockSpec((nq, 8, tq), lambda h, i: (0, h // 8, 0))],
        out_specs=[pl.BlockSpec((tq, HEAD_DIM), lambda h, i: (i, h)),
                   pl.BlockSpec((None, tq, LANES), lambda h, i: (h, i, 0))],
        out_shape=[_sds((S, D), BF16), _sds((H, S, LANES), F32)],
        compiler_params=_params(("parallel", "arbitrary")),
    )(q, k, v, fk)


def fox_dq(q, k, v, do, lse, fk, tq, name):
    S, D = q.shape
    H, nq = D // HEAD_DIM, S // tq
    scale = 1.0 / math.sqrt(HEAD_DIM)

    def body(q_ref, k_ref, v_ref, do_ref, lse_ref, fk_ref, dq_ref, row_ref):
        h, i = pl.program_id(0), pl.program_id(1)
        qb, dob = q_ref[...], do_ref[...]
        lsec = lse_ref[:, 0:1]
        hs = h % 8

        def p_dp(j, masked):
            off = pl.multiple_of(j * tq, tq)
            kb, vb = k_ref[pl.ds(off, tq), :], v_ref[pl.ds(off, tq), :]
            s = lax.dot_general(qb, kb, _NT, preferred_element_type=F32) - fk_ref[j, pl.ds(hs, 1), :]
            if masked:
                s = jnp.where(_causal(tq), s, NEG)
            return jnp.exp(s - lsec), lax.dot_general(dob, vb, _NT, preferred_element_type=F32), kb

        def sums(j, carry, masked):
            p, dp, _ = p_dp(j, masked)
            return carry[0] + jnp.sum(p * dp, axis=1, keepdims=True), carry[1] + jnp.sum(p, axis=1, keepdims=True)

        zero = jnp.zeros((tq, 1), F32)
        carry = lax.fori_loop(0, i, lambda j, c: sums(j, c, False), (zero, zero))
        num, den = sums(i, carry, True)
        rowterm = num / den

        def step(j, acc, masked):
            p, dp, kb = p_dp(j, masked)
            ds = p * (dp - rowterm)
            return acc + jnp.dot(ds.astype(BF16), kb, preferred_element_type=F32)

        acc = lax.fori_loop(0, i, lambda j, c: step(j, c, False), jnp.zeros((tq, HEAD_DIM), F32))
        acc = step(i, acc, True)
        dq_ref[...] = (acc * scale).astype(dq_ref.dtype)
        row_ref[...] = jnp.broadcast_to(rowterm, (tq, LANES))

    tile = pl.BlockSpec((tq, HEAD_DIM), lambda h, i: (i, h))
    full = pl.BlockSpec((S, HEAD_DIM), lambda h, i: (0, h))
    stat = pl.BlockSpec((None, tq, LANES), lambda h, i: (h, i, 0))
    return pl.pallas_call(
        body, name=name, grid=(H, nq),
        in_specs=[tile, full, full, tile, stat, pl.BlockSpec((nq, 8, tq), lambda h, i: (0, h // 8, 0))],
        out_specs=[tile, stat], out_shape=[_sds((S, D), BF16), _sds((H, S, LANES), F32)],
        compiler_params=_params(("parallel", "arbitrary")),
    )(q, k, v, do, lse, fk)


def fox_dkv(q, k, v, do, rowterm, lse, fk, tq, name):
    S, D = q.shape
    H, nq = D // HEAD_DIM, S // tq

    def body(q_ref, k_ref, v_ref, do_ref, row_ref, lse_ref, fk_ref, dk_ref, dv_ref, dfk_ref):
        h, j = pl.program_id(0), pl.program_id(1)
        kb, vb = k_ref[...], v_ref[...]
        fkr = fk_ref[pl.ds(h % 8, 1), :]

        def step(i, carry, masked):
            dk, dv, dfk = carry
            off = pl.multiple_of(i * tq, tq)
            qb, dob = q_ref[pl.ds(off, tq), :], do_ref[pl.ds(off, tq), :]
            delta = row_ref[pl.ds(off, tq), 0:1]
            lsec = lse_ref[pl.ds(off, tq), 0:1]
            s = lax.dot_general(qb, kb, _NT, preferred_element_type=F32) - fkr
            if masked:
                s = jnp.where(_causal(tq), s, NEG)
            p = jnp.exp(s - lsec)
            dv = dv + lax.dot_general(p.astype(BF16), dob, _TN, preferred_element_type=F32)
            dp = lax.dot_general(dob, vb, _NT, preferred_element_type=F32)
            ds = p * (dp - delta)
            dk = dk + lax.dot_general(ds.astype(BF16), qb, _TN, preferred_element_type=F32)
            return dk, dv, dfk - _colsum(ds)

        init = (jnp.zeros((tq, HEAD_DIM), F32), jnp.zeros((tq, HEAD_DIM), F32), jnp.zeros((1, tq), F32))
        carry = step(j, init, True)
        dk, dv, dfk = lax.fori_loop(j + 1, nq, lambda i, c: step(i, c, False), carry)
        dk_ref[...] = dk.astype(dk_ref.dtype)
        dv_ref[...] = dv.astype(dv_ref.dtype)
        dfk_ref[...] = jnp.broadcast_to(dfk, (8, tq))

    tile = pl.BlockSpec((tq, HEAD_DIM), lambda h, j: (j, h))
    full = pl.BlockSpec((S, HEAD_DIM), lambda h, j: (0, h))
    stat = pl.BlockSpec((None, S, LANES), lambda h, j: (h, 0, 0))
    return pl.pallas_call(
        body, name=name, grid=(H, nq),
        in_specs=[full, tile, tile, full, stat, stat, pl.BlockSpec((None, 8, tq), lambda h, j: (j, h // 8, 0))],
        out_specs=[tile, tile, pl.BlockSpec((None, None, 8, tq), lambda h, j: (h, j, 0, 0))],
        out_shape=[_sds((S, D), BF16), _sds((S, D), BF16), _sds((H, nq, 8, tq), F32)],
        compiler_params=_params(("parallel", "arbitrary")),
    )(q, k, v, do, rowterm, lse, fk)


def _adamw_math(w, g, m, v):
    m = ADAM_B1 * m + (1.0 - ADAM_B1) * g
    v = ADAM_B2 * v + (1.0 - ADAM_B2) * jnp.square(g)
    m_hat = m / (1.0 - ADAM_B1 ** ADAM_STEP)
    v_hat = v / (1.0 - ADAM_B2 ** ADAM_STEP)
    delta = -ADAM_LR * (m_hat / (jnp.sqrt(v_hat) + ADAM_EPS) + ADAM_WD * w)
    return delta, m, v


def adamw(w, g, m, v, name, with_grad=False):
    shape = w.shape
    cols = shape[-1]
    two_d = lambda a: a.reshape(-1, cols)
    rows = max(1, w.size // cols)
    tr = rows if rows * cols * 4 <= 2**21 else max(8, (2**21 // (cols * 4)) // 8 * 8)
    while rows % tr:
        tr -= 8
    fn = (lambda wb, gb, mb, vb: (*_adamw_math(wb, gb, mb, vb), gb)) if with_grad else _adamw_math
    res = rowmap(fn, [two_d(w), two_d(g), two_d(m), two_d(v)], [], [(cols, F32)] * (3 + with_grad), tr=tr, name=name)
    return [r.reshape(shape) for r in res]


def adamw_outer(w, sct, dm, m, v, name):
    L, R, C = w.shape
    B = sct.shape[1]
    tr = min(R, 256)

    def body(w_ref, s_ref, d_ref, m_ref, v_ref, g_out, dl_out, m_out, v_out):
        g = jnp.dot(s_ref[...], d_ref[...], preferred_element_type=F32, precision=lax.Precision.HIGHEST)
        delta, mn, vn = _adamw_math(w_ref[...], g, m_ref[...], v_ref[...])
        g_out[...] = g
        dl_out[...] = delta
        m_out[...] = mn
        v_out[...] = vn

    big = pl.BlockSpec((None, tr, C), lambda l, i: (l, i, 0))
    return pl.pallas_call(
        body, name=name, grid=(L, R // tr),
        in_specs=[big, pl.BlockSpec((tr, B), lambda l, i: (i, 0)), pl.BlockSpec((None, B, C), lambda l, i: (l, 0, 0)),
                  big, big],
        out_specs=[big] * 4, out_shape=[_sds((L, R, C), F32)] * 4,
        compiler_params=_params(("parallel", "parallel")),
    )(w, sct, dm, m, v)


def _place():
    x, y, c = lax.axis_index("x"), lax.axis_index("y"), lax.axis_index("c")
    return x, y, c


def _other_chips(x, y):
    return [(1 - x, y), (x, 1 - y), (1 - x, 1 - y)]


def allgather_small(block, name):
    m_per, n = block.shape

    def body(x_ref, out_ref, send_sems, recv_sems, local_sem):
        x, y, c = _place()
        me, sibling = (x, y, c), (x, y, 1 - c)
        chips = _other_chips(x, y)

        def rows(px, py, pc):
            return out_ref.at[pl.ds((4 * px + 2 * py + pc) * m_per, m_per), :]

        def copy(k, block_of, to, src=None):
            return pltpu.make_async_remote_copy(
                src_ref=rows(*block_of) if src is None else src, dst_ref=rows(*block_of),
                send_sem=send_sems.at[k], recv_sem=recv_sems.at[k], device_id=to, device_id_type=MESH)

        mine = pltpu.make_async_copy(x_ref, rows(*me), local_sem)
        mine.start()
        first = [copy(0, me, sibling, src=x_ref)]
        first += [copy(1 + j, me, (*chip, c), src=x_ref) for j, chip in enumerate(chips)]
        for cp in first:
            cp.start()
        passed = [copy(4 + j, (*chip, c), sibling) for j, chip in enumerate(chips)]
        for j, chip in enumerate(chips):
            copy(1 + j, (*chip, c), me).wait_recv()
            passed[j].start()
        copy(0, sibling, me).wait_recv()
        for j, chip in enumerate(chips):
            copy(4 + j, (*chip, 1 - c), me).wait_recv()
        for cp in first + passed:
            cp.wait_send()
        mine.wait()

    return pl.pallas_call(
        body, name=name, out_shape=_sds((8 * m_per, n), block.dtype),
        in_specs=[pl.BlockSpec(memory_space=pltpu.VMEM)], out_specs=pl.BlockSpec(memory_space=pltpu.VMEM),
        scratch_shapes=[pltpu.SemaphoreType.DMA((7,)), pltpu.SemaphoreType.DMA((7,)), pltpu.SemaphoreType.DMA],
        compiler_params=_params(),
    )(block)


def _half(ref, which):
    n = ref.shape[-2] // 2
    idx = (slice(None),) * (len(ref.shape) - 2) + (pl.ds(which * n, n), slice(None))
    return ref.at[idx]


_ANY = pl.BlockSpec(memory_space=pl.ANY)


def exchange_halves(full, name):
    T = len(full)

    def body(*refs):
        ins, outs = refs[:T], refs[T:2 * T]
        send_sems, recv_sems = refs[2 * T:]
        x, y, c = _place()
        cps = [pltpu.make_async_remote_copy(
            src_ref=_half(ins[t], 1 - c), dst_ref=outs[t], send_sem=send_sems.at[t], recv_sem=recv_sems.at[t],
            device_id=(x, y, 1 - c), device_id_type=MESH) for t in range(T)]
        for cp in cps:
            cp.start()
        for cp in cps:
            cp.wait()

    return pl.pallas_call(
        body, name=name,
        out_shape=[_sds((4, f.shape[1] // 2, f.shape[2]), f.dtype) for f in full],
        in_specs=[_ANY] * T, out_specs=[_ANY] * T,
        scratch_shapes=[pltpu.SemaphoreType.DMA((T,)), pltpu.SemaphoreType.DMA((T,))],
        compiler_params=_params(),
    )(*full)


_HBM = pl.BlockSpec(memory_space=pltpu.HBM)
_SEM = pl.BlockSpec(memory_space=pltpu.SEMAPHORE)
_EFFECT = pltpu.SideEffectType.DATAFLOW_SIDE_EFFECTING


def _in_hbm(a):
    return pltpu.with_memory_space_constraint(a, pltpu.HBM)


def _split_params():
    return pltpu.CompilerParams(has_side_effects=_EFFECT, vmem_limit_bytes=VMEM_LIMIT)


def _gather_copy(bufs, t, j, chip_of_data, to, send_sems, recv_sems, c):
    return pltpu.make_async_remote_copy(
        src_ref=_half(bufs[t].at[_my_chip()], c), dst_ref=_half(bufs[t].at[chip_of_data], c),
        send_sem=send_sems.at[3 * t + j], recv_sem=recv_sems.at[3 * t + j], device_id=to, device_id_type=MESH)


def gather_start(bufs, after, name):
    T = len(bufs)

    def body(*refs):
        ins, send_sems, recv_sems, token = refs[:T], refs[T + 1], refs[T + 2], refs[-1]
        x, y, c = _place()
        for j, chip in enumerate(_other_chips(x, y)):
            for t in range(T):
                _gather_copy(ins, t, j, _my_chip(), (*chip, c), send_sems, recv_sems, c).start()
        token[...] = jnp.zeros(token.shape, token.dtype)

    sem = pltpu.SemaphoreType.DMA((3 * T,))
    res = pl.pallas_call(
        body, name=name,
        out_shape=(sem, sem, *[pltpu.HBM(b.shape, b.dtype) for b in bufs], _sds((8, LANES), F32)),
        in_specs=[_HBM] * T + [_ANY], out_specs=(_SEM, _SEM, *[_HBM] * T, pl.BlockSpec(memory_space=pltpu.VMEM)),
        input_output_aliases={t: 2 + t for t in range(T)}, compiler_params=_split_params(),
    )(*[_in_hbm(b) for b in bufs], after)
    return res[0], res[1], list(res[2:2 + T]), res[-1]


def gather_wait(bufs, send_sems, recv_sems, after, name):
    T = len(bufs)

    def body(*refs):
        ins, ssem, rsem = refs[:T], refs[T], refs[T + 1]
        x, y, c = _place()
        for j, (cx, cy) in enumerate(_other_chips(x, y)):
            for t in range(T):
                cp = _gather_copy(ins, t, j, 2 * cx + cy, (x, y, c), ssem, rsem, c)
                cp.wait_send()
                cp.wait_recv()

    return pl.pallas_call(
        body, name=name, out_shape=[pltpu.HBM(b.shape, b.dtype) for b in bufs],
        in_specs=[_HBM] * T + [_SEM, _SEM, _ANY], out_specs=[_HBM] * T,
        input_output_aliases={t: t for t in range(T)}, compiler_params=_split_params(),
    )(*bufs, send_sems, recv_sems, after)


def gather_pass(bufs, name):
    T = len(bufs)

    def body(*refs):
        ins, outs = refs[:T], refs[T:2 * T]
        send_sems, recv_sems = refs[2 * T:]
        x, y, c = _place()
        chips = _other_chips(x, y)

        def d2d(t, j, chip_of_data, which):
            return pltpu.make_async_remote_copy(
                src_ref=_half(ins[t].at[chip_of_data], which), dst_ref=_half(outs[t].at[chip_of_data], which),
                send_sem=send_sems.at[t, j], recv_sem=recv_sems.at[t, j], device_id=(x, y, 1 - c),
                device_id_type=MESH)

        passed = [d2d(t, j, 2 * cx + cy, c) for j, (cx, cy) in enumerate(chips) for t in range(T)]
        for cp in passed:
            cp.start()
        for j, (cx, cy) in enumerate(chips):
            for t in range(T):
                d2d(t, j, 2 * cx + cy, 1 - c).wait_recv()
        for cp in passed:
            cp.wait_send()

    sem = lambda: pltpu.SemaphoreType.DMA((T, 3))
    return pl.pallas_call(
        body, name=name, out_shape=[_sds(b.shape, b.dtype) for b in bufs],
        in_specs=[_ANY] * T, out_specs=[_ANY] * T, input_output_aliases={t: t for t in range(T)},
        scratch_shapes=[sem(), sem()], compiler_params=_params(),
    )(*bufs)


def _scatter_copy(sums, lands, t, j, chip_xy, c, send_sems, recv_sems):
    cx, cy = chip_xy
    return pltpu.make_async_remote_copy(
        src_ref=sums[t].at[2 * cx + cy], dst_ref=lands[t].at[j], send_sem=send_sems.at[3 * t + j],
        recv_sem=recv_sems.at[3 * t + j], device_id=(cx, cy, c), device_id_type=MESH)


def scatter_start(sums, after, name):
    T = len(sums)
    lands = [lax.empty((3,) + s.shape[1:], s.dtype) for s in sums]

    def body(*refs):
        s_in, l_in = refs[:T], refs[T:2 * T]
        send_sems, recv_sems, token = refs[2 * T + 1], refs[2 * T + 2], refs[-1]
        x, y, c = _place()
        for j, chip in enumerate(_other_chips(x, y)):
            for t in range(T):
                _scatter_copy(s_in, l_in, t, j, chip, c, send_sems, recv_sems).start()
        token[...] = jnp.zeros(token.shape, token.dtype)

    sem = pltpu.SemaphoreType.DMA((3 * T,))
    both = list(sums) + lands
    res = pl.pallas_call(
        body, name=name,
        out_shape=(sem, sem, *[pltpu.HBM(b.shape, b.dtype) for b in both], _sds((8, LANES), F32)),
        in_specs=[_HBM] * (2 * T) + [_ANY],
        out_specs=(_SEM, _SEM, *[_HBM] * (2 * T), pl.BlockSpec(memory_space=pltpu.VMEM)),
        input_output_aliases={t: 2 + t for t in range(2 * T)}, compiler_params=_split_params(),
    )(*[_in_hbm(b) for b in both], after)
    return res[0], res[1], list(res[2:2 + T]), list(res[2 + T:2 + 2 * T]), res[-1]


def scatter_wait(sums, lands, send_sems, recv_sems, after, name):
    T = len(sums)

    def body(*refs):
        s_in, l_in, ssem, rsem = refs[:T], refs[T:2 * T], refs[2 * T], refs[2 * T + 1]
        x, y, c = _place()
        for j, chip in enumerate(_other_chips(x, y)):
            for t in range(T):
                cp = _scatter_copy(s_in, l_in, t, j, chip, c, ssem, rsem)
                cp.wait_send()
                cp.wait_recv()

    both = list(sums) + list(lands)
    res = pl.pallas_call(
        body, name=name, out_shape=[pltpu.HBM(b.shape, b.dtype) for b in both],
        in_specs=[_HBM] * (2 * T) + [_SEM, _SEM, _ANY], out_specs=[_HBM] * (2 * T),
        input_output_aliases={t: t for t in range(2 * T)}, compiler_params=_split_params(),
    )(*both, send_sems, recv_sems, after)
    return list(res[:T]), list(res[T:])


def join_halves(bufs, name):
    G = len(bufs)
    layers = [(g, l) for g in range(G) for l in range(bufs[g].shape[0])]
    T = len(layers)

    def body(*refs):
        ins, outs = refs[:G], refs[G:2 * G]
        send_sems, recv_sems = refs[2 * G:]
        x, y, c = _place()
        cps = [pltpu.make_async_remote_copy(
            src_ref=_half(ins[g].at[l], c), dst_ref=_half(outs[g].at[l], c), send_sem=send_sems.at[t],
            recv_sem=recv_sems.at[t], device_id=(x, y, 1 - c), device_id_type=MESH) for t, (g, l) in enumerate(layers)]
        for cp in cps:
            cp.start()
        for cp in cps:
            cp.wait()

    return pl.pallas_call(
        body, name=name, out_shape=[_sds(b.shape, b.dtype) for b in bufs],
        in_specs=[_ANY] * G, out_specs=[_ANY] * G, input_output_aliases={g: g for g in range(G)},
        scratch_shapes=[pltpu.SemaphoreType.DMA((T,)), pltpu.SemaphoreType.DMA((T,))],
        compiler_params=_params(),
    )(*bufs)


def _my_chip():
    return 2 * lax.axis_index("x") + lax.axis_index("y")


def add_halves(full, got, name):
    nb, R, C = full.shape
    rh = R // 2
    tr = _fit(512, [rh])
    per = rh // tr

    def body(a_ref, b_ref, o_ref):
        o_ref[...] = (a_ref[...].astype(F32) + b_ref[...].astype(F32)).astype(o_ref.dtype)

    return pl.pallas_call(
        body, name=name, grid=(nb, per),
        in_specs=[pl.BlockSpec((None, tr, C), lambda b, i: (b, lax.axis_index("c") * per + i, 0)),
                  pl.BlockSpec((None, tr, C), lambda b, i: (b, i, 0))],
        out_specs=pl.BlockSpec((None, tr, C), lambda b, i: (b, i, 0)),
        out_shape=_sds((nb, rh, C), BF16),
        compiler_params=_params(("parallel", "parallel")),
    )(full, got)


def finish_sum(sums, got, stacked, n_layers, l, name):
    nb, rh, C = sums.shape
    tr = _fit(512, [rh])
    per = rh // tr

    def body(s_ref, g_ref, *rest):
        o_ref = rest[-1]
        acc = s_ref[...].astype(F32)
        for j in range(3):
            acc = acc + g_ref[j].astype(F32)
        o_ref[...] = acc

    in_specs = [pl.BlockSpec((None, tr, C), lambda i: (_my_chip(), i, 0)),
                pl.BlockSpec((3, tr, C), lambda i: (0, i, 0))]
    args = [sums, got]
    aliases = {}
    if stacked is not None:
        in_specs.append(_ANY)
        args.append(stacked)
        aliases = {2: 0}
    return pl.pallas_call(
        body, name=name, grid=(per,), in_specs=in_specs,
        out_specs=pl.BlockSpec((None, tr, C), lambda i: (l, lax.axis_index("c") * per + i, 0)),
        out_shape=_sds((n_layers, 2 * rh, C), F32), input_output_aliases=aliases,
        compiler_params=_params(("arbitrary",)),
    )(*args)


def sum_devices(gathered, name):
    n_dev, M, N = gathered.shape

    def body(g_ref, o_ref):
        acc = g_ref[0]
        for d in range(1, n_dev):
            acc = acc + g_ref[d]
        o_ref[...] = acc

    tr = 8
    return pl.pallas_call(
        body, name=name, grid=(M // tr,),
        in_specs=[pl.BlockSpec((n_dev, tr, N), lambda i: (0, i, 0))], out_specs=pl.BlockSpec((tr, N), lambda i: (i, 0)),
        out_shape=_sds((M, N), F32), compiler_params=_params(("parallel",)),
    )(gathered)


def _pack(arrays):
    flat = jnp.concatenate([a.reshape(-1).astype(F32) for a in arrays])
    unit = 8 * PACK_COLS
    pad = (-flat.shape[0]) % unit
    return jnp.pad(flat, (0, pad)).reshape(-1, PACK_COLS)


def _unpack(packed, shapes):
    flat = packed.reshape(packed.shape[:-2] + (-1,))
    out, off = [], 0
    for s in shapes:
        n = math.prod(s)
        out.append(flat[..., off:off + n].reshape(packed.shape[:-2] + tuple(s)))
        off += n
    return out


def to_bf16(w, l, after, name):
    _, R, C = w.shape
    tr = _fit(512, [R])

    def body(w_ref, after_ref, o_ref):
        o_ref[...] = w_ref[...].astype(BF16)

    return pl.pallas_call(
        body, name=name, grid=(R // tr,),
        in_specs=[pl.BlockSpec((None, tr, C), lambda i: (l, i, 0)), _ANY],
        out_specs=pl.BlockSpec((None, tr, C), lambda i: (_my_chip(), i, 0)),
        out_shape=_sds((4, R, C), BF16),
        compiler_params=_params(("parallel",)),
    )(w, after)


def kernel(x, c, ada_w, ada_b, norm_g, mlp_w1, mlp_w2, gmlp_w_in, gmlp_ln_g, gmlp_ln_b, gmlp_ws, gmlp_bs, gmlp_w_out, kv_norm_g, kv_ada_w, kv_ada_b, w_kv, k_norm_g, w_f, b_f, attn_wq, q_norm_g, attn_wo, loss_target, m_ada_w, m_ada_b, m_norm_g, m_mlp_w1, m_mlp_w2, m_gmlp_w_in, m_gmlp_ln_g, m_gmlp_ln_b, m_gmlp_ws, m_gmlp_bs, m_gmlp_w_out, m_kv_norm_g, m_kv_ada_w, m_kv_ada_b, m_w_kv, m_k_norm_g, m_w_f, m_b_f, m_attn_wq, m_q_norm_g, m_attn_wo, v_ada_w, v_ada_b, v_norm_g, v_mlp_w1, v_mlp_w2, v_gmlp_w_in, v_gmlp_ln_g, v_gmlp_ln_b, v_gmlp_ws, v_gmlp_bs, v_gmlp_w_out, v_kv_norm_g, v_kv_ada_w, v_kv_ada_b, v_w_kv, v_k_norm_g, v_w_f, v_b_f, v_attn_wq, v_q_norm_g, v_attn_wo):
    given = dict(locals())
    S, D = x.shape[1], x.shape[2]
    depth = ada_w.shape[0]
    n_a = gmlp_w_in.shape[0]
    H = D // HEAD_DIM
    G = gmlp_ws.shape[1]
    half = gmlp_w_out.shape[1] * 4
    n_dev = 8
    tq = min(1024, S // 4)
    ax, ay, ac = _place()
    chip = 2 * ax + ay
    me = 4 * ax + 2 * ay + ac
    row = lambda v: v.reshape(1, -1)
    x0 = x[0]
    tgt = loss_target[0]

    small_in = [c, w_f, norm_g, gmlp_ln_g, gmlp_ln_b]
    g1 = allgather_small(_pack(small_in), "gather_small_params").reshape(n_dev, -1, PACK_COLS)
    c_all, wf_all, ng_all, lg_all, lb_all = _unpack(g1, [a.shape for a in small_in])
    c_all = c_all[:, 0, :]
    per_chip = lambda a: [a[2 * j] for j in range(4)]
    w_f_full = jnp.concatenate(per_chip(wf_all), axis=0)
    norm_g_full = jnp.concatenate(per_chip(ng_all), axis=-1)
    ln_g_full = jnp.concatenate(per_chip(lg_all), axis=-1)
    ln_b_full = jnp.concatenate(per_chip(lb_all), axis=-1)
    w_f_pad = jnp.pad(w_f_full, ((0, 0), (0, LANES - H))).astype(BF16)
    b_f_pad = jnp.pad(b_f, (0, LANES - H)).reshape(1, LANES)

    sc_all = rowmap(lambda cb: cb * jax.nn.sigmoid(cb), [c_all], [], [(D, F32)], tr=8, name="silu")[0]
    n_loc = ada_w.shape[2]
    ada_b_loc = lax.dynamic_slice_in_dim(ada_b, chip * n_loc, n_loc, axis=1).reshape(1, -1)
    add_bias = lambda acc, b: (acc + b,)
    mod_loc = mm(sc_all, ada_w, outs=[(F32, None)], epilogue=add_bias, extras=[ada_b_loc], name="ada_mod")[0]
    kv_loc_n = kv_ada_w.shape[1]
    kv_b_loc = lax.dynamic_slice_in_dim(kv_ada_b, chip * kv_loc_n, kv_loc_n).reshape(1, -1)
    kvmod_loc = mm(sc_all, kv_ada_w, outs=[(F32, None)], epilogue=add_bias, extras=[kv_b_loc], name="kv_ada_mod")[0]
    g2 = allgather_small(_pack([mod_loc, kvmod_loc]), "gather_mod").reshape(n_dev, -1, PACK_COLS)
    mod_all, kvmod_all = _unpack(g2, [mod_loc.shape, kvmod_loc.shape])
    mod_me = jnp.concatenate(
        [lax.dynamic_index_in_dim(m, me, 0, keepdims=False).reshape(depth, n_loc) for m in per_chip(mod_all)], axis=1)
    kvmod_me = jnp.concatenate([lax.dynamic_index_in_dim(m, me, 0, keepdims=False) for m in per_chip(kvmod_all)])
    mods = [[row(v) for v in jnp.split(mod_me[l], N_MOD)] for l in range(depth)]
    kv_shift, kv_scale = [row(v) for v in jnp.split(kvmod_me, 2)]

    big = ['mlp_w1', 'mlp_w2', 'gmlp_w_in', 'gmlp_w_out', 'w_kv', 'attn_wq', 'attn_wo']
    stacked = {n: (given[n] if given[n].ndim == 3 else given[n][None]) for n in big}
    owner = [(n, l) for n in big for l in range(stacked[n].shape[0])]
    W = {}

    def group(g):
        l = g // 2
        if g % 2:
            return [('mlp_w1', l), ('mlp_w2', l)]
        if l < n_a:
            return [('gmlp_w_in', l), ('gmlp_w_out', l)]
        b = l - n_a
        return ([('w_kv', 0)] if b == 0 else []) + [('attn_wq', b), ('attn_wo', b)]

    def start_gather(g, after):
        return gather_start([bufs[k] for k in group(g)], after, f"gather_start_{g}")

    def finish_gather(g, pending, after):
        send_sems, recv_sems, thru, _ = pending
        done = gather_wait(thru, send_sems, recv_sems, after, f"gather_wait_{g}")
        W.update(zip(group(g), gather_pass(done, f"gather_pass_{g}")))

    cast = lambda o, after: to_bf16(stacked[o[0]], o[1], after, f"cast_{o[0]}_{o[1]}")
    bufs = {o: cast(o, c) for o in group(0)}
    pending = start_gather(0, c)
    bufs.update({o: cast(o, pending[3]) for o in owner if o not in bufs})
    finish_gather(0, pending, bufs[group(1)[0]])
    rows_of = lambda a: a.reshape(-1, a.shape[-1])
    W1 = lambda l: W['mlp_w1', l]
    W2 = lambda l: rows_of(W['mlp_w2', l])
    Win = lambda a: W['gmlp_w_in', a]
    Wout = lambda a: rows_of(W['gmlp_w_out', a])
    Wkv = lambda: W['w_kv', 0]
    Wq = lambda b: rows_of(W['attn_wq', b])
    Wo = lambda b: rows_of(W['attn_wo', b])
    bsb = jnp.broadcast_to(gmlp_bs[..., None], gmlp_bs.shape + (LANES,))

    def resid(acc, xr, gate):
        return xr + gate * acc, acc

    saved = []
    xs = x0
    kv = None
    for l in range(depth):
        sh1, sc1, gt1, sh2, sc2, gt2 = mods[l]
        ng0, ng1 = row(norm_g_full[l, 0]), row(norm_g_full[l, 1])
        pending = start_gather(2 * l + 1, W[group(2 * l)[0]])
        sh1 = sh1 + pending[3][0:1, 0:1]
        st = {"x": xs}
        h1 = modnorm_fwd(xs, ng0, sc1, sh1, f"norm1_{l}")
        st["h1"] = h1
        if l < n_a:
            t, z = mm(h1, Win(l), outs=[(BF16, None), (BF16, None)], epilogue=lambda acc: (acc, _gelu(acc)),
                      name=f"gmlp_in_{l}")
            vn = gmlp_ln_fwd(z, row(ln_g_full[l]), row(ln_b_full[l]), f"gmlp_ln_{l}")
            p = gmlp_mix_fwd(z, vn, gmlp_ws[l], bsb[l], f"gmlp_mix_{l}")
            x1, y = mm(p, Wout(l), outs=[(F32, None), (BF16, None)], epilogue=resid, extras=[xs, gt1],
                       name=f"gmlp_out_{l}")
            st.update(t=t, z=z, vn=vn, p=p, y=y)
        else:
            if kv is None:
                hk = modnorm_fwd(xs, row(kv_norm_g), kv_scale, kv_shift, "kv_norm")
                kg = row(k_norm_g)
                kp, kk = mm(hk, Wkv()[:2], outs=[(BF16, None), (BF16, None)], extras=[kg],
                            epilogue=lambda acc, g: (acc, _head_norm(acc, g)), name="kv_k")
                vv = mm(hk, Wkv()[2:], outs=[(BF16, None)], name="kv_v")[0]
                zf, logf = mm(hk, w_f_pad, outs=[(F32, None), (F32, None)], extras=[b_f_pad],
                              epilogue=lambda acc, b: (acc + b, jax.nn.log_sigmoid(acc + b)), name="kv_f")
                fc, fk = fcum_fwd(logf, tq, "fcum")
                kv = dict(x=xs, hk=hk, kp=kp, k=kk, v=vv, zf=zf, fc=fc, fk=fk)
            b = l - n_a
            qp, q = mm(h1, Wq(b), outs=[(BF16, None), (BF16, None)], extras=[row(q_norm_g[b])],
                       epilogue=lambda acc, g: (acc, _head_norm(acc, g) * (1.0 / math.sqrt(HEAD_DIM))),
                       name=f"attn_q_{l}")
            o, lse = fox_fwd(q, kv["k"], kv["v"], kv["fk"], tq, f"fox_fwd_{l}")
            x1, y = mm(o, Wo(b), outs=[(F32, None), (BF16, None)], epilogue=resid, extras=[xs, gt1],
                       name=f"attn_o_{l}")
            st.update(qp=qp, q=q, o=o, lse=lse, y=y)
        st["x1"] = x1
        finish_gather(2 * l + 1, pending, x1)
        if l + 1 < depth:
            pending = start_gather(2 * l + 2, W[group(2 * l + 1)[0]])
            sh2 = sh2 + pending[3][0:1, 0:1]
        h2 = modnorm_fwd(x1, ng1, sc2, sh2, f"norm2_{l}")
        a_pre, a_sq = mm(h2, W1(l), outs=[(BF16, None), (BF16, None)],
                         epilogue=lambda acc: (acc, jnp.square(jnp.maximum(acc, 0.0))), name=f"mlp_up_{l}")
        xs, y2 = mm(a_sq, W2(l), outs=[(F32, None), (BF16, None)], epilogue=resid, extras=[x1, gt2],
                    name=f"mlp_down_{l}")
        st.update(h2=h2, a=a_pre, a_sq=a_sq, y2=y2)
        saved.append(st)
        if l + 1 < depth:
            finish_gather(2 * l + 2, pending, xs)

    def loss_fn(yb, tb_):
        e = yb - tb_
        return e * (1.0 / D), _colsum(e * e)
    dx, sq = rowmap(loss_fn, [xs, tgt], [], [(D, F32)], reds=[(1, D)], tr=512, name="loss")
    loss = lax.psum(0.5 * jnp.sum(sq) / D, ("x", "y", "c"))

    big_grads = {n: [None] * stacked[n].shape[0] for n in big}
    dmod = [None] * depth
    d_norm_g = [[None, None] for _ in range(depth)]
    small = {}
    d_ws, d_bs, d_lg, d_lb, d_qg = [None] * n_a, [None] * n_a, [None] * n_a, [None] * n_a, [None] * (depth - n_a)
    dk_parts, dv_parts, dfk_parts = [], [], []
    quarter = lambda g: g.reshape(4, g.shape[0] // 4, g.shape[1])
    stacks = {n: None for n in big}

    def start_scatter(g, after):
        flat = [big_grads[n][i] for n, i in group(g)]
        got = exchange_halves(flat, f"grad_exchange_halves_{g}")
        sums = [add_halves(f, h, f"grad_chip_sum_{n}_{i}") for f, h, (n, i) in zip(flat, got, group(g))]
        return scatter_start(sums, after, f"grad_scatter_start_{g}")

    def finish_scatter(g, pend, after):
        send_sems, recv_sems, sums, lands, _ = pend
        sums, lands = scatter_wait(sums, lands, send_sems, recv_sems, after, f"grad_scatter_wait_{g}")
        for s, h, (n, i) in zip(sums, lands, group(g)):
            stacks[n] = finish_sum(s, h, stacks[n], stacked[n].shape[0], i, f"grad_finish_{n}_{i}")

    scattering = None
    for l in reversed(range(depth)):
        sh1, sc1, gt1, sh2, sc2, gt2 = mods[l]
        ng0, ng1 = row(norm_g_full[l, 0]), row(norm_g_full[l, 1])
        st = saved[l]
        if scattering is not None:
            gt2 = gt2 + scattering[4][0:1, 0:1]
        dy2, dgt2 = gate_bwd(dx, st["y2"], gt2, f"gate2_bwd_{l}")
        big_grads['mlp_w2'][l] = quarter(mm(st["a_sq"], dy2, ta=True, outs=[(BF16, None)], name=f"mlp_w2_grad_{l}")[0])
        da = mm(dy2, W2(l), tb=True, outs=[(BF16, None)], extras=[st["a"]],
                epilogue=lambda acc, a: (acc * (2.0 * jnp.maximum(a.astype(F32), 0.0)),), name=f"mlp_down_bwd_{l}")[0]
        big_grads['mlp_w1'][l] = mm(st["h2"], da, ta=True, outs=[(BF16, 4)], name=f"mlp_w1_grad_{l}")[0]
        dh2 = mm(da, W1(l), tb=True, outs=[(BF16, None)], name=f"mlp_up_bwd_{l}")[0]
        if scattering is not None:
            finish_scatter(2 * l + 2, scattering, dh2)
        scattering = start_scatter(2 * l + 1, dh2)
        sc2 = sc2 + scattering[4][0:1, 0:1]
        dx, dsh2, d_norm_g[l][1], dsc2 = modnorm_bwd(st["x1"], dh2, dx, ng1, sc2, f"norm2_bwd_{l}")
        dy, dgt1 = gate_bwd(dx, st["y"], gt1, f"gate1_bwd_{l}")
        if l < n_a:
            big_grads['gmlp_w_out'][l] = quarter(
                mm(st["p"], dy, ta=True, outs=[(BF16, None)], name=f"gmlp_w_out_grad_{l}")[0])
            dp = mm(dy, Wout(l), tb=True, outs=[(BF16, None)], name=f"gmlp_out_bwd_{l}")[0]
            du, dvn, d_ws[l], db = gmlp_mix_bwd(dp, st["z"], st["vn"], gmlp_ws[l], bsb[l], f"gmlp_mix_bwd_{l}")
            d_bs[l] = db[:, :, 0]
            dt, d_lg[l], d_lb[l] = gmlp_act_bwd(st["t"], du, dvn, row(ln_g_full[l]), f"gmlp_act_bwd_{l}")
            big_grads['gmlp_w_in'][l] = mm(st["h1"], dt, ta=True, outs=[(BF16, 4)], name=f"gmlp_w_in_grad_{l}")[0]
            dh1 = mm(dt, Win(l), tb=True, outs=[(BF16, None)], name=f"gmlp_in_bwd_{l}")[0]
        else:
            b = l - n_a
            big_grads['attn_wo'][b] = quarter(
                mm(st["o"], dy, ta=True, outs=[(BF16, None)], name=f"attn_wo_grad_{l}")[0])
            do = mm(dy, Wo(b), tb=True, outs=[(BF16, None)], name=f"attn_o_bwd_{l}")[0]
            dq, rowterm = fox_dq(st["q"], kv["k"], kv["v"], do, st["lse"], kv["fk"], tq, f"fox_dq_{l}")
            dk_l, dv_l, dfk_l = fox_dkv(st["q"], kv["k"], kv["v"], do, rowterm, st["lse"], kv["fk"], tq, f"fox_dkv_{l}")
            dk_parts.append(dk_l)
            dv_parts.append(dv_l)
            dfk_parts.append(jnp.pad(dfk_l[:, :, 0, :].reshape(H, S), ((0, LANES - H), (0, 0))))
            dqp, d_qg[b] = head_norm_bwd(st["qp"], dq, row(q_norm_g[b]), f"q_norm_bwd_{l}")
            big_grads['attn_wq'][b] = quarter(
                mm(st["h1"], dqp, ta=True, outs=[(BF16, None)], name=f"attn_wq_grad_{l}")[0])
            dh1 = mm(dqp, Wq(b), tb=True, outs=[(BF16, None)], name=f"attn_q_bwd_{l}")[0]
        dx, dsh1, d_norm_g[l][0], dsc1 = modnorm_bwd(st["x"], dh1, dx, ng0, sc1, f"norm1_bwd_{l}")
        dmod[l] = jnp.concatenate([dsh1, dsc1, dgt1, dsh2, dsc2, dgt2], axis=1)
        if l == n_a:
            add2 = lambda a, b_: a.astype(F32) + b_.astype(F32)
            dk_sum = rowmap(add2, dk_parts, [], [(D, BF16)], tr=512, name="dk_sum")[0]
            dv_sum = rowmap(add2, dv_parts, [], [(D, BF16)], tr=512, name="dv_sum")[0]
            dkvp, small['k_norm_g'] = head_norm_bwd(kv["kp"], dk_sum, row(k_norm_g), "k_norm_bwd", extra=dv_sum)
            dzf, db_f = fcum_bwd(dfk_parts[0], dfk_parts[1], kv["zf"], H, "fcum_bwd")
            small['b_f'] = db_f[0, :H]
            big_grads['w_kv'][0] = mm(kv["hk"], dkvp, ta=True, outs=[(BF16, 4)], name="w_kv_grad")[0]
            small['w_f'] = mm(kv["hk"], dzf, ta=True, outs=[(F32, None)], name="w_f_grad")[0][:, :H]
            dhk_f = mm(dzf, w_f_pad, tb=True, outs=[(BF16, None)], name="kv_f_bwd")[0]
            dhk = mm(dkvp, Wkv(), tb=True, outs=[(BF16, None)], extras=[dhk_f],
                     epilogue=lambda acc, e: (acc + e.astype(F32),), name="kv_bwd")[0]
            dx, dkv_shift, small['kv_norm_g'], dkv_scale = modnorm_bwd(
                kv["x"], dhk, dx, row(kv_norm_g), kv_scale, "kv_norm_bwd")
            dkvmod = jnp.concatenate([dkv_shift, dkv_scale], axis=1)
        finish_scatter(2 * l + 1, scattering, dx)
        scattering = start_scatter(2 * l, dx) if l > 0 else None
    grad_x = dx[None]

    small['norm_g'] = jnp.stack([jnp.concatenate(p, axis=0) for p in d_norm_g])
    small['gmlp_ln_g'] = jnp.concatenate(d_lg, axis=0)
    small['gmlp_ln_b'] = jnp.concatenate(d_lb, axis=0)
    small['gmlp_ws'] = jnp.stack(d_ws)
    small['gmlp_bs'] = jnp.stack(d_bs)
    small['q_norm_g'] = jnp.concatenate(d_qg, axis=0)
    small['ada_b'] = jnp.concatenate(dmod, axis=0)
    small['kv_ada_b'] = dkvmod
    names = sorted(small)
    shapes = [small[n].shape for n in names]
    g3 = allgather_small(_pack([small[n] for n in names]), "gather_small_grads").reshape(n_dev, -1, PACK_COLS)
    summed = dict(zip(names, _unpack(sum_devices(g3, "sum_small_grads"), shapes)))
    each = dict(zip(names, _unpack(g3, shapes)))
    local_cols = lambda a, n: lax.dynamic_slice_in_dim(a, chip * n, n, axis=a.ndim - 1)
    grads = {
        'ada_b': summed['ada_b'], 'gmlp_ws': summed['gmlp_ws'], 'gmlp_bs': summed['gmlp_bs'],
        'kv_norm_g': summed['kv_norm_g'].reshape(-1), 'kv_ada_b': summed['kv_ada_b'].reshape(-1),
        'k_norm_g': summed['k_norm_g'].reshape(-1), 'b_f': summed['b_f'], 'q_norm_g': summed['q_norm_g'],
        'norm_g': local_cols(summed['norm_g'], norm_g.shape[2]),
        'gmlp_ln_g': local_cols(summed['gmlp_ln_g'], gmlp_ln_g.shape[1]),
        'gmlp_ln_b': local_cols(summed['gmlp_ln_b'], gmlp_ln_b.shape[1]),
        'w_f': lax.dynamic_slice_in_dim(summed['w_f'], chip * w_f.shape[0], w_f.shape[0], axis=0),
    }
    dmod_all = each['ada_b'].reshape(n_dev, depth, N_MOD * D)
    dm_loc = jnp.transpose(local_cols(dmod_all, n_loc), (1, 0, 2))
    dkv_loc = local_cols(each['kv_ada_b'].reshape(n_dev, 2 * D), kv_loc_n)[None]
    scattering = start_scatter(0, g3)
    dm_loc = dm_loc + scattering[4][0:1, 0:1]

    delta, new_m, new_v = {}, {}, {}
    sct = jnp.transpose(sc_all)
    grads['ada_w'], delta['ada_w'], new_m['ada_w'], new_v['ada_w'] = adamw_outer(
        ada_w, sct, dm_loc, m_ada_w, v_ada_w, "adamw_ada_w")
    r = adamw_outer(kv_ada_w[None], sct, dkv_loc, m_kv_ada_w[None], v_kv_ada_w[None], "adamw_kv_ada_w")
    grads['kv_ada_w'], delta['kv_ada_w'], new_m['kv_ada_w'], new_v['kv_ada_w'] = [a[0] for a in r]
    for n in WEIGHTS:
        if n in delta or n in big:
            continue
        grads[n] = grads[n].reshape(given[n].shape)
        delta[n], new_m[n], new_v[n] = adamw(given[n], grads[n], given["m_" + n], given["v_" + n], "adamw_" + n)

    done = sum(new_v[n][(0,) * new_v[n].ndim] for n in WEIGHTS if n not in big).reshape(1, 1)
    finish_scatter(0, scattering, done)
    joined = join_halves([stacks[n] for n in big], "grad_join")
    for n, g in zip(big, joined):
        delta[n], new_m[n], new_v[n], grads[n] = adamw(
            given[n], g.reshape(given[n].shape), given["m_" + n], given["v_" + n], "adamw_" + n, with_grad=True)
    return (loss, grad_x, *[grads[n] for n in WEIGHTS], *[delta[n] for n in WEIGHTS],
            *[new_m[n] for n in WEIGHTS], *[new_v[n] for n in WEIGHTS])
```
